```python
import math
import jax, jax.numpy as jnp
from jax import lax
import numpy as np

D_MODEL = 1024
BATCH = 8
SEQ = 4096
DEPTH = 1

MIX_WIDTH = D_MODEL
HEAD_GROUP_DIM = 64
MLA_HEADS = 8
MLA_NOPE_DIM = 64
MLA_ROPE_DIM = 32
MLA_V_DIM = HEAD_GROUP_DIM
MLA_Q_RANK = 256
MLA_KV_RANK = 128
ROPE_THETA = 10000.0
Q_BLOCK = 128
HYENA_CHANNELS = MIX_WIDTH - MLA_HEADS * MLA_V_DIM
HYENA_GROUPS = HYENA_CHANNELS // HEAD_GROUP_DIM
HYENA_ORDER = 2
SHORT_CONV = 3
FILTER_BANDS = 16
FILTER_EMB = 2 * FILTER_BANDS + 1
FILTER_HIDDEN = 64
DECAY_TARGET = 1e-2
FAST_DECAY_PCT = 0.3
SLOW_DECAY_PCT = 1.5
MAX_DECAY = math.log(DECAY_TARGET) / FAST_DECAY_PCT
MIN_DECAY = math.log(DECAY_TARGET) / SLOW_DECAY_PCT
N_GROUPS = MLA_HEADS + HYENA_GROUPS
IN_COLS = MLA_Q_RANK + MLA_KV_RANK + MLA_ROPE_DIM + (HYENA_ORDER + 1) * HYENA_CHANNELS
IN_SPLITS = (MLA_Q_RANK, MLA_Q_RANK + MLA_KV_RANK, MLA_Q_RANK + MLA_KV_RANK + MLA_ROPE_DIM)
FFN_HIDDEN = 2816
FFN_RES = 0.5
EPS = 1e-6

kernel_name = 'hymba_mla_hyena_macaron'


def rmsnorm(x, g):
    xf = x.astype(jnp.float32)
    y = xf * lax.rsqrt(jnp.mean(xf * xf, axis=-1, keepdims=True) + EPS)
    return y.astype(x.dtype) * g


def swiglu(x, w_gate, w_up, w_down):
    return (jax.nn.silu(x @ w_gate) * (x @ w_up)) @ w_down


def rope_tables(seq):
    inv = 1.0 / (ROPE_THETA ** (jnp.arange(0, MLA_ROPE_DIM, 2, dtype=jnp.float32) / MLA_ROPE_DIM))
    ang = jnp.arange(seq, dtype=jnp.float32)[:, None] * inv[None, :]
    return jnp.cos(ang), jnp.sin(ang)


def apply_rope(x, cos, sin):
    half = x.shape[-1] // 2
    x1, x2 = x[..., :half], x[..., half:]
    cos = cos.astype(x.dtype)
    sin = sin.astype(x.dtype)
    return jnp.concatenate([x1 * cos - x2 * sin, x1 * sin + x2 * cos], axis=-1)


def mla(c_q, c_kv, k_rope, q_norm_g, w_uq, kv_norm_g, w_ukv):
    B, S, _ = c_q.shape
    q = (rmsnorm(c_q, q_norm_g) @ w_uq).reshape(B, S, MLA_HEADS, MLA_NOPE_DIM + MLA_ROPE_DIM)
    kv = (rmsnorm(c_kv, kv_norm_g) @ w_ukv).reshape(B, S, MLA_HEADS, MLA_NOPE_DIM + MLA_V_DIM)
    q_nope, q_pe = q[..., :MLA_NOPE_DIM], q[..., MLA_NOPE_DIM:]
    k_nope, v = kv[..., :MLA_NOPE_DIM], kv[..., MLA_NOPE_DIM:]
    cos, sin = rope_tables(S)
    q_pe = apply_rope(q_pe, cos[:, None, :], sin[:, None, :])
    k_pe = apply_rope(k_rope, cos, sin)
    scale = (MLA_NOPE_DIM + MLA_ROPE_DIM) ** -0.5
    q = jnp.concatenate([q_nope, q_pe], axis=-1) * scale
    k = jnp.concatenate([k_nope, jnp.broadcast_to(k_pe[:, :, None, :], (B, S, MLA_HEADS, MLA_ROPE_DIM))], axis=-1)
    n_blk = S // Q_BLOCK
    q_blocks = q.reshape(B, n_blk, Q_BLOCK, MLA_HEADS, q.shape[-1]).transpose(1, 0, 2, 3, 4)

    def attend(qb):
        s = jnp.einsum('bqhd,bkhd->bhqk', qb, k).astype(jnp.float32)
        p = jax.nn.softmax(s, axis=-1).astype(v.dtype)
        return jnp.einsum('bhqk,bkhd->bqhd', p, v)

    out = lax.map(attend, q_blocks)
    return out.transpose(1, 0, 2, 3, 4).reshape(B, S, MLA_HEADS * MLA_V_DIM)


def short_conv(u, w, b):
    S = u.shape[1]
    pad = SHORT_CONV // 2
    up = jnp.pad(u, ((0, 0), (pad, SHORT_CONV - 1 - pad), (0, 0)))
    y = b
    for tap in range(SHORT_CONV):
        y = y + up[:, tap:tap + S] * w[tap]
    return y


def hyena_filters(L, w1, b1, w2, b2, w3, freq):
    f32 = jnp.float32
    pos = jnp.arange(L, dtype=f32)
    t = pos / max(L - 1, 1)
    bands = jnp.linspace(1e-4, FILTER_BANDS - 1, FILTER_BANDS, dtype=f32)
    ang = (2.0 * math.pi * pos / L)[:, None] * bands[None, :]
    z = jnp.concatenate([t[:, None], jnp.cos(ang), -jnp.sin(ang)], axis=-1)
    fr = freq.astype(f32)
    h = jnp.sin(fr * (z @ w1.astype(f32) + b1.astype(f32)))
    h = jnp.sin(fr * (h @ w2.astype(f32) + b2.astype(f32)))
    h = h @ w3.astype(f32)
    deltas = jnp.abs(jnp.linspace(MIN_DECAY, MAX_DECAY, HYENA_CHANNELS, dtype=f32))
    decay = jnp.exp(-t[:, None] * deltas[None, :])
    h = h.reshape(L, HYENA_ORDER, 2, HYENA_CHANNELS) * decay[:, None, None, :]
    h_fwd, h_bwd = h[:, :, 0], h[:, :, 1]
    k = jnp.concatenate([h_fwd, jnp.zeros((1, HYENA_ORDER, HYENA_CHANNELS), f32), h_bwd[:0:-1]], axis=0)
    k = k / jnp.sum(jnp.abs(k), axis=0, keepdims=True)
    return jnp.fft.rfft(k, axis=0)


def fftconv(u, k_f, d):
    L = u.shape[1]
    uf32 = u.astype(jnp.float32)
    u_f = jnp.fft.rfft(uf32, n=2 * L, axis=1)
    y = jnp.fft.irfft(u_f * k_f[None], n=2 * L, axis=1)[:, :L]
    return (y + uf32 * d.astype(jnp.float32)).astype(u.dtype)


def hyena(u, conv_w, conv_b, w1, b1, w2, b2, w3, freq, d_skip):
    L = u.shape[1]
    u = short_conv(u, conv_w, conv_b)
    v, x1, x2 = jnp.split(u, HYENA_ORDER + 1, axis=-1)
    k_f = hyena_filters(L, w1, b1, w2, b2, w3, freq)
    z = fftconv(v, k_f[:, 0], d_skip[0]) * x1
    z = fftconv(z, k_f[:, 1], d_skip[1]) * x2
    return z


def setup_inputs(seed: int = 0) -> dict:
    key = jax.random.key(seed)
    ks = iter(jax.random.split(key, 40))

    def nrm(shape, scale):
        return jax.random.normal(next(ks), shape, jnp.float32) * scale

    def gain(shape):
        return 1.0 + nrm(shape, 0.02)

    L_ = DEPTH
    return {
        'x': nrm((BATCH, SEQ, D_MODEL), 1.0),
        'ffn1_norm_g': gain((L_, D_MODEL)),
        'ffn1_w_gate': nrm((L_, D_MODEL, FFN_HIDDEN), D_MODEL ** -0.5),
        'ffn1_w_up': nrm((L_, D_MODEL, FFN_HIDDEN), D_MODEL ** -0.5),
        'ffn1_w_down': nrm((L_, FFN_HIDDEN, D_MODEL), FFN_HIDDEN ** -0.5),
        'mix_norm_g': gain((L_, D_MODEL)),
        'w_in': nrm((L_, D_MODEL, IN_COLS), D_MODEL ** -0.5),
        'q_norm_g': gain((L_, MLA_Q_RANK)),
        'w_uq': nrm((L_, MLA_Q_RANK, MLA_HEADS * (MLA_NOPE_DIM + MLA_ROPE_DIM)), MLA_Q_RANK ** -0.5),
        'kv_norm_g': gain((L_, MLA_KV_RANK)),
        'w_ukv': nrm((L_, MLA_KV_RANK, MLA_HEADS * (MLA_NOPE_DIM + MLA_V_DIM)), MLA_KV_RANK ** -0.5),
        'hyena_conv_w': nrm((L_, SHORT_CONV, (HYENA_ORDER + 1) * HYENA_CHANNELS), SHORT_CONV ** -0.5),
        'hyena_conv_b': nrm((L_, (HYENA_ORDER + 1) * HYENA_CHANNELS), 0.01),
        'filt_w1': nrm((L_, FILTER_EMB, FILTER_HIDDEN), FILTER_EMB ** -0.5),
        'filt_b1': nrm((L_, FILTER_HIDDEN), 0.01),
        'filt_w2': nrm((L_, FILTER_HIDDEN, FILTER_HIDDEN), FILTER_HIDDEN ** -0.5),
        'filt_b2': nrm((L_, FILTER_HIDDEN), 0.01),
        'filt_w3': nrm((L_, FILTER_HIDDEN, HYENA_ORDER * 2 * HYENA_CHANNELS), FILTER_HIDDEN ** -0.5),
        'filt_freq': gain((L_, FILTER_HIDDEN)),
        'hyena_d': nrm((L_, HYENA_ORDER, HYENA_CHANNELS), 0.1),
        'head_norm_g': gain((L_, MIX_WIDTH)),
        'w_out': nrm((L_, MIX_WIDTH, D_MODEL), MIX_WIDTH ** -0.5),
        'ffn2_norm_g': gain((L_, D_MODEL)),
        'ffn2_w_gate': nrm((L_, D_MODEL, FFN_HIDDEN), D_MODEL ** -0.5),
        'ffn2_w_up': nrm((L_, D_MODEL, FFN_HIDDEN), D_MODEL ** -0.5),
        'ffn2_w_down': nrm((L_, FFN_HIDDEN, D_MODEL), FFN_HIDDEN ** -0.5),
        'final_norm_g': gain((D_MODEL,)),
    }


def reference(x, ffn1_norm_g, ffn1_w_gate, ffn1_w_up, ffn1_w_down, mix_norm_g, w_in,
              q_norm_g, w_uq, kv_norm_g, w_ukv, hyena_conv_w, hyena_conv_b,
              filt_w1, filt_b1, filt_w2, filt_b2, filt_w3, filt_freq, hyena_d,
              head_norm_g, w_out, ffn2_norm_g, ffn2_w_gate, ffn2_w_up, ffn2_w_down,
              final_norm_g):
    B, S, _ = x.shape
    for l in range(DEPTH):
        x = x + FFN_RES * swiglu(rmsnorm(x, ffn1_norm_g[l]), ffn1_w_gate[l], ffn1_w_up[l], ffn1_w_down[l])
        u = rmsnorm(x, mix_norm_g[l]) @ w_in[l]
        c_q, c_kv, k_rope, u_hy = jnp.split(u, IN_SPLITS, axis=-1)
        a = mla(c_q, c_kv, k_rope, q_norm_g[l], w_uq[l], kv_norm_g[l], w_ukv[l])
        h = hyena(u_hy, hyena_conv_w[l], hyena_conv_b[l], filt_w1[l], filt_b1[l],
                  filt_w2[l], filt_b2[l], filt_w3[l], filt_freq[l], hyena_d[l])
        mix = jnp.concatenate([a, h], axis=-1).reshape(B, S, N_GROUPS, HEAD_GROUP_DIM)
        mix = rmsnorm(mix, head_norm_g[l].reshape(N_GROUPS, HEAD_GROUP_DIM)).reshape(B, S, MIX_WIDTH)
        x = x + mix @ w_out[l]
        x = x + FFN_RES * swiglu(rmsnorm(x, ffn2_norm_g[l]), ffn2_w_gate[l], ffn2_w_up[l], ffn2_w_down[l])
    return rmsnorm(x, final_norm_g)
```

```python
import functools
import math

import jax
import jax.numpy as jnp
from jax import lax
from jax.experimental import pallas as pl
from jax.experimental.pallas import tpu as pltpu

F32 = jnp.float32
BF16 = jnp.bfloat16

D_MODEL = 1024
MLA_HEADS = 8
NOPE = 64
ROPE = 32
V_DIM = 64
Q_RANK = 256
KV_RANK = 128
ROPE_THETA = 10000.0
HY_C = 512
FILTER_BANDS = 16
FILTER_HIDDEN = 64
HY_ORDER = 2
FFN_HIDDEN = 2816
FFN_RES = 0.5
EPS = 1e-6
GROUP = 64
DECAY_TARGET = 1e-2
MAX_DECAY = math.log(DECAY_TARGET) / 0.3
MIN_DECAY = math.log(DECAY_TARGET) / 1.5

LANES = 128
HEAD_PAD = 128
VMEM_LIMIT = 56 * 1024 * 1024

TM_FFN = 512
FFN_CHUNK = 1408
TS_PROJ = 512
TQ = 512
TK = 512
CT = 128
P_BLK = 4
RC = 32
TL_FILT = 512


def _cparams(sem):
    return pltpu.CompilerParams(dimension_semantics=sem, vmem_limit_bytes=VMEM_LIMIT)


def _rms(x, g):
    ms = jnp.mean(x * x, axis=-1, keepdims=True)
    return x * lax.rsqrt(ms + EPS) * g


def _swiglu_residual(x, g_ref, wg_ref, wu_ref, wd_ref):
    xn = _rms(x, g_ref[...]).astype(BF16)
    acc = jnp.zeros(x.shape, F32)
    for c0 in range(0, FFN_HIDDEN, FFN_CHUNK):
        gate = jnp.dot(xn, wg_ref[:, c0:c0 + FFN_CHUNK], preferred_element_type=F32)
        up = jnp.dot(xn, wu_ref[:, c0:c0 + FFN_CHUNK], preferred_element_type=F32)
        h = (gate * jax.nn.sigmoid(gate) * up).astype(BF16)
        acc = acc + jnp.dot(h, wd_ref[c0:c0 + FFN_CHUNK, :], preferred_element_type=F32)
    return x + FFN_RES * acc


def _ffn1_kernel(x_ref, g_ref, wg_ref, wu_ref, wd_ref, o_ref):
    o_ref[...] = _swiglu_residual(x_ref[...], g_ref, wg_ref, wu_ref, wd_ref)


def _full(shape):
    return pl.BlockSpec(shape, lambda *_: (0,) * len(shape))


def _ffn1(x2d, g, wg, wu, wd):
    t = x2d.shape[0]
    row = pl.BlockSpec((TM_FFN, D_MODEL), lambda i: (i, 0))
    return pl.pallas_call(
        _ffn1_kernel,
        grid=(t // TM_FFN,),
        in_specs=[row, _full(g.shape), _full(wg.shape), _full(wu.shape), _full(wd.shape)],
        out_specs=row,
        out_shape=jax.ShapeDtypeStruct(x2d.shape, F32),
        compiler_params=_cparams(("parallel",)),
    )(x2d, g, wg, wu, wd)


def _inproj_kernel(x_ref, g_ref, win_ref, qg_ref, wq_ref, wqs_ref, kvg_ref, wk_ref, wv_ref,
                   cq_ref, ck_ref, sn_ref, q_ref, k_ref, v_ref, u_ref, *, scale):
    xn = _rms(x_ref[...], g_ref[...]).astype(BF16)
    u = jnp.dot(xn, win_ref[...], preferred_element_type=F32)
    c_q = u[:, 0:Q_RANK]
    c_kv = u[:, Q_RANK:Q_RANK + KV_RANK]
    o_pe = Q_RANK + KV_RANK
    kpe_a = u[:, o_pe:o_pe + LANES]
    kpe_b = u[:, o_pe + LANES:o_pe + 2 * LANES]
    u_ref[...] = u[:, o_pe + 2 * LANES:]

    cq_t = cq_ref[...]
    ck_t = ck_ref[...]
    sn_t = sn_ref[...]

    cqn = _rms(c_q, qg_ref[...]).astype(BF16)
    q = jnp.dot(cqn, wq_ref[...], preferred_element_type=F32)
    qs = jnp.dot(cqn, wqs_ref[...], preferred_element_type=F32)
    ckn = _rms(c_kv, kvg_ref[...]).astype(BF16)
    kn = jnp.dot(ckn, wk_ref[...], preferred_element_type=F32)
    v_ref[...] = jnp.dot(ckn, wv_ref[...], preferred_element_type=F32).astype(BF16)
    k_pe = kpe_a * ck_t + kpe_b * sn_t
    for h in range(MLA_HEADS):
        sl = slice(h * HEAD_PAD, (h + 1) * HEAD_PAD)
        q_ref[:, sl] = ((q[:, sl] * cq_t + qs[:, sl] * sn_t) * scale).astype(BF16)
        k_ref[:, sl] = (kn[:, sl] + k_pe).astype(BF16)


def _inproj(x1, g, win, qg, wq, wqs, kvg, wk, wv, cq_t, ck_t, sn_t, seq):
    t = x1.shape[0]
    nblk = seq // TS_PROJ
    row = lambda w: pl.BlockSpec((TS_PROJ, w), lambda i: (i, 0))
    tab = pl.BlockSpec((TS_PROJ, LANES), lambda i: (i % nblk, 0))
    scale = (NOPE + ROPE) ** -0.5
    return pl.pallas_call(
        functools.partial(_inproj_kernel, scale=scale),
        grid=(t // TS_PROJ,),
        in_specs=[row(D_MODEL), _full(g.shape), _full(win.shape), _full(qg.shape), _full(wq.shape),
                  _full(wqs.shape), _full(kvg.shape), _full(wk.shape), _full(wv.shape), tab, tab, tab],
        out_specs=[row(MLA_HEADS * HEAD_PAD), row(MLA_HEADS * HEAD_PAD), row(MLA_HEADS * V_DIM),
                   row(3 * HY_C)],
        out_shape=[jax.ShapeDtypeStruct((t, MLA_HEADS * HEAD_PAD), BF16),
                   jax.ShapeDtypeStruct((t, MLA_HEADS * HEAD_PAD), BF16),
                   jax.ShapeDtypeStruct((t, MLA_HEADS * V_DIM), BF16),
                   jax.ShapeDtypeStruct((t, 3 * HY_C), F32)],
        compiler_params=_cparams(("parallel",)),
    )(x1, g, win, qg, wq, wqs, kvg, wk, wv, cq_t, ck_t, sn_t)


def _attn_kernel(q_ref, k_ref, v_ref, o_ref, m_sc, l_sc, acc_sc, out_sc):
    seq = k_ref.shape[1]
    nk = seq // TK
    for h in range(2):
        hs = slice(h * HEAD_PAD, (h + 1) * HEAD_PAD)
        q = q_ref[0, :, hs]
        m_sc[...] = jnp.full(m_sc.shape, -jnp.inf, F32)
        l_sc[...] = jnp.zeros(l_sc.shape, F32)
        acc_sc[...] = jnp.zeros(acc_sc.shape, F32)

        def body(j, carry):
            off = pl.multiple_of(j * TK, TK)
            k = k_ref[0, pl.ds(off, TK), hs]
            v = v_ref[0, pl.ds(off, TK), :]
            s = lax.dot_general(k, q, (((1,), (1,)), ((), ())),
                                preferred_element_type=F32)
            m_prev = m_sc[...]
            m_new = jnp.maximum(m_prev, jnp.max(s, axis=0, keepdims=True))
            alpha = jnp.exp(m_prev - m_new)
            p = jnp.exp(s - m_new)
            l_sc[...] = alpha * l_sc[...] + jnp.sum(p, axis=0, keepdims=True)
            pv = lax.dot_general(v, p.astype(BF16), (((0,), (0,)), ((), ())),
                                 preferred_element_type=F32)
            acc_sc[...] = alpha * acc_sc[...] + pv
            m_sc[...] = m_new
            return carry

        lax.fori_loop(0, nk, body, 0)
        vs = slice(h * V_DIM, (h + 1) * V_DIM)
        out_sc[vs, :] = acc_sc[vs, :] / l_sc[...]
    o_ref[0] = out_sc[...].T


def _attention(q3, k3, v3):
    b, seq, _ = q3.shape
    pairs = MLA_HEADS // 2
    return pl.pallas_call(
        _attn_kernel,
        grid=(b, pairs, seq // TQ),
        in_specs=[pl.BlockSpec((1, TQ, 2 * HEAD_PAD), lambda bi, j, qi: (bi, qi, j)),
                  pl.BlockSpec((1, seq, 2 * HEAD_PAD), lambda bi, j, qi: (bi, 0, j)),
                  pl.BlockSpec((1, seq, 2 * V_DIM), lambda bi, j, qi: (bi, 0, j))],
        out_specs=pl.BlockSpec((1, TQ, 2 * V_DIM), lambda bi, j, qi: (bi, qi, j)),
        out_shape=jax.ShapeDtypeStruct((b, seq, MLA_HEADS * V_DIM), F32),
        scratch_shapes=[pltpu.VMEM((1, TQ), F32), pltpu.VMEM((1, TQ), F32),
                        pltpu.VMEM((2 * V_DIM, TQ), F32), pltpu.VMEM((2 * V_DIM, TQ), F32)],
        compiler_params=_cparams(("parallel", "parallel", "parallel")),
    )(q3, k3, v3)


def _filter_kernel(bands_ref, w1t_ref, w1c_ref, w1s_ref, b1_ref, w2_ref, b2_ref, w3_ref,
                   fr_ref, dl_ref, kk_ref, asum_ref, *, seq):
    i = pl.program_id(0)
    hp = lax.Precision.HIGHEST
    l_idx = i * TL_FILT + lax.broadcasted_iota(jnp.int32, (TL_FILT, 1), 0)
    pos_i = jnp.where(l_idx < seq, seq - l_idx, l_idx - seq)
    pos = pos_i.astype(F32)
    t = pos / float(max(seq - 1, 1))
    ang = (2.0 * math.pi * pos / seq) * bands_ref[...]
    pre = (t * w1t_ref[...]
           + jnp.dot(jnp.cos(ang), w1c_ref[...], precision=hp, preferred_element_type=F32)
           + jnp.dot(-jnp.sin(ang), w1s_ref[...], precision=hp, preferred_element_type=F32))
    fr = fr_ref[...]
    h = jnp.sin(fr * (pre + b1_ref[...]))
    h = jnp.sin(fr * (jnp.dot(h, w2_ref[...], precision=hp, preferred_element_type=F32) + b2_ref[...]))
    h = jnp.dot(h, w3_ref[0], precision=hp, preferred_element_type=F32)
    decay = jnp.exp(-t * dl_ref[...])
    kk = jnp.where(l_idx == 0, 0.0, h * decay)
    kk_ref[...] = kk

    @pl.when(i == 0)
    def _():
        asum_ref[...] = jnp.zeros(asum_ref.shape, F32)

    asum_ref[...] += jnp.sum(jnp.abs(kk), axis=0, keepdims=True)


def _filters(bands, w1t, w1c, w1s, b1, w2, b2, w3d, fr, dl, seq):
    nl = 2 * seq // TL_FILT
    half = seq // TL_FILT
    ncol = HY_ORDER * HY_C
    return pl.pallas_call(
        functools.partial(_filter_kernel, seq=seq),
        grid=(nl,),
        in_specs=[_full(bands.shape), _full(w1t.shape), _full(w1c.shape), _full(w1s.shape),
                  _full(b1.shape), _full(w2.shape), _full(b2.shape),
                  pl.BlockSpec((1, FILTER_HIDDEN, ncol), lambda i: (jnp.where(i < half, 1, 0), 0, 0)),
                  _full(fr.shape), _full(dl.shape)],
        out_specs=[pl.BlockSpec((TL_FILT, ncol), lambda i: (i, 0)),
                   pl.BlockSpec((1, ncol), lambda i: (0, 0))],
        out_shape=[jax.ShapeDtypeStruct((2 * seq, ncol), F32),
                   jax.ShapeDtypeStruct((1, ncol), F32)],
        compiler_params=_cparams(("arbitrary",)),
    )(bands, w1t, w1c, w1s, b1, w2, b2, w3d, fr, dl)


def _spec_kernel(kk_ref, asum_ref, f_ref, g_ref, *, lb):
    nhalf = kk_ref.shape[0] // lb
    kkn = (kk_ref[...] / asum_ref[...]).astype(BF16)
    x8 = jnp.concatenate([kkn[m * lb:(m + 1) * lb] for m in range(nhalf)], axis=1)
    hh = jnp.dot(f_ref[...], x8, preferred_element_type=F32)
    ct = kk_ref.shape[1]
    row = lax.broadcasted_iota(jnp.int32, (2 * lb, 1), 0)
    sgn = (1 - 2 * (row & 1)).astype(F32)
    low = row <= lb
    for dd in range(nhalf - 1):
        h_neg = hh[:, dd * ct:(dd + 1) * ct]
        h_pos = hh[:, (dd + 1) * ct:(dd + 2) * ct]
        h0 = kkn[dd * lb:dd * lb + 1, :].astype(F32)
        g_ref[dd] = h_pos + sgn * (h_neg - jnp.where(low, h0, 0.0))


def _spectra(kk, asum, fmat, lb):
    ncol = kk.shape[1]
    nd = 2 * P_BLK - 1
    return pl.pallas_call(
        functools.partial(_spec_kernel, lb=lb),
        grid=(ncol // CT,),
        in_specs=[pl.BlockSpec((kk.shape[0], CT), lambda c: (0, c)),
                  pl.BlockSpec((1, CT), lambda c: (0, c)),
                  _full(fmat.shape)],
        out_specs=pl.BlockSpec((nd, 2 * lb, CT), lambda c: (0, 0, c)),
        out_shape=jax.ShapeDtypeStruct((nd, 2 * lb, ncol), F32),
        compiler_params=_cparams(("parallel",)),
    )(kk, asum, fmat)


def _sconv(u, w3, b):
    s = u.shape[0]
    row = lax.broadcasted_iota(jnp.int32, (s, 1), 0)
    prev = jnp.where(row == 0, 0.0, pltpu.roll(u, 1, axis=0))
    nxt = jnp.where(row == s - 1, 0.0, pltpu.roll(u, s - 1, axis=0))
    return b + prev * w3[0:1, :] + u * w3[1:2, :] + nxt * w3[2:3, :]


def _hyena_kernel(a_ref, gt_ref, cw_ref, cb_ref, g_ref, d_ref, f_ref, fi_ref, o_ref,
                  u_sc, y_sc, *, conv_a, lb):
    ct = a_ref.shape[2]
    a = a_ref[0]
    gate = gt_ref[0]
    if conv_a:
        a = _sconv(a, cw_ref[0], cb_ref[0:1, :])
    gate = _sconv(gate, cw_ref[1], cb_ref[1:2, :])

    ab = a.astype(BF16)
    x4 = jnp.concatenate([ab[j * lb:(j + 1) * lb] for j in range(P_BLK)], axis=1)
    u_sc[...] = jnp.dot(f_ref[...], x4, preferred_element_type=F32)

    def pointwise(r0, rows, packed_row):
        ure = [u_sc[pl.ds(r0, rows), j * ct:(j + 1) * ct] for j in range(P_BLK)]
        uim = [u_sc[pl.ds(lb + r0, rows), j * ct:(j + 1) * ct] for j in range(P_BLK)]
        if packed_row:
            first = lax.broadcasted_iota(jnp.int32, (rows, 1), 0) == 0
        for i in range(P_BLK):
            yre = yim = None
            dc = ny = None
            for j in range(P_BLK):
                dd = i - j + P_BLK - 1
                gre = g_ref[dd, pl.ds(r0, rows), :]
                gim = g_ref[dd, pl.ds(lb + r0, rows), :]
                rr = ure[j] * gre
                ii = uim[j] * gim
                tre = rr - ii
                tim = ure[j] * gim + uim[j] * gre
                yre = tre if yre is None else yre + tre
                yim = tim if yim is None else yim + tim
                if packed_row:
                    dc = rr if dc is None else dc + rr
                    ny = ii if ny is None else ny + ii
            if packed_row:
                yre = jnp.where(first, dc, yre)
                yim = jnp.where(first, ny, yim)
            y_sc[pl.ds(r0, rows), i * ct:(i + 1) * ct] = yre.astype(BF16)
            y_sc[pl.ds(lb + r0, rows), i * ct:(i + 1) * ct] = yim.astype(BF16)

    def chunk(r, carry):
        pointwise(pl.multiple_of(r * RC, RC), RC, False)
        return carry

    lax.fori_loop(0, lb // RC, chunk, 0)
    pointwise(0, 16, True)

    y4 = jnp.dot(fi_ref[...], y_sc[...], preferred_element_type=F32)
    y = jnp.concatenate([y4[:, i * ct:(i + 1) * ct] for i in range(P_BLK)], axis=0)
    o_ref[0] = (y + a * d_ref[...]) * gate


def _hyena_stage(a_arr, a_blk0, g_arr, g_blk0, cw, cb, gspec, g_blk0_spec, d, fmat, fimat, conv_a, lb):
    b, seq, _ = a_arr.shape
    nct = HY_C // CT
    nd = 2 * P_BLK - 1
    return pl.pallas_call(
        functools.partial(_hyena_kernel, conv_a=conv_a, lb=lb),
        grid=(nct, b),
        in_specs=[pl.BlockSpec((1, seq, CT), lambda c, bi: (bi, 0, a_blk0 + c)),
                  pl.BlockSpec((1, seq, CT), lambda c, bi: (bi, 0, g_blk0 + c)),
                  pl.BlockSpec((2, 3, CT), lambda c, bi: (0, 0, c)),
                  pl.BlockSpec((2, CT), lambda c, bi: (0, c)),
                  pl.BlockSpec((nd, 2 * lb, CT), lambda c, bi: (0, 0, g_blk0_spec + c),
                               pipeline_mode=pl.Buffered(1)),
                  pl.BlockSpec((1, CT), lambda c, bi: (0, c)),
                  _full(fmat.shape), _full(fimat.shape)],
        out_specs=pl.BlockSpec((1, seq, CT), lambda c, bi: (bi, 0, c)),
        out_shape=jax.ShapeDtypeStruct((b, seq, HY_C), F32),
        scratch_shapes=[pltpu.VMEM((2 * lb, P_BLK * CT), F32), pltpu.VMEM((2 * lb, P_BLK * CT), BF16)],
        compiler_params=_cparams(("parallel", "parallel")),
    )(a_arr, g_arr, cw, cb, gspec, d, fmat, fimat)


def _group_norm_tile(xs, g):
    lane = lax.broadcasted_iota(jnp.int32, xs.shape, 1)
    lo = lane < GROUP
    sq = xs * xs
    s_lo = jnp.sum(jnp.where(lo, sq, 0.0), axis=-1, keepdims=True)
    s_hi = jnp.sum(jnp.where(lo, 0.0, sq), axis=-1, keepdims=True)
    r = jnp.where(lo, lax.rsqrt(s_lo / GROUP + EPS), lax.rsqrt(s_hi / GROUP + EPS))
    return xs * r * g


def _ffn2_kernel(x_ref, a_ref, h_ref, hg_ref, wo_ref, g_ref, wg_ref, wu_ref, wd_ref, gf_ref, o_ref):
    tiles = []
    n_a = a_ref.shape[1] // LANES
    for s in range(D_MODEL // LANES):
        src = a_ref[:, s * LANES:(s + 1) * LANES] if s < n_a else \
            h_ref[:, (s - n_a) * LANES:(s - n_a + 1) * LANES]
        tiles.append(_group_norm_tile(src, hg_ref[:, s * LANES:(s + 1) * LANES]).astype(BF16))
    mix = jnp.concatenate(tiles, axis=1)
    x = x_ref[...] + jnp.dot(mix, wo_ref[...], preferred_element_type=F32)
    y = _swiglu_residual(x, g_ref, wg_ref, wu_ref, wd_ref)
    o_ref[...] = _rms(y, gf_ref[...])


def _ffn2(x1, a2d, h2d, hg, wo, g, wg, wu, wd, gf):
    t = x1.shape[0]
    row = lambda w: pl.BlockSpec((TM_FFN, w), lambda i: (i, 0))
    return pl.pallas_call(
        _ffn2_kernel,
        grid=(t // TM_FFN,),
        in_specs=[row(D_MODEL), row(a2d.shape[1]), row(h2d.shape[1]), _full(hg.shape), _full(wo.shape),
                  _full(g.shape), _full(wg.shape), _full(wu.shape), _full(wd.shape), _full(gf.shape)],
        out_specs=row(D_MODEL),
        out_shape=jax.ShapeDtypeStruct(x1.shape, F32),
        compiler_params=_cparams(("parallel",)),
    )(x1, a2d, h2d, hg, wo, g, wg, wu, wd, gf)


def _rope_tables(seq):
    inv = 1.0 / (ROPE_THETA ** (jnp.arange(0, ROPE, 2, dtype=F32) / ROPE))
    ang = jnp.arange(seq, dtype=F32)[:, None] * inv[None, :]
    cos, sin = jnp.cos(ang), jnp.sin(ang)
    z64 = jnp.zeros((seq, NOPE), F32)
    z32 = jnp.zeros((seq, HEAD_PAD - NOPE - ROPE), F32)
    cq_t = jnp.concatenate([jnp.ones((seq, NOPE), F32), cos, cos, z32], axis=1)
    ck_t = jnp.concatenate([z64, cos, cos, z32], axis=1)
    sn_t = jnp.concatenate([z64, -sin, sin, z32], axis=1)
    return cq_t, ck_t, sn_t


def _dft_matrices(lb):
    n2 = 2 * lb
    f = jnp.arange(lb, dtype=jnp.int32)[:, None]
    n = jnp.arange(lb, dtype=jnp.int32)[None, :]
    ang = (2.0 * math.pi / n2) * ((f * n) % n2).astype(F32)
    cos, sin = jnp.cos(ang), jnp.sin(ang)
    alt = (1 - 2 * (jnp.arange(lb, dtype=jnp.int32) % 2)).astype(F32)
    first = (f == 0)
    fwd = jnp.concatenate([cos, jnp.where(first, alt[None, :], -sin)], axis=0)
    inv_re = jnp.where(first, 1.0 / n2, (2.0 / n2) * cos)
    inv_im = jnp.where(first, alt[None, :] / n2, (-2.0 / n2) * sin)
    inv = jnp.concatenate([inv_re, inv_im], axis=0).T
    return fwd.astype(BF16), inv.astype(BF16)


def kernel(x, ffn1_norm_g, ffn1_w_gate, ffn1_w_up, ffn1_w_down, mix_norm_g, w_in, q_norm_g, w_uq, kv_norm_g, w_ukv, hyena_conv_w, hyena_conv_b, filt_w1, filt_b1, filt_w2, filt_b2, filt_w3, filt_freq, hyena_d, head_norm_g, w_out, ffn2_norm_g, ffn2_w_gate, ffn2_w_up, ffn2_w_down, final_norm_g):
    b, seq, d = x.shape
    t = b * seq
    lb = seq // P_BLK
    l = 0
    row = lambda v: v.reshape(1, -1)

    x1 = _ffn1(x.reshape(t, d), row(ffn1_norm_g[l]), ffn1_w_gate[l].astype(BF16),
               ffn1_w_up[l].astype(BF16), ffn1_w_down[l].astype(BF16))

    wi = w_in[l]
    o_kr = Q_RANK + KV_RANK
    half = ROPE // 2
    zc = lambda n: jnp.zeros((d, n), F32)
    kr = wi[:, o_kr:o_kr + ROPE]
    win_p = jnp.concatenate([
        wi[:, :o_kr],
        zc(NOPE), kr, zc(HEAD_PAD - NOPE - ROPE),
        zc(NOPE), kr[:, half:], kr[:, :half], zc(HEAD_PAD - NOPE - ROPE),
        wi[:, o_kr + ROPE:]], axis=1).astype(BF16)
    wq3 = w_uq[l].reshape(Q_RANK, MLA_HEADS, NOPE + ROPE)
    zq = lambda n: jnp.zeros((Q_RANK, MLA_HEADS, n), F32)
    wq_p = jnp.concatenate([wq3, zq(HEAD_PAD - NOPE - ROPE)], axis=2).reshape(Q_RANK, -1).astype(BF16)
    wqs_p = jnp.concatenate([zq(NOPE), wq3[:, :, NOPE + half:], wq3[:, :, NOPE:NOPE + half],
                             zq(HEAD_PAD - NOPE - ROPE)], axis=2).reshape(Q_RANK, -1).astype(BF16)
    wkv3 = w_ukv[l].reshape(KV_RANK, MLA_HEADS, NOPE + V_DIM)
    wk_p = jnp.concatenate([wkv3[:, :, :NOPE], jnp.zeros((KV_RANK, MLA_HEADS, HEAD_PAD - NOPE), F32)],
                           axis=2).reshape(KV_RANK, -1).astype(BF16)
    wv_p = wkv3[:, :, NOPE:].reshape(KV_RANK, -1).astype(BF16)
    cq_t, ck_t, sn_t = _rope_tables(seq)
    q2, k2, v2, u_hy = _inproj(x1, row(mix_norm_g[l]), win_p, row(q_norm_g[l]), wq_p, wqs_p,
                               row(kv_norm_g[l]), wk_p, wv_p, cq_t, ck_t, sn_t, seq)

    a = _attention(q2.reshape(b, seq, -1), k2.reshape(b, seq, -1), v2.reshape(b, seq, -1))

    bands = jnp.linspace(1e-4, FILTER_BANDS - 1, FILTER_BANDS, dtype=F32).reshape(1, -1)
    w1 = filt_w1[l].astype(F32)
    w3d = filt_w3[l].astype(F32).reshape(FILTER_HIDDEN, HY_ORDER, 2, HY_C).transpose(2, 0, 1, 3)
    w3d = w3d.reshape(2, FILTER_HIDDEN, HY_ORDER * HY_C)
    deltas = jnp.abs(jnp.linspace(MIN_DECAY, MAX_DECAY, HY_C, dtype=F32))
    dl = jnp.tile(deltas, HY_ORDER).reshape(1, -1)
    kk, asum = _filters(bands, w1[0:1], w1[1:1 + FILTER_BANDS], w1[1 + FILTER_BANDS:],
                        row(filt_b1[l].astype(F32)), filt_w2[l].astype(F32), row(filt_b2[l].astype(F32)),
                        w3d, row(filt_freq[l].astype(F32)), dl, seq)
    fmat, fimat = _dft_matrices(lb)
    gspec = _spectra(kk, asum, fmat, lb)

    nct = HY_C // CT
    cw = hyena_conv_w[l].reshape(3, 3, HY_C).transpose(1, 0, 2)
    cb = hyena_conv_b[l].reshape(3, HY_C)
    dsk = hyena_d[l].astype(F32)
    u3 = u_hy.reshape(b, seq, 3 * HY_C)
    z1 = _hyena_stage(u3, 0, u3, nct, cw[0:2], cb[0:2], gspec, 0, dsk[0:1], fmat, fimat, True, lb)
    cw2 = jnp.stack([cw[0], cw[2]])
    cb2 = jnp.stack([cb[0], cb[2]])
    hy = _hyena_stage(z1, 0, u3, 2 * nct, cw2, cb2, gspec, nct, dsk[1:2], fmat, fimat, False, lb)

    out = _ffn2(x1, a.reshape(t, -1), hy.reshape(t, -1), row(head_norm_g[l]), w_out[l].astype(BF16),
                row(ffn2_norm_g[l]), ffn2_w_gate[l].astype(BF16), ffn2_w_up[l].astype(BF16),
                ffn2_w_down[l].astype(BF16), row(final_norm_g))
    return out.reshape(b, seq, d)
```

```python
import functools
import math

import jax
import jax.numpy as jnp
from jax import lax
from jax.experimental import pallas as pl
from jax.experimental.pallas import tpu as pltpu

F32 = jnp.float32
BF16 = jnp.bfloat16

D_MODEL = 1024
MLA_HEADS = 8
NOPE = 64
ROPE = 32
V_DIM = 64
Q_RANK = 256
KV_RANK = 128
ROPE_THETA = 10000.0
HY_C = 512
FILTER_BANDS = 16
FILTER_HIDDEN = 64
HY_ORDER = 2
FFN_HIDDEN = 2816
FFN_RES = 0.5
EPS = 1e-6
GROUP = 64
DECAY_TARGET = 1e-2
MAX_DECAY = math.log(DECAY_TARGET) / 0.3
MIN_DECAY = math.log(DECAY_TARGET) / 1.5

LANES = 128
HEAD_PAD = 128
VMEM_LIMIT = 56 * 1024 * 1024

TM_FFN = 512
MXU_DIM = 256
FFN_CHUNK = 6 * MXU_DIM
TS_PROJ = 512
TQ = 512
TK = 1024
CT = 128
P_BLK = 4
RC = 32
TL_FILT = 512


def _cparams(sem, flags=None):
    return pltpu.CompilerParams(dimension_semantics=sem, vmem_limit_bytes=VMEM_LIMIT, flags=flags)


def _rms(x, g):
    ms = jnp.mean(x * x, axis=-1, keepdims=True)
    return x * lax.rsqrt(ms + EPS) * g


def _swiglu_residual(x, g_ref, wg_ref, wu_ref, wd_ref):
    xn = _rms(x, g_ref[...]).astype(BF16)
    acc = jnp.zeros(x.shape, F32)
    for c0 in range(0, FFN_HIDDEN, FFN_CHUNK):
        c1 = min(c0 + FFN_CHUNK, FFN_HIDDEN)
        gate = jnp.dot(xn, wg_ref[:, c0:c1], preferred_element_type=F32)
        up = jnp.dot(xn, wu_ref[:, c0:c1], preferred_element_type=F32)
        h = (gate * jax.nn.sigmoid(gate) * up).astype(BF16)
        acc = acc + jnp.dot(h, wd_ref[c0:c1, :], preferred_element_type=F32)
    return x + FFN_RES * acc


def _ffn1_kernel(x_ref, g_ref, wg_ref, wu_ref, wd_ref, o_ref):
    o_ref[...] = _swiglu_residual(x_ref[...], g_ref, wg_ref, wu_ref, wd_ref)


def _full(shape):
    return pl.BlockSpec(shape, lambda *_: (0,) * len(shape))


def _ffn1(x2d, g, wg, wu, wd):
    t = x2d.shape[0]
    row = pl.BlockSpec((TM_FFN, D_MODEL), lambda i: (i, 0))
    return pl.pallas_call(
        _ffn1_kernel, name="ffn1",
        grid=(t // TM_FFN,),
        in_specs=[row, _full(g.shape), _full(wg.shape), _full(wu.shape), _full(wd.shape)],
        out_specs=row,
        out_shape=jax.ShapeDtypeStruct(x2d.shape, F32),
        compiler_params=_cparams(("parallel",)),
    )(x2d, g, wg, wu, wd)


def _inproj_kernel(x_ref, g_ref, win_ref, qg_ref, wqt_ref, wqst_ref, kvg_ref, wk_ref, wvt_ref,
                   cqt_ref, snt_ref, ck_ref, sn_ref, q_ref, k_ref, v_ref, u_ref, *, scale):
    xn = _rms(x_ref[0], g_ref[...]).astype(BF16)
    u = jnp.dot(xn, win_ref[...], preferred_element_type=F32)
    c_q = u[:, 0:Q_RANK]
    c_kv = u[:, Q_RANK:Q_RANK + KV_RANK]
    o_pe = Q_RANK + KV_RANK
    kpe_a = u[:, o_pe:o_pe + LANES]
    kpe_b = u[:, o_pe + LANES:o_pe + 2 * LANES]
    u_ref[0] = u[:, o_pe + 2 * LANES:]

    nt = (((1,), (1,)), ((), ()))
    cqn = _rms(c_q, qg_ref[...]).astype(BF16)
    qt = lax.dot_general(wqt_ref[...], cqn, nt, preferred_element_type=F32)
    qst = lax.dot_general(wqst_ref[...], cqn, nt, preferred_element_type=F32)
    cq_t = cqt_ref[...]
    sn_tt = snt_ref[...]
    for h in range(MLA_HEADS):
        sl = slice(h * HEAD_PAD, (h + 1) * HEAD_PAD)
        q_ref[0, sl, :] = ((qt[sl] * cq_t + qst[sl] * sn_tt) * scale).astype(BF16)

    ckn = _rms(c_kv, kvg_ref[...]).astype(BF16)
    kn = jnp.dot(ckn, wk_ref[...], preferred_element_type=F32)
    k_pe = kpe_a * ck_ref[...] + kpe_b * sn_ref[...]
    for h in range(MLA_HEADS):
        sl = slice(h * HEAD_PAD, (h + 1) * HEAD_PAD)
        k_ref[0, h] = (kn[:, sl] + k_pe).astype(BF16)
    vt = lax.dot_general(wvt_ref[...], ckn, nt, preferred_element_type=F32)
    v_ref[0] = vt.reshape(MLA_HEADS // 2, 2 * V_DIM, vt.shape[1]).astype(BF16)


def _inproj(x1, g, win, qg, wqt, wqst, kvg, wk, wvt, cq_tt, sn_tt, ck_t, sn_t):
    b, seq, _ = x1.shape
    ts = TS_PROJ
    scale = (NOPE + ROPE) ** -0.5 * math.log2(math.e)
    hp = MLA_HEADS * HEAD_PAD
    return pl.pallas_call(
        functools.partial(_inproj_kernel, scale=scale), name="inproj",
        grid=(b, seq // ts),
        in_specs=[pl.BlockSpec((1, ts, D_MODEL), lambda bi, i: (bi, i, 0)),
                  _full(g.shape), _full(win.shape), _full(qg.shape), _full(wqt.shape),
                  _full(wqst.shape), _full(kvg.shape), _full(wk.shape), _full(wvt.shape),
                  pl.BlockSpec((HEAD_PAD, ts), lambda bi, i: (0, i)),
                  pl.BlockSpec((HEAD_PAD, ts), lambda bi, i: (0, i)),
                  pl.BlockSpec((ts, LANES), lambda bi, i: (i, 0)),
                  pl.BlockSpec((ts, LANES), lambda bi, i: (i, 0))],
        out_specs=[pl.BlockSpec((1, hp, ts), lambda bi, i: (bi, 0, i)),
                   pl.BlockSpec((1, MLA_HEADS, ts, HEAD_PAD), lambda bi, i: (bi, 0, i, 0)),
                   pl.BlockSpec((1, MLA_HEADS // 2, 2 * V_DIM, ts), lambda bi, i: (bi, 0, 0, i)),
                   pl.BlockSpec((1, ts, 3 * HY_C), lambda bi, i: (bi, i, 0))],
        out_shape=[jax.ShapeDtypeStruct((b, hp, seq), BF16),
                   jax.ShapeDtypeStruct((b, MLA_HEADS, seq, HEAD_PAD), BF16),
                   jax.ShapeDtypeStruct((b, MLA_HEADS // 2, 2 * V_DIM, seq), BF16),
                   jax.ShapeDtypeStruct((b, seq, 3 * HY_C), F32)],
        compiler_params=_cparams(("parallel", "parallel")),
    )(x1, g, win, qg, wqt, wqst, kvg, wk, wvt, cq_tt, sn_tt, ck_t, sn_t)


def _attn_kernel(q_ref, k_ref, v_ref, o_ref, s_buf, p_buf, mc_buf, al_buf, m_sc, l_sc, acc_sc):
    seq = k_ref.shape[2]
    nk = seq // TK
    nblk = 2 * nk

    def split(n):
        if isinstance(n, int):
            return n // nk, (n % nk) * TK, (n // nk) * HEAD_PAD
        h = (n >= nk).astype(jnp.int32)
        return h, pl.multiple_of((n - h * nk) * TK, TK), pl.multiple_of(h * HEAD_PAD, HEAD_PAD)

    def scores(n, slot):
        h, off, qoff = split(n)
        k = k_ref[0, h, pl.ds(off, TK), :]
        s = jnp.dot(k, q_ref[0, pl.ds(qoff, HEAD_PAD), :], preferred_element_type=F32)
        s_buf[slot] = s
        mc_buf[slot] = jnp.max(s, axis=0, keepdims=True)

    def softmax(n, slot):
        h, _, _ = split(n)
        m_prev = m_sc[h]
        m_new = jnp.maximum(m_prev, mc_buf[slot])
        alpha = jnp.exp2(m_prev - m_new)
        p = jnp.exp2(s_buf[slot] - m_new)
        l_sc[h] = alpha * l_sc[h] + jnp.sum(p, axis=0, keepdims=True)
        p_buf[slot] = p.astype(BF16)
        al_buf[slot] = alpha
        m_sc[h] = m_new

    def values(n, slot):
        h, off, _ = split(n)
        vt = v_ref[0, 0, :, pl.ds(off, TK)]
        pv = jnp.dot(vt, p_buf[slot], preferred_element_type=F32)
        acc_sc[h] = al_buf[slot] * acc_sc[h] + pv

    m_sc[...] = jnp.full(m_sc.shape, -jnp.inf, F32)
    l_sc[...] = jnp.zeros(l_sc.shape, F32)
    acc_sc[...] = jnp.zeros(acc_sc.shape, F32)

    scores(0, 0)
    scores(1, 1)
    softmax(0, 0)

    def body(i, carry):
        n = 2 * i
        scores(n + 2, 0)
        softmax(n + 1, 1)
        values(n, 0)
        scores(n + 3, 1)
        softmax(n + 2, 0)
        values(n + 1, 1)
        return carry

    lax.fori_loop(0, (nblk - 2) // 2, body, 0)
    softmax(nblk - 1, 1)
    values(nblk - 2, 0)
    values(nblk - 1, 1)

    row = lax.broadcasted_iota(jnp.int32, acc_sc.shape[1:], 0)
    out_t = jnp.where(row < V_DIM, acc_sc[0] / l_sc[0], acc_sc[1] / l_sc[1])
    o_ref[0] = out_t.T


def _attention(qt, k4, vt):
    b, _, seq = qt.shape
    pairs = MLA_HEADS // 2
    return pl.pallas_call(
        _attn_kernel, name="attn",
        grid=(b, pairs, seq // TQ),
        in_specs=[pl.BlockSpec((1, 2 * HEAD_PAD, TQ), lambda bi, j, qi: (bi, j, qi)),
                  pl.BlockSpec((1, 2, seq, HEAD_PAD), lambda bi, j, qi: (bi, j, 0, 0)),
                  pl.BlockSpec((1, 1, 2 * V_DIM, seq), lambda bi, j, qi: (bi, j, 0, 0))],
        out_specs=pl.BlockSpec((1, TQ, 2 * V_DIM), lambda bi, j, qi: (bi, qi, j)),
        out_shape=jax.ShapeDtypeStruct((b, seq, MLA_HEADS * V_DIM), F32),
        scratch_shapes=[pltpu.VMEM((2, TK, TQ), F32), pltpu.VMEM((2, TK, TQ), BF16),
                        pltpu.VMEM((2, 1, TQ), F32), pltpu.VMEM((2, 1, TQ), F32),
                        pltpu.VMEM((2, 1, TQ), F32), pltpu.VMEM((2, 1, TQ), F32),
                        pltpu.VMEM((2, 2 * V_DIM, TQ), F32)],
        compiler_params=_cparams(("parallel", "parallel", "parallel")),
    )(qt, k4, vt)


def _filter_kernel(bands_ref, w1t_ref, w1c_ref, w1s_ref, b1_ref, w2_ref, b2_ref, w3_ref,
                   fr_ref, dl_ref, kk_ref, asum_ref, *, seq):
    i = pl.program_id(0)
    hp = lax.Precision.HIGHEST
    l_idx = i * TL_FILT + lax.broadcasted_iota(jnp.int32, (TL_FILT, 1), 0)
    pos_i = jnp.where(l_idx < seq, seq - l_idx, l_idx - seq)
    pos = pos_i.astype(F32)
    t = pos / float(max(seq - 1, 1))
    ang = (2.0 * math.pi * pos / seq) * bands_ref[...]
    pre = (t * w1t_ref[...]
           + jnp.dot(jnp.cos(ang), w1c_ref[...], precision=hp, preferred_element_type=F32)
           + jnp.dot(-jnp.sin(ang), w1s_ref[...], precision=hp, preferred_element_type=F32))
    fr = fr_ref[...]
    h = jnp.sin(fr * (pre + b1_ref[...]))
    h = jnp.sin(fr * (jnp.dot(h, w2_ref[...], precision=hp, preferred_element_type=F32) + b2_ref[...]))
    h = jnp.dot(h, w3_ref[0], precision=hp, preferred_element_type=F32)
    decay = jnp.exp(-t * dl_ref[...])
    kk = jnp.where(l_idx == 0, 0.0, h * decay)
    kk_ref[...] = kk

    @pl.when(i == 0)
    def _():
        asum_ref[...] = jnp.zeros(asum_ref.shape, F32)

    asum_ref[...] += jnp.sum(jnp.abs(kk), axis=0, keepdims=True)


def _filters(bands, w1t, w1c, w1s, b1, w2, b2, w3d, fr, dl, seq):
    nl = 2 * seq // TL_FILT
    half = seq // TL_FILT
    ncol = HY_ORDER * HY_C
    return pl.pallas_call(
        functools.partial(_filter_kernel, seq=seq), name="hyfilt",
        grid=(nl,),
        in_specs=[_full(bands.shape), _full(w1t.shape), _full(w1c.shape), _full(w1s.shape),
                  _full(b1.shape), _full(w2.shape), _full(b2.shape),
                  pl.BlockSpec((1, FILTER_HIDDEN, ncol), lambda i: (jnp.where(i < half, 1, 0), 0, 0)),
                  _full(fr.shape), _full(dl.shape)],
        out_specs=[pl.BlockSpec((TL_FILT, ncol), lambda i: (i, 0)),
                   pl.BlockSpec((1, ncol), lambda i: (0, 0))],
        out_shape=[jax.ShapeDtypeStruct((2 * seq, ncol), F32),
                   jax.ShapeDtypeStruct((1, ncol), F32)],
        compiler_params=_cparams(("arbitrary",)),
    )(bands, w1t, w1c, w1s, b1, w2, b2, w3d, fr, dl)


def _spec_kernel(kk_ref, asum_ref, f_ref, g_ref, *, lb):
    nhalf = kk_ref.shape[0] // lb
    kkn = (kk_ref[...] / asum_ref[...]).astype(BF16)
    x8 = jnp.concatenate([kkn[m * lb:(m + 1) * lb] for m in range(nhalf)], axis=1)
    hh = jnp.dot(f_ref[...], x8, preferred_element_type=F32)
    ct = kk_ref.shape[1]
    row = lax.broadcasted_iota(jnp.int32, (2 * lb, 1), 0)
    sgn = (1 - 2 * (row & 1)).astype(F32)
    low = row <= lb
    for dd in range(nhalf - 1):
        h_neg = hh[:, dd * ct:(dd + 1) * ct]
        h_pos = hh[:, (dd + 1) * ct:(dd + 2) * ct]
        h0 = kkn[dd * lb:dd * lb + 1, :].astype(F32)
        g_ref[dd] = h_pos + sgn * (h_neg - jnp.where(low, h0, 0.0))


def _spectra(kk, asum, fmat, lb):
    ncol = kk.shape[1]
    nd = 2 * P_BLK - 1
    return pl.pallas_call(
        functools.partial(_spec_kernel, lb=lb), name="hyspec",
        grid=(ncol // CT,),
        in_specs=[pl.BlockSpec((kk.shape[0], CT), lambda c: (0, c)),
                  pl.BlockSpec((1, CT), lambda c: (0, c)),
                  _full(fmat.shape)],
        out_specs=pl.BlockSpec((nd, 2 * lb, CT), lambda c: (0, 0, c)),
        out_shape=jax.ShapeDtypeStruct((nd, 2 * lb, ncol), F32),
        compiler_params=_cparams(("parallel",)),
    )(kk, asum, fmat)


def _sconv(u, w3, b):
    s = u.shape[0]
    row = lax.broadcasted_iota(jnp.int32, (s, 1), 0)
    prev = jnp.where(row == 0, 0.0, pltpu.roll(u, 1, axis=0))
    nxt = jnp.where(row == s - 1, 0.0, pltpu.roll(u, s - 1, axis=0))
    return b + prev * w3[0:1, :] + u * w3[1:2, :] + nxt * w3[2:3, :]


def _hyena_kernel(a_ref, gt_ref, cw_ref, cb_ref, g_ref, d_ref, f_ref, fi_ref, o_ref,
                  u_sc, y_sc, *, conv_a, lb):
    ct = a_ref.shape[2]
    a = a_ref[0]
    gate = gt_ref[0]
    if conv_a:
        a = _sconv(a, cw_ref[0], cb_ref[0:1, :])
    gate = _sconv(gate, cw_ref[1], cb_ref[1:2, :])

    ab = a.astype(BF16)
    x4 = jnp.concatenate([ab[j * lb:(j + 1) * lb] for j in range(P_BLK)], axis=1)
    u_sc[...] = jnp.dot(f_ref[...], x4, preferred_element_type=F32)

    def pointwise(r0, rows, packed_row):
        ure = [u_sc[pl.ds(r0, rows), j * ct:(j + 1) * ct] for j in range(P_BLK)]
        uim = [u_sc[pl.ds(lb + r0, rows), j * ct:(j + 1) * ct] for j in range(P_BLK)]
        if packed_row:
            first = lax.broadcasted_iota(jnp.int32, (rows, 1), 0) == 0
        for i in range(P_BLK):
            yre = yim = None
            dc = ny = None
            for j in range(P_BLK):
                dd = i - j + P_BLK - 1
                gre = g_ref[dd, pl.ds(r0, rows), :]
                gim = g_ref[dd, pl.ds(lb + r0, rows), :]
                rr = ure[j] * gre
                ii = uim[j] * gim
                tre = rr - ii
                tim = ure[j] * gim + uim[j] * gre
                yre = tre if yre is None else yre + tre
                yim = tim if yim is None else yim + tim
                if packed_row:
                    dc = rr if dc is None else dc + rr
                    ny = ii if ny is None else ny + ii
            if packed_row:
                yre = jnp.where(first, dc, yre)
                yim = jnp.where(first, ny, yim)
            y_sc[pl.ds(r0, rows), i * ct:(i + 1) * ct] = yre.astype(BF16)
            y_sc[pl.ds(lb + r0, rows), i * ct:(i + 1) * ct] = yim.astype(BF16)

    def chunk(r, carry):
        pointwise(pl.multiple_of(r * RC, RC), RC, False)
        return carry

    lax.fori_loop(0, lb // RC, chunk, 0)
    pointwise(0, 16, True)

    y4 = jnp.dot(fi_ref[...], y_sc[...], preferred_element_type=F32)
    y = jnp.concatenate([y4[:, i * ct:(i + 1) * ct] for i in range(P_BLK)], axis=0)
    o_ref[0] = (y + a * d_ref[...]) * gate


def _hyena_stage(a_arr, a_blk0, g_arr, g_blk0, cw, cb, gspec, g_blk0_spec, d, fmat, fimat, conv_a, lb):
    b, seq, _ = a_arr.shape
    nct = HY_C // CT
    nd = 2 * P_BLK - 1
    return pl.pallas_call(
        functools.partial(_hyena_kernel, conv_a=conv_a, lb=lb), name="hyena",
        grid=(nct, b),
        in_specs=[pl.BlockSpec((1, seq, CT), lambda c, bi: (bi, 0, a_blk0 + c)),
                  pl.BlockSpec((1, seq, CT), lambda c, bi: (bi, 0, g_blk0 + c)),
                  pl.BlockSpec((2, 3, CT), lambda c, bi: (0, 0, c)),
                  pl.BlockSpec((2, CT), lambda c, bi: (0, c)),
                  pl.BlockSpec((nd, 2 * lb, CT), lambda c, bi: (0, 0, g_blk0_spec + c),
                               pipeline_mode=pl.Buffered(1)),
                  pl.BlockSpec((1, CT), lambda c, bi: (0, c)),
                  _full(fmat.shape), _full(fimat.shape)],
        out_specs=pl.BlockSpec((1, seq, CT), lambda c, bi: (bi, 0, c)),
        out_shape=jax.ShapeDtypeStruct((b, seq, HY_C), F32),
        scratch_shapes=[pltpu.VMEM((2 * lb, P_BLK * CT), F32), pltpu.VMEM((2 * lb, P_BLK * CT), BF16)],
        compiler_params=_cparams(("parallel", "parallel")),
    )(a_arr, g_arr, cw, cb, gspec, d, fmat, fimat)


def _group_norm_tile(xs, g):
    lane = lax.broadcasted_iota(jnp.int32, xs.shape, 1)
    lo = lane < GROUP
    sq = xs * xs
    s_lo = jnp.sum(jnp.where(lo, sq, 0.0), axis=-1, keepdims=True)
    s_hi = jnp.sum(jnp.where(lo, 0.0, sq), axis=-1, keepdims=True)
    r = jnp.where(lo, lax.rsqrt(s_lo / GROUP + EPS), lax.rsqrt(s_hi / GROUP + EPS))
    return xs * r * g


def _ffn2_kernel(x_ref, a_ref, h_ref, hg_ref, wo_ref, g_ref, wg_ref, wu_ref, wd_ref, gf_ref, o_ref):
    tiles = []
    n_a = a_ref.shape[1] // LANES
    for s in range(D_MODEL // LANES):
        src = a_ref[:, s * LANES:(s + 1) * LANES] if s < n_a else \
            h_ref[:, (s - n_a) * LANES:(s - n_a + 1) * LANES]
        tiles.append(_group_norm_tile(src, hg_ref[:, s * LANES:(s + 1) * LANES]).astype(BF16))
    mix = jnp.concatenate(tiles, axis=1)
    x = x_ref[...] + jnp.dot(mix, wo_ref[...], preferred_element_type=F32)
    y = _swiglu_residual(x, g_ref, wg_ref, wu_ref, wd_ref)
    o_ref[...] = _rms(y, gf_ref[...])


def _ffn2(x1, a2d, h2d, hg, wo, g, wg, wu, wd, gf):
    t = x1.shape[0]
    row = lambda w: pl.BlockSpec((TM_FFN, w), lambda i: (i, 0))
    return pl.pallas_call(
        _ffn2_kernel, name="ffn2",
        grid=(t // TM_FFN,),
        in_specs=[row(D_MODEL), row(a2d.shape[1]), row(h2d.shape[1]), _full(hg.shape), _full(wo.shape),
                  _full(g.shape), _full(wg.shape), _full(wu.shape), _full(wd.shape), _full(gf.shape)],
        out_specs=row(D_MODEL),
        out_shape=jax.ShapeDtypeStruct(x1.shape, F32),
        compiler_params=_cparams(("parallel",)),
    )(x1, a2d, h2d, hg, wo, g, wg, wu, wd, gf)


def _rope_tables(seq):
    inv = 1.0 / (ROPE_THETA ** (jnp.arange(0, ROPE, 2, dtype=F32) / ROPE))
    ang = jnp.arange(seq, dtype=F32)[:, None] * inv[None, :]
    cos, sin = jnp.cos(ang), jnp.sin(ang)
    z64 = jnp.zeros((seq, NOPE), F32)
    z32 = jnp.zeros((seq, HEAD_PAD - NOPE - ROPE), F32)
    cq_t = jnp.concatenate([jnp.ones((seq, NOPE), F32), cos, cos, z32], axis=1)
    ck_t = jnp.concatenate([z64, cos, cos, z32], axis=1)
    sn_t = jnp.concatenate([z64, -sin, sin, z32], axis=1)
    return cq_t, ck_t, sn_t


def _dft_matrices(lb):
    n2 = 2 * lb
    f = jnp.arange(lb, dtype=jnp.int32)[:, None]
    n = jnp.arange(lb, dtype=jnp.int32)[None, :]
    ang = (2.0 * math.pi / n2) * ((f * n) % n2).astype(F32)
    cos, sin = jnp.cos(ang), jnp.sin(ang)
    alt = (1 - 2 * (jnp.arange(lb, dtype=jnp.int32) % 2)).astype(F32)
    first = (f == 0)
    fwd = jnp.concatenate([cos, jnp.where(first, alt[None, :], -sin)], axis=0)
    inv_re = jnp.where(first, 1.0 / n2, (2.0 / n2) * cos)
    inv_im = jnp.where(first, alt[None, :] / n2, (-2.0 / n2) * sin)
    inv = jnp.concatenate([inv_re, inv_im], axis=0).T
    return fwd.astype(BF16), inv.astype(BF16)


def kernel(x, ffn1_norm_g, ffn1_w_gate, ffn1_w_up, ffn1_w_down, mix_norm_g, w_in, q_norm_g, w_uq, kv_norm_g, w_ukv, hyena_conv_w, hyena_conv_b, filt_w1, filt_b1, filt_w2, filt_b2, filt_w3, filt_freq, hyena_d, head_norm_g, w_out, ffn2_norm_g, ffn2_w_gate, ffn2_w_up, ffn2_w_down, final_norm_g):
    b, seq, d = x.shape
    t = b * seq
    lb = seq // P_BLK
    l = 0
    row = lambda v: v.reshape(1, -1)

    x1 = _ffn1(x.reshape(t, d), row(ffn1_norm_g[l]), ffn1_w_gate[l].astype(BF16),
               ffn1_w_up[l].astype(BF16), ffn1_w_down[l].astype(BF16))

    wi = w_in[l]
    o_kr = Q_RANK + KV_RANK
    half = ROPE // 2
    zc = lambda n: jnp.zeros((d, n), F32)
    kr = wi[:, o_kr:o_kr + ROPE]
    win_p = jnp.concatenate([
        wi[:, :o_kr],
        zc(NOPE), kr, zc(HEAD_PAD - NOPE - ROPE),
        zc(NOPE), kr[:, half:], kr[:, :half], zc(HEAD_PAD - NOPE - ROPE),
        wi[:, o_kr + ROPE:]], axis=1).astype(BF16)
    wq3 = w_uq[l].reshape(Q_RANK, MLA_HEADS, NOPE + ROPE)
    zq = lambda n: jnp.zeros((Q_RANK, MLA_HEADS, n), F32)
    wq_p = jnp.concatenate([wq3, zq(HEAD_PAD - NOPE - ROPE)], axis=2).reshape(Q_RANK, -1).astype(BF16)
    wqs_p = jnp.concatenate([zq(NOPE), wq3[:, :, NOPE + half:], wq3[:, :, NOPE:NOPE + half],
                             zq(HEAD_PAD - NOPE - ROPE)], axis=2).reshape(Q_RANK, -1).astype(BF16)
    wkv3 = w_ukv[l].reshape(KV_RANK, MLA_HEADS, NOPE + V_DIM)
    wk_p = jnp.concatenate([wkv3[:, :, :NOPE], jnp.zeros((KV_RANK, MLA_HEADS, HEAD_PAD - NOPE), F32)],
                           axis=2).reshape(KV_RANK, -1).astype(BF16)
    wv_p = wkv3[:, :, NOPE:].reshape(KV_RANK, -1).astype(BF16)
    cq_t, ck_t, sn_t = _rope_tables(seq)
    qt, k4, vt, u3 = _inproj(x1.reshape(b, seq, d), row(mix_norm_g[l]), win_p, row(q_norm_g[l]),
                             wq_p.T, wqs_p.T, row(kv_norm_g[l]), wk_p, wv_p.T,
                             cq_t.T, sn_t.T, ck_t, sn_t)

    a = _attention(qt, k4, vt)

    bands = jnp.linspace(1e-4, FILTER_BANDS - 1, FILTER_BANDS, dtype=F32).reshape(1, -1)
    w1 = filt_w1[l].astype(F32)
    w3d = filt_w3[l].astype(F32).reshape(FILTER_HIDDEN, HY_ORDER, 2, HY_C).transpose(2, 0, 1, 3)
    w3d = w3d.reshape(2, FILTER_HIDDEN, HY_ORDER * HY_C)
    deltas = jnp.abs(jnp.linspace(MIN_DECAY, MAX_DECAY, HY_C, dtype=F32))
    dl = jnp.tile(deltas, HY_ORDER).reshape(1, -1)
    kk, asum = _filters(bands, w1[0:1], w1[1:1 + FILTER_BANDS], w1[1 + FILTER_BANDS:],
                        row(filt_b1[l].astype(F32)), filt_w2[l].astype(F32), row(filt_b2[l].astype(F32)),
                        w3d, row(filt_freq[l].astype(F32)), dl, seq)
    fmat, fimat = _dft_matrices(lb)
    gspec = _spectra(kk, asum, fmat, lb)

    nct = HY_C // CT
    cw = hyena_conv_w[l].reshape(3, 3, HY_C).transpose(1, 0, 2)
    cb = hyena_conv_b[l].reshape(3, HY_C)
    dsk = hyena_d[l].astype(F32)
    z1 = _hyena_stage(u3, 0, u3, nct, cw[0:2], cb[0:2], gspec, 0, dsk[0:1], fmat, fimat, True, lb)
    cw2 = jnp.stack([cw[0], cw[2]])
    cb2 = jnp.stack([cb[0], cb[2]])
    hy = _hyena_stage(z1, 0, u3, 2 * nct, cw2, cb2, gspec, nct, dsk[1:2], fmat, fimat, False, lb)

    out = _ffn2(x1, a.reshape(t, -1), hy.reshape(t, -1), row(head_norm_g[l]), w_out[l].astype(BF16),
                row(ffn2_norm_g[l]), ffn2_w_gate[l].astype(BF16), ffn2_w_up[l].astype(BF16),
                ffn2_w_down[l].astype(BF16), row(final_norm_g))
    return out.reshape(b, seq, d)
```

```python
import functools
import math

import jax
import jax.numpy as jnp
from jax import lax
from jax.experimental import pallas as pl
from jax.experimental.pallas import tpu as pltpu

F32 = jnp.float32
BF16 = jnp.bfloat16

D_MODEL = 1024
MLA_HEADS = 8
NOPE = 64
ROPE = 32
V_DIM = 64
Q_RANK = 256
KV_RANK = 128
ROPE_THETA = 10000.0
HY_C = 512
FILTER_BANDS = 16
FILTER_HIDDEN = 64
HY_ORDER = 2
FFN_HIDDEN = 2816
FFN_RES = 0.5
EPS = 1e-6
GROUP = 64
DECAY_TARGET = 1e-2
MAX_DECAY = math.log(DECAY_TARGET) / 0.3
MIN_DECAY = math.log(DECAY_TARGET) / 1.5

LANES = 128
HEAD_PAD = 128
VMEM_LIMIT = 56 * 1024 * 1024

TM_FFN = 512
MXU_DIM = 256
FFN_CHUNK = 6 * MXU_DIM
TS_PROJ = 512
TQ = 512
TK = 1024
ATT_SLOTS = 2
REF_ROWS = 16
ATT_SAFE_EXP = 60.0
CT = 128
P_BLK = 4
RC = 32
TL_FILT = 512


def _cparams(sem, flags=None):
    return pltpu.CompilerParams(dimension_semantics=sem, vmem_limit_bytes=VMEM_LIMIT, flags=flags)


def _rms(x, g):
    ms = jnp.mean(x * x, axis=-1, keepdims=True)
    return x * lax.rsqrt(ms + EPS) * g


def _swiglu_residual(x, g_ref, wg_ref, wu_ref, wd_ref):
    xn = _rms(x, g_ref[...]).astype(BF16)
    acc = jnp.zeros(x.shape, F32)
    for c0 in range(0, FFN_HIDDEN, FFN_CHUNK):
        c1 = min(c0 + FFN_CHUNK, FFN_HIDDEN)
        gate = jnp.dot(xn, wg_ref[:, c0:c1], preferred_element_type=F32)
        up = jnp.dot(xn, wu_ref[:, c0:c1], preferred_element_type=F32)
        h = (gate * jax.nn.sigmoid(gate) * up).astype(BF16)
        acc = acc + jnp.dot(h, wd_ref[c0:c1, :], preferred_element_type=F32)
    return x + FFN_RES * acc


def _ffn1_kernel(x_ref, g_ref, wg_ref, wu_ref, wd_ref, o_ref):
    o_ref[...] = _swiglu_residual(x_ref[...], g_ref, wg_ref, wu_ref, wd_ref)


def _full(shape):
    return pl.BlockSpec(shape, lambda *_: (0,) * len(shape))


def _ffn1(x2d, g, wg, wu, wd):
    t = x2d.shape[0]
    row = pl.BlockSpec((TM_FFN, D_MODEL), lambda i: (i, 0))
    return pl.pallas_call(
        _ffn1_kernel, name="ffn1",
        grid=(t // TM_FFN,),
        in_specs=[row, _full(g.shape), _full(wg.shape), _full(wu.shape), _full(wd.shape)],
        out_specs=row,
        out_shape=jax.ShapeDtypeStruct(x2d.shape, F32),
        compiler_params=_cparams(("parallel",)),
    )(x2d, g, wg, wu, wd)


def _inproj_kernel(x_ref, g_ref, win_ref, qg_ref, wqt_ref, wqst_ref, kvg_ref, wk_ref, wvt_ref,
                   cqt_ref, snt_ref, ck_ref, sn_ref, q_ref, k_ref, v_ref, u_ref, *, scale):
    xn = _rms(x_ref[0], g_ref[...]).astype(BF16)
    u = jnp.dot(xn, win_ref[...], preferred_element_type=F32)
    c_q = u[:, 0:Q_RANK]
    c_kv = u[:, Q_RANK:Q_RANK + KV_RANK]
    o_pe = Q_RANK + KV_RANK
    kpe_a = u[:, o_pe:o_pe + LANES]
    kpe_b = u[:, o_pe + LANES:o_pe + 2 * LANES]
    u_ref[0] = u[:, o_pe + 2 * LANES:]

    nt = (((1,), (1,)), ((), ()))
    cqn = _rms(c_q, qg_ref[...]).astype(BF16)
    qt = lax.dot_general(wqt_ref[...], cqn, nt, preferred_element_type=F32)
    qst = lax.dot_general(wqst_ref[...], cqn, nt, preferred_element_type=F32)
    cq_t = cqt_ref[...]
    sn_tt = snt_ref[...]
    for h in range(MLA_HEADS):
        sl = slice(h * HEAD_PAD, (h + 1) * HEAD_PAD)
        q_ref[0, sl, :] = ((qt[sl] * cq_t + qst[sl] * sn_tt) * scale).astype(BF16)

    ckn = _rms(c_kv, kvg_ref[...]).astype(BF16)
    kn = jnp.dot(ckn, wk_ref[...], preferred_element_type=F32)
    k_pe = kpe_a * ck_ref[...] + kpe_b * sn_ref[...]
    for h in range(MLA_HEADS):
        sl = slice(h * HEAD_PAD, (h + 1) * HEAD_PAD)
        k_ref[0, h] = (kn[:, sl] + k_pe).astype(BF16)
    vt = lax.dot_general(wvt_ref[...], ckn, nt, preferred_element_type=F32)
    v_ref[0] = vt.reshape(MLA_HEADS // 2, 2 * V_DIM, vt.shape[1]).astype(BF16)


def _inproj(x1, g, win, qg, wqt, wqst, kvg, wk, wvt, cq_tt, sn_tt, ck_t, sn_t):
    b, seq, _ = x1.shape
    ts = TS_PROJ
    scale = (NOPE + ROPE) ** -0.5 * math.log2(math.e)
    hp = MLA_HEADS * HEAD_PAD
    return pl.pallas_call(
        functools.partial(_inproj_kernel, scale=scale), name="inproj",
        grid=(b, seq // ts),
        in_specs=[pl.BlockSpec((1, ts, D_MODEL), lambda bi, i: (bi, i, 0)),
                  _full(g.shape), _full(win.shape), _full(qg.shape), _full(wqt.shape),
                  _full(wqst.shape), _full(kvg.shape), _full(wk.shape), _full(wvt.shape),
                  pl.BlockSpec((HEAD_PAD, ts), lambda bi, i: (0, i)),
                  pl.BlockSpec((HEAD_PAD, ts), lambda bi, i: (0, i)),
                  pl.BlockSpec((ts, LANES), lambda bi, i: (i, 0)),
                  pl.BlockSpec((ts, LANES), lambda bi, i: (i, 0))],
        out_specs=[pl.BlockSpec((1, hp, ts), lambda bi, i: (bi, 0, i)),
                   pl.BlockSpec((1, MLA_HEADS, ts, HEAD_PAD), lambda bi, i: (bi, 0, i, 0)),
                   pl.BlockSpec((1, MLA_HEADS // 2, 2 * V_DIM, ts), lambda bi, i: (bi, 0, 0, i)),
                   pl.BlockSpec((1, ts, 3 * HY_C), lambda bi, i: (bi, i, 0))],
        out_shape=[jax.ShapeDtypeStruct((b, hp, seq), BF16),
                   jax.ShapeDtypeStruct((b, MLA_HEADS, seq, HEAD_PAD), BF16),
                   jax.ShapeDtypeStruct((b, MLA_HEADS // 2, 2 * V_DIM, seq), BF16),
                   jax.ShapeDtypeStruct((b, seq, 3 * HY_C), F32)],
        compiler_params=_cparams(("parallel", "parallel")),
    )(x1, g, win, qg, wqt, wqst, kvg, wk, wvt, cq_tt, sn_tt, ck_t, sn_t)


def _attn_robust(q_ref, k_ref, v_ref, o_ref, s_buf, p_buf, mc_buf, al_buf, m_sc, l_sc, acc_sc):
    seq = k_ref.shape[2]
    nk = seq // TK
    nblk = 2 * nk

    def split(n):
        if isinstance(n, int):
            return n // nk, (n % nk) * TK, (n // nk) * HEAD_PAD
        h = (n >= nk).astype(jnp.int32)
        return h, pl.multiple_of((n - h * nk) * TK, TK), pl.multiple_of(h * HEAD_PAD, HEAD_PAD)

    def scores(n, slot):
        h, off, qoff = split(n)
        k = k_ref[0, h, pl.ds(off, TK), :]
        s = jnp.dot(k, q_ref[0, pl.ds(qoff, HEAD_PAD), :], preferred_element_type=F32)
        s_buf[slot] = s
        mc_buf[slot] = jnp.max(s, axis=0, keepdims=True)

    def softmax(n, slot):
        h, _, _ = split(n)
        m_prev = m_sc[h]
        m_new = jnp.maximum(m_prev, mc_buf[slot])
        alpha = jnp.exp2(m_prev - m_new)
        p = jnp.exp2(s_buf[slot] - m_new)
        l_sc[h] = alpha * l_sc[h] + jnp.sum(p, axis=0, keepdims=True)
        p_buf[slot] = p.astype(BF16)
        al_buf[slot] = alpha
        m_sc[h] = m_new

    def values(n, slot):
        h, off, _ = split(n)
        vt = v_ref[0, 0, :, pl.ds(off, TK)]
        pv = jnp.dot(vt, p_buf[slot], preferred_element_type=F32)
        acc_sc[h] = al_buf[slot] * acc_sc[h] + pv

    m_sc[...] = jnp.full(m_sc.shape, -jnp.inf, F32)
    l_sc[...] = jnp.zeros(l_sc.shape, F32)
    acc_sc[...] = jnp.zeros(acc_sc.shape, F32)

    scores(0, 0)
    scores(1, 1)
    softmax(0, 0)

    def body(i, carry):
        n = 2 * i
        scores(n + 2, 0)
        softmax(n + 1, 1)
        values(n, 0)
        scores(n + 3, 1)
        softmax(n + 2, 0)
        values(n + 1, 1)
        return carry

    lax.fori_loop(0, (nblk - 2) // 2, body, 0)
    softmax(nblk - 1, 1)
    values(nblk - 2, 0)
    values(nblk - 1, 1)

    row = lax.broadcasted_iota(jnp.int32, acc_sc.shape[1:], 0)
    out_t = jnp.where(row < V_DIM, acc_sc[0] / l_sc[0], acc_sc[1] / l_sc[1])
    o_ref[0] = out_t.T


def _attn_kernel(q_ref, k_ref, v_ref, o_ref, *scratch):
    seq = k_ref.shape[2]
    outs = []
    exceed = None
    for h in range(2):
        qt = q_ref[0, h * HEAD_PAD:(h + 1) * HEAD_PAD, :]
        s0 = jnp.dot(k_ref[0, h, 0:REF_ROWS, :], qt, preferred_element_type=F32)
        m_ref = jnp.max(s0, axis=0, keepdims=True)
        cmax = m_ref
        l = acc = None
        for j in range(seq // TK):
            ks = slice(j * TK, (j + 1) * TK)
            s = jnp.dot(k_ref[0, h, ks, :], qt, preferred_element_type=F32)
            cmax = jnp.maximum(cmax, jnp.max(s, axis=0, keepdims=True))
            p = jnp.exp2(s - m_ref)
            p_sum = jnp.sum(p, axis=0, keepdims=True)
            pv = jnp.dot(v_ref[0, 0, :, ks], p.astype(BF16), preferred_element_type=F32)
            l = p_sum if l is None else l + p_sum
            acc = pv if acc is None else acc + pv
        outs.append(acc / l)
        e = jnp.max(cmax - m_ref)
        exceed = e if exceed is None else jnp.maximum(exceed, e)
    row = lax.broadcasted_iota(jnp.int32, outs[0].shape, 0)
    o_ref[0] = jnp.where(row < V_DIM, outs[0], outs[1]).T

    @pl.when(exceed > ATT_SAFE_EXP)
    def _():
        _attn_robust(q_ref, k_ref, v_ref, o_ref, *scratch)


def _attention(qt, k4, vt):
    b, _, seq = qt.shape
    pairs = MLA_HEADS // 2
    return pl.pallas_call(
        _attn_kernel, name="attn",
        grid=(b, pairs, seq // TQ),
        in_specs=[pl.BlockSpec((1, 2 * HEAD_PAD, TQ), lambda bi, j, qi: (bi, j, qi)),
                  pl.BlockSpec((1, 2, seq, HEAD_PAD), lambda bi, j, qi: (bi, j, 0, 0)),
                  pl.BlockSpec((1, 1, 2 * V_DIM, seq), lambda bi, j, qi: (bi, j, 0, 0))],
        out_specs=pl.BlockSpec((1, TQ, 2 * V_DIM), lambda bi, j, qi: (bi, qi, j)),
        out_shape=jax.ShapeDtypeStruct((b, seq, MLA_HEADS * V_DIM), F32),
        scratch_shapes=[pltpu.VMEM((ATT_SLOTS, TK, TQ), F32), pltpu.VMEM((ATT_SLOTS, TK, TQ), BF16),
                        pltpu.VMEM((ATT_SLOTS, 1, TQ), F32), pltpu.VMEM((ATT_SLOTS, 1, TQ), F32),
                        pltpu.VMEM((2, 1, TQ), F32), pltpu.VMEM((2, 1, TQ), F32),
                        pltpu.VMEM((2, 2 * V_DIM, TQ), F32)],
        compiler_params=_cparams(("parallel", "parallel", "parallel")),
    )(qt, k4, vt)


def _filter_kernel(bands_ref, w1t_ref, w1c_ref, w1s_ref, b1_ref, w2_ref, b2_ref, w3_ref,
                   fr_ref, dl_ref, kk_ref, asum_ref, *, seq):
    i = pl.program_id(0)
    hp = lax.Precision.HIGHEST
    l_idx = i * TL_FILT + lax.broadcasted_iota(jnp.int32, (TL_FILT, 1), 0)
    pos_i = jnp.where(l_idx < seq, seq - l_idx, l_idx - seq)
    pos = pos_i.astype(F32)
    t = pos / float(max(seq - 1, 1))
    ang = (2.0 * math.pi * pos / seq) * bands_ref[...]
    pre = (t * w1t_ref[...]
           + jnp.dot(jnp.cos(ang), w1c_ref[...], precision=hp, preferred_element_type=F32)
           + jnp.dot(-jnp.sin(ang), w1s_ref[...], precision=hp, preferred_element_type=F32))
    fr = fr_ref[...]
    h = jnp.sin(fr * (pre + b1_ref[...]))
    h = jnp.sin(fr * (jnp.dot(h, w2_ref[...], precision=hp, preferred_element_type=F32) + b2_ref[...]))
    h = jnp.dot(h, w3_ref[0], precision=hp, preferred_element_type=F32)
    decay = jnp.exp(-t * dl_ref[...])
    kk = jnp.where(l_idx == 0, 0.0, h * decay)
    kk_ref[...] = kk

    @pl.when(i == 0)
    def _():
        asum_ref[...] = jnp.zeros(asum_ref.shape, F32)

    asum_ref[...] += jnp.sum(jnp.abs(kk), axis=0, keepdims=True)


def _filters(bands, w1t, w1c, w1s, b1, w2, b2, w3d, fr, dl, seq):
    nl = 2 * seq // TL_FILT
    half = seq // TL_FILT
    ncol = HY_ORDER * HY_C
    return pl.pallas_call(
        functools.partial(_filter_kernel, seq=seq), name="hyfilt",
        grid=(nl,),
        in_specs=[_full(bands.shape), _full(w1t.shape), _full(w1c.shape), _full(w1s.shape),
                  _full(b1.shape), _full(w2.shape), _full(b2.shape),
                  pl.BlockSpec((1, FILTER_HIDDEN, ncol), lambda i: (jnp.where(i < half, 1, 0), 0, 0)),
                  _full(fr.shape), _full(dl.shape)],
        out_specs=[pl.BlockSpec((TL_FILT, ncol), lambda i: (i, 0)),
                   pl.BlockSpec((1, ncol), lambda i: (0, 0))],
        out_shape=[jax.ShapeDtypeStruct((2 * seq, ncol), F32),
                   jax.ShapeDtypeStruct((1, ncol), F32)],
        compiler_params=_cparams(("arbitrary",)),
    )(bands, w1t, w1c, w1s, b1, w2, b2, w3d, fr, dl)


def _spec_kernel(kk_ref, asum_ref, f_ref, g_ref, *, lb):
    nhalf = kk_ref.shape[0] // lb
    kkn = (kk_ref[...] / asum_ref[...]).astype(BF16)
    x8 = jnp.concatenate([kkn[m * lb:(m + 1) * lb] for m in range(nhalf)], axis=1)
    hh = jnp.dot(f_ref[...], x8, preferred_element_type=F32)
    ct = kk_ref.shape[1]
    row = lax.broadcasted_iota(jnp.int32, (2 * lb, 1), 0)
    sgn = (1 - 2 * (row & 1)).astype(F32)
    low = row <= lb
    for dd in range(nhalf - 1):
        h_neg = hh[:, dd * ct:(dd + 1) * ct]
        h_pos = hh[:, (dd + 1) * ct:(dd + 2) * ct]
        h0 = kkn[dd * lb:dd * lb + 1, :].astype(F32)
        g_ref[dd] = h_pos + sgn * (h_neg - jnp.where(low, h0, 0.0))


def _spectra(kk, asum, fmat, lb):
    ncol = kk.shape[1]
    nd = 2 * P_BLK - 1
    return pl.pallas_call(
        functools.partial(_spec_kernel, lb=lb), name="hyspec",
        grid=(ncol // CT,),
        in_specs=[pl.BlockSpec((kk.shape[0], CT), lambda c: (0, c)),
                  pl.BlockSpec((1, CT), lambda c: (0, c)),
                  _full(fmat.shape)],
        out_specs=pl.BlockSpec((nd, 2 * lb, CT), lambda c: (0, 0, c)),
        out_shape=jax.ShapeDtypeStruct((nd, 2 * lb, ncol), F32),
        compiler_params=_cparams(("parallel",)),
    )(kk, asum, fmat)


def _sconv(u, w3, b):
    s = u.shape[0]
    row = lax.broadcasted_iota(jnp.int32, (s, 1), 0)
    prev = jnp.where(row == 0, 0.0, pltpu.roll(u, 1, axis=0))
    nxt = jnp.where(row == s - 1, 0.0, pltpu.roll(u, s - 1, axis=0))
    return b + prev * w3[0:1, :] + u * w3[1:2, :] + nxt * w3[2:3, :]


def _hyena_kernel(a_ref, gt_ref, cw_ref, cb_ref, g_ref, d_ref, f_ref, fi_ref, o_ref,
                  u_sc, y_sc, *, conv_a, lb):
    ct = a_ref.shape[2]
    a = a_ref[0]
    gate = gt_ref[0]
    if conv_a:
        a = _sconv(a, cw_ref[0], cb_ref[0:1, :])
    gate = _sconv(gate, cw_ref[1], cb_ref[1:2, :])

    ab = a.astype(BF16)
    x4 = jnp.concatenate([ab[j * lb:(j + 1) * lb] for j in range(P_BLK)], axis=1)
    u_sc[...] = jnp.dot(f_ref[...], x4, preferred_element_type=F32)

    def pointwise(r0, rows, packed_row):
        ure = [u_sc[pl.ds(r0, rows), j * ct:(j + 1) * ct] for j in range(P_BLK)]
        uim = [u_sc[pl.ds(lb + r0, rows), j * ct:(j + 1) * ct] for j in range(P_BLK)]
        if packed_row:
            first = lax.broadcasted_iota(jnp.int32, (rows, 1), 0) == 0
        for i in range(P_BLK):
            yre = yim = None
            dc = ny = None
            for j in range(P_BLK):
                dd = i - j + P_BLK - 1
                gre = g_ref[dd, pl.ds(r0, rows), :]
                gim = g_ref[dd, pl.ds(lb + r0, rows), :]
                rr = ure[j] * gre
                ii = uim[j] * gim
                tre = rr - ii
                tim = ure[j] * gim + uim[j] * gre
                yre = tre if yre is None else yre + tre
                yim = tim if yim is None else yim + tim
                if packed_row:
                    dc = rr if dc is None else dc + rr
                    ny = ii if ny is None else ny + ii
            if packed_row:
                yre = jnp.where(first, dc, yre)
                yim = jnp.where(first, ny, yim)
            y_sc[pl.ds(r0, rows), i * ct:(i + 1) * ct] = yre.astype(BF16)
            y_sc[pl.ds(lb + r0, rows), i * ct:(i + 1) * ct] = yim.astype(BF16)

    def chunk(r, carry):
        pointwise(pl.multiple_of(r * RC, RC), RC, False)
        return carry

    lax.fori_loop(0, lb // RC, chunk, 0)
    pointwise(0, 16, True)

    y4 = jnp.dot(fi_ref[...], y_sc[...], preferred_element_type=F32)
    y = jnp.concatenate([y4[:, i * ct:(i + 1) * ct] for i in range(P_BLK)], axis=0)
    o_ref[0] = (y + a * d_ref[...]) * gate


def _hyena_stage(a_arr, a_blk0, g_arr, g_blk0, cw, cb, gspec, g_blk0_spec, d, fmat, fimat, conv_a, lb):
    b, seq, _ = a_arr.shape
    nct = HY_C // CT
    nd = 2 * P_BLK - 1
    return pl.pallas_call(
        functools.partial(_hyena_kernel, conv_a=conv_a, lb=lb), name="hyena",
        grid=(nct, b),
        in_specs=[pl.BlockSpec((1, seq, CT), lambda c, bi: (bi, 0, a_blk0 + c)),
                  pl.BlockSpec((1, seq, CT), lambda c, bi: (bi, 0, g_blk0 + c)),
                  pl.BlockSpec((2, 3, CT), lambda c, bi: (0, 0, c)),
                  pl.BlockSpec((2, CT), lambda c, bi: (0, c)),
                  pl.BlockSpec((nd, 2 * lb, CT), lambda c, bi: (0, 0, g_blk0_spec + c),
                               pipeline_mode=pl.Buffered(1)),
                  pl.BlockSpec((1, CT), lambda c, bi: (0, c)),
                  _full(fmat.shape), _full(fimat.shape)],
        out_specs=pl.BlockSpec((1, seq, CT), lambda c, bi: (bi, 0, c)),
        out_shape=jax.ShapeDtypeStruct((b, seq, HY_C), F32),
        scratch_shapes=[pltpu.VMEM((2 * lb, P_BLK * CT), F32), pltpu.VMEM((2 * lb, P_BLK * CT), BF16)],
        compiler_params=_cparams(("parallel", "parallel")),
    )(a_arr, g_arr, cw, cb, gspec, d, fmat, fimat)


def _group_norm_tile(xs, g):
    lane = lax.broadcasted_iota(jnp.int32, xs.shape, 1)
    lo = lane < GROUP
    sq = xs * xs
    s_lo = jnp.sum(jnp.where(lo, sq, 0.0), axis=-1, keepdims=True)
    s_hi = jnp.sum(jnp.where(lo, 0.0, sq), axis=-1, keepdims=True)
    r = jnp.where(lo, lax.rsqrt(s_lo / GROUP + EPS), lax.rsqrt(s_hi / GROUP + EPS))
    return xs * r * g


def _ffn2_kernel(x_ref, a_ref, h_ref, hg_ref, wo_ref, g_ref, wg_ref, wu_ref, wd_ref, gf_ref, o_ref):
    tiles = []
    n_a = a_ref.shape[1] // LANES
    for s in range(D_MODEL // LANES):
        src = a_ref[:, s * LANES:(s + 1) * LANES] if s < n_a else \
            h_ref[:, (s - n_a) * LANES:(s - n_a + 1) * LANES]
        tiles.append(_group_norm_tile(src, hg_ref[:, s * LANES:(s + 1) * LANES]).astype(BF16))
    mix = jnp.concatenate(tiles, axis=1)
    x = x_ref[...] + jnp.dot(mix, wo_ref[...], preferred_element_type=F32)
    y = _swiglu_residual(x, g_ref, wg_ref, wu_ref, wd_ref)
    o_ref[...] = _rms(y, gf_ref[...])


def _ffn2(x1, a2d, h2d, hg, wo, g, wg, wu, wd, gf):
    t = x1.shape[0]
    row = lambda w: pl.BlockSpec((TM_FFN, w), lambda i: (i, 0))
    return pl.pallas_call(
        _ffn2_kernel, name="ffn2",
        grid=(t // TM_FFN,),
        in_specs=[row(D_MODEL), row(a2d.shape[1]), row(h2d.shape[1]), _full(hg.shape), _full(wo.shape),
                  _full(g.shape), _full(wg.shape), _full(wu.shape), _full(wd.shape), _full(gf.shape)],
        out_specs=row(D_MODEL),
        out_shape=jax.ShapeDtypeStruct(x1.shape, F32),
        compiler_params=_cparams(("parallel",)),
    )(x1, a2d, h2d, hg, wo, g, wg, wu, wd, gf)


def _rope_tables(seq):
    inv = 1.0 / (ROPE_THETA ** (jnp.arange(0, ROPE, 2, dtype=F32) / ROPE))
    ang = jnp.arange(seq, dtype=F32)[:, None] * inv[None, :]
    cos, sin = jnp.cos(ang), jnp.sin(ang)
    z64 = jnp.zeros((seq, NOPE), F32)
    z32 = jnp.zeros((seq, HEAD_PAD - NOPE - ROPE), F32)
    cq_t = jnp.concatenate([jnp.ones((seq, NOPE), F32), cos, cos, z32], axis=1)
    ck_t = jnp.concatenate([z64, cos, cos, z32], axis=1)
    sn_t = jnp.concatenate([z64, -sin, sin, z32], axis=1)
    return cq_t, ck_t, sn_t


def _dft_matrices(lb):
    n2 = 2 * lb
    f = jnp.arange(lb, dtype=jnp.int32)[:, None]
    n = jnp.arange(lb, dtype=jnp.int32)[None, :]
    ang = (2.0 * math.pi / n2) * ((f * n) % n2).astype(F32)
    cos, sin = jnp.cos(ang), jnp.sin(ang)
    alt = (1 - 2 * (jnp.arange(lb, dtype=jnp.int32) % 2)).astype(F32)
    first = (f == 0)
    fwd = jnp.concatenate([cos, jnp.where(first, alt[None, :], -sin)], axis=0)
    inv_re = jnp.where(first, 1.0 / n2, (2.0 / n2) * cos)
    inv_im = jnp.where(first, alt[None, :] / n2, (-2.0 / n2) * sin)
    inv = jnp.concatenate([inv_re, inv_im], axis=0).T
    return fwd.astype(BF16), inv.astype(BF16)


def kernel(x, ffn1_norm_g, ffn1_w_gate, ffn1_w_up, ffn1_w_down, mix_norm_g, w_in, q_norm_g, w_uq, kv_norm_g, w_ukv, hyena_conv_w, hyena_conv_b, filt_w1, filt_b1, filt_w2, filt_b2, filt_w3, filt_freq, hyena_d, head_norm_g, w_out, ffn2_norm_g, ffn2_w_gate, ffn2_w_up, ffn2_w_down, final_norm_g):
    b, seq, d = x.shape
    t = b * seq
    lb = seq // P_BLK
    l = 0
    row = lambda v: v.reshape(1, -1)

    x1 = _ffn1(x.reshape(t, d), row(ffn1_norm_g[l]), ffn1_w_gate[l].astype(BF16),
               ffn1_w_up[l].astype(BF16), ffn1_w_down[l].astype(BF16))

    wi = w_in[l]
    o_kr = Q_RANK + KV_RANK
    half = ROPE // 2
    zc = lambda n: jnp.zeros((d, n), F32)
    kr = wi[:, o_kr:o_kr + ROPE]
    win_p = jnp.concatenate([
        wi[:, :o_kr],
        zc(NOPE), kr, zc(HEAD_PAD - NOPE - ROPE),
        zc(NOPE), kr[:, half:], kr[:, :half], zc(HEAD_PAD - NOPE - ROPE),
        wi[:, o_kr + ROPE:]], axis=1).astype(BF16)
    wq3 = w_uq[l].reshape(Q_RANK, MLA_HEADS, NOPE + ROPE)
    zq = lambda n: jnp.zeros((Q_RANK, MLA_HEADS, n), F32)
    wq_p = jnp.concatenate([wq3, zq(HEAD_PAD - NOPE - ROPE)], axis=2).reshape(Q_RANK, -1).astype(BF16)
    wqs_p = jnp.concatenate([zq(NOPE), wq3[:, :, NOPE + half:], wq3[:, :, NOPE:NOPE + half],
                             zq(HEAD_PAD - NOPE - ROPE)], axis=2).reshape(Q_RANK, -1).astype(BF16)
    wkv3 = w_ukv[l].reshape(KV_RANK, MLA_HEADS, NOPE + V_DIM)
    wk_p = jnp.concatenate([wkv3[:, :, :NOPE], jnp.zeros((KV_RANK, MLA_HEADS, HEAD_PAD - NOPE), F32)],
                           axis=2).reshape(KV_RANK, -1).astype(BF16)
    wv_p = wkv3[:, :, NOPE:].reshape(KV_RANK, -1).astype(BF16)
    cq_t, ck_t, sn_t = _rope_tables(seq)
    qt, k4, vt, u3 = _inproj(x1.reshape(b, seq, d), row(mix_norm_g[l]), win_p, row(q_norm_g[l]),
                             wq_p.T, wqs_p.T, row(kv_norm_g[l]), wk_p, wv_p.T,
                             cq_t.T, sn_t.T, ck_t, sn_t)

    a = _attention(qt, k4, vt)

    bands = jnp.linspace(1e-4, FILTER_BANDS - 1, FILTER_BANDS, dtype=F32).reshape(1, -1)
    w1 = filt_w1[l].astype(F32)
    w3d = filt_w3[l].astype(F32).reshape(FILTER_HIDDEN, HY_ORDER, 2, HY_C).transpose(2, 0, 1, 3)
    w3d = w3d.reshape(2, FILTER_HIDDEN, HY_ORDER * HY_C)
    deltas = jnp.abs(jnp.linspace(MIN_DECAY, MAX_DECAY, HY_C, dtype=F32))
    dl = jnp.tile(deltas, HY_ORDER).reshape(1, -1)
    kk, asum = _filters(bands, w1[0:1], w1[1:1 + FILTER_BANDS], w1[1 + FILTER_BANDS:],
                        row(filt_b1[l].astype(F32)), filt_w2[l].astype(F32), row(filt_b2[l].astype(F32)),
                        w3d, row(filt_freq[l].astype(F32)), dl, seq)
    fmat, fimat = _dft_matrices(lb)
    gspec = _spectra(kk, asum, fmat, lb)

    nct = HY_C // CT
    cw = hyena_conv_w[l].reshape(3, 3, HY_C).transpose(1, 0, 2)
    cb = hyena_conv_b[l].reshape(3, HY_C)
    dsk = hyena_d[l].astype(F32)
    z1 = _hyena_stage(u3, 0, u3, nct, cw[0:2], cb[0:2], gspec, 0, dsk[0:1], fmat, fimat, True, lb)
    cw2 = jnp.stack([cw[0], cw[2]])
    cb2 = jnp.stack([cb[0], cb[2]])
    hy = _hyena_stage(z1, 0, u3, 2 * nct, cw2, cb2, gspec, nct, dsk[1:2], fmat, fimat, False, lb)

    out = _ffn2(x1, a.reshape(t, -1), hy.reshape(t, -1), row(head_norm_g[l]), w_out[l].astype(BF16),
                row(ffn2_norm_g[l]), ffn2_w_gate[l].astype(BF16), ffn2_w_up[l].astype(BF16),
                ffn2_w_down[l].astype(BF16), row(final_norm_g))
    return out.reshape(b, seq, d)
```

```python
import functools
import math

import jax
import jax.numpy as jnp
from jax import lax
from jax.experimental import pallas as pl
from jax.experimental.pallas import tpu as pltpu

F32 = jnp.float32
BF16 = jnp.bfloat16

D_MODEL = 1024
MLA_HEADS = 8
NOPE = 64
ROPE = 32
V_DIM = 64
Q_RANK = 256
KV_RANK = 128
ROPE_THETA = 10000.0
HY_C = 512
FILTER_BANDS = 16
FILTER_HIDDEN = 64
HY_ORDER = 2
FFN_HIDDEN = 2816
FFN_RES = 0.5
EPS = 1e-6
GROUP = 64
DECAY_TARGET = 1e-2
MAX_DECAY = math.log(DECAY_TARGET) / 0.3
MIN_DECAY = math.log(DECAY_TARGET) / 1.5

LANES = 128
HALO = 8
BF16_ROWS = 16
HEAD_PAD = 128
VMEM_LIMIT = 56 * 1024 * 1024

TM_FFN = 512
MXU_DIM = 256
FFN_CHUNK = 6 * MXU_DIM
TS_PROJ = 512
TQ = 512
TK = 1024
ATT_SLOTS = 2
REF_ROWS = 16
ATT_SAFE_EXP = 60.0
CT = 128
P_BLK = 4
RC = 32
TL_FILT = 512


def _cparams(sem, flags=None):
    return pltpu.CompilerParams(dimension_semantics=sem, vmem_limit_bytes=VMEM_LIMIT, flags=flags)


def _rms(x, g):
    ms = jnp.mean(x * x, axis=-1, keepdims=True)
    return x * lax.rsqrt(ms + EPS) * g


def _swiglu_residual(x, g_ref, wg_ref, wu_ref, wd_ref):
    xn = _rms(x, g_ref[...]).astype(BF16)
    acc = jnp.zeros(x.shape, F32)
    for c0 in range(0, FFN_HIDDEN, FFN_CHUNK):
        c1 = min(c0 + FFN_CHUNK, FFN_HIDDEN)
        gate = jnp.dot(xn, wg_ref[:, c0:c1], preferred_element_type=F32)
        up = jnp.dot(xn, wu_ref[:, c0:c1], preferred_element_type=F32)
        h = (gate * jax.nn.sigmoid(gate) * up).astype(BF16)
        acc = acc + jnp.dot(h, wd_ref[c0:c1, :], preferred_element_type=F32)
    return x + FFN_RES * acc


def _ffn1_kernel(x_ref, g_ref, wg_ref, wu_ref, wd_ref, o_ref):
    o_ref[...] = _swiglu_residual(x_ref[...], g_ref, wg_ref, wu_ref, wd_ref)


def _full(shape):
    return pl.BlockSpec(shape, lambda *_: (0,) * len(shape))


def _ffn1(x2d, g, wg, wu, wd):
    t = x2d.shape[0]
    row = pl.BlockSpec((TM_FFN, D_MODEL), lambda i: (i, 0))
    return pl.pallas_call(
        _ffn1_kernel, name="ffn1",
        grid=(t // TM_FFN,),
        in_specs=[row, _full(g.shape), _full(wg.shape), _full(wu.shape), _full(wd.shape)],
        out_specs=row,
        out_shape=jax.ShapeDtypeStruct(x2d.shape, F32),
        compiler_params=_cparams(("parallel",)),
    )(x2d, g, wg, wu, wd)


def _inproj_kernel(x_ref, xp_ref, xn_ref, g_ref, win_ref, cw_ref, cb_ref, qg_ref, wqt_ref, wqst_ref,
                   kvg_ref, wk_ref, wvt_ref, cqt_ref, snt_ref, ck_ref, sn_ref,
                   q_ref, k_ref, v_ref, u_ref, *, scale):
    ts = x_ref.shape[1]
    i = pl.program_id(1)
    x_all = jnp.concatenate([xp_ref[0], x_ref[0], xn_ref[0]], axis=0)
    xn = _rms(x_all, g_ref[...]).astype(BF16)
    u_all = jnp.dot(xn, win_ref[...], preferred_element_type=F32)
    u = u_all[HALO:HALO + ts]
    c_q = u[:, 0:Q_RANK]
    c_kv = u[:, Q_RANK:Q_RANK + KV_RANK]
    o_pe = Q_RANK + KV_RANK
    kpe_a = u[:, o_pe:o_pe + LANES]
    kpe_b = u[:, o_pe + LANES:o_pe + 2 * LANES]

    o_hy = o_pe + 2 * LANES
    keep_prev = (i > 0).astype(F32)
    keep_next = (i < pl.num_programs(1) - 1).astype(F32)
    rows = ts + 2 * HALO
    uh = jnp.concatenate([u_all[0:HALO, o_hy:] * keep_prev, u_all[HALO:HALO + ts, o_hy:],
                          u_all[HALO + ts:, o_hy:] * keep_next], axis=0)
    prev = pltpu.roll(uh * cw_ref[0:1, :], 1, axis=0)
    nxt = pltpu.roll(uh * cw_ref[2:3, :], rows - 1, axis=0)
    conv = (cb_ref[...] + prev) + uh * cw_ref[1:2, :] + nxt
    u_ref[0] = conv[HALO:HALO + ts]

    nt = (((1,), (1,)), ((), ()))
    cqn = _rms(c_q, qg_ref[...]).astype(BF16)
    qt = lax.dot_general(wqt_ref[...], cqn, nt, preferred_element_type=F32)
    qst = lax.dot_general(wqst_ref[...], cqn, nt, preferred_element_type=F32)
    cq_t = cqt_ref[...]
    sn_tt = snt_ref[...]
    for h in range(MLA_HEADS):
        sl = slice(h * HEAD_PAD, (h + 1) * HEAD_PAD)
        q_ref[0, sl, :] = ((qt[sl] * cq_t + qst[sl] * sn_tt) * scale).astype(BF16)

    ckn = _rms(c_kv, kvg_ref[...]).astype(BF16)
    kn = jnp.dot(ckn, wk_ref[...], preferred_element_type=F32)
    k_pe = kpe_a * ck_ref[...] + kpe_b * sn_ref[...]
    for h in range(MLA_HEADS):
        sl = slice(h * HEAD_PAD, (h + 1) * HEAD_PAD)
        k_ref[0, h] = (kn[:, sl] + k_pe).astype(BF16)
    vt = lax.dot_general(wvt_ref[...], ckn, nt, preferred_element_type=F32)
    v_ref[0] = vt.reshape(MLA_HEADS // 2, 2 * V_DIM, vt.shape[1]).astype(BF16)


def _inproj(x1, g, win, cw, cb, qg, wqt, wqst, kvg, wk, wvt, cq_tt, sn_tt, ck_t, sn_t):
    b, seq, _ = x1.shape
    ts = TS_PROJ
    scale = (NOPE + ROPE) ** -0.5 * math.log2(math.e)
    hp = MLA_HEADS * HEAD_PAD
    hpt = ts // HALO
    last = seq // HALO - 1
    return pl.pallas_call(
        functools.partial(_inproj_kernel, scale=scale), name="inproj",
        grid=(b, seq // ts),
        in_specs=[pl.BlockSpec((1, ts, D_MODEL), lambda bi, i: (bi, i, 0)),
                  pl.BlockSpec((1, HALO, D_MODEL), lambda bi, i: (bi, jnp.maximum(i * hpt - 1, 0), 0)),
                  pl.BlockSpec((1, HALO, D_MODEL), lambda bi, i: (bi, jnp.minimum((i + 1) * hpt, last), 0)),
                  _full(g.shape), _full(win.shape), _full(cw.shape), _full(cb.shape),
                  _full(qg.shape), _full(wqt.shape),
                  _full(wqst.shape), _full(kvg.shape), _full(wk.shape), _full(wvt.shape),
                  pl.BlockSpec((HEAD_PAD, ts), lambda bi, i: (0, i)),
                  pl.BlockSpec((HEAD_PAD, ts), lambda bi, i: (0, i)),
                  pl.BlockSpec((ts, LANES), lambda bi, i: (i, 0)),
                  pl.BlockSpec((ts, LANES), lambda bi, i: (i, 0))],
        out_specs=[pl.BlockSpec((1, hp, ts), lambda bi, i: (bi, 0, i)),
                   pl.BlockSpec((1, MLA_HEADS, ts, HEAD_PAD), lambda bi, i: (bi, 0, i, 0)),
                   pl.BlockSpec((1, MLA_HEADS // 2, 2 * V_DIM, ts), lambda bi, i: (bi, 0, 0, i)),
                   pl.BlockSpec((1, ts, 3 * HY_C), lambda bi, i: (bi, i, 0))],
        out_shape=[jax.ShapeDtypeStruct((b, hp, seq), BF16),
                   jax.ShapeDtypeStruct((b, MLA_HEADS, seq, HEAD_PAD), BF16),
                   jax.ShapeDtypeStruct((b, MLA_HEADS // 2, 2 * V_DIM, seq), BF16),
                   jax.ShapeDtypeStruct((b, seq, 3 * HY_C), F32)],
        compiler_params=_cparams(("parallel", "parallel")),
    )(x1, x1, x1, g, win, cw, cb, qg, wqt, wqst, kvg, wk, wvt, cq_tt, sn_tt, ck_t, sn_t)


def _attn_robust(q_ref, k_ref, v_ref, o_ref, s_buf, p_buf, mc_buf, al_buf, m_sc, l_sc, acc_sc):
    seq = k_ref.shape[2]
    nk = seq // TK
    nblk = 2 * nk

    def split(n):
        if isinstance(n, int):
            return n // nk, (n % nk) * TK, (n // nk) * HEAD_PAD
        h = (n >= nk).astype(jnp.int32)
        return h, pl.multiple_of((n - h * nk) * TK, TK), pl.multiple_of(h * HEAD_PAD, HEAD_PAD)

    def scores(n, slot):
        h, off, qoff = split(n)
        k = k_ref[0, h, pl.ds(off, TK), :]
        s = jnp.dot(k, q_ref[0, pl.ds(qoff, HEAD_PAD), :], preferred_element_type=F32)
        s_buf[slot] = s
        mc_buf[slot] = jnp.max(s, axis=0, keepdims=True)

    def softmax(n, slot):
        h, _, _ = split(n)
        m_prev = m_sc[h]
        m_new = jnp.maximum(m_prev, mc_buf[slot])
        alpha = jnp.exp2(m_prev - m_new)
        p = jnp.exp2(s_buf[slot] - m_new)
        l_sc[h] = alpha * l_sc[h] + jnp.sum(p, axis=0, keepdims=True)
        p_buf[slot] = p.astype(BF16)
        al_buf[slot] = alpha
        m_sc[h] = m_new

    def values(n, slot):
        h, off, _ = split(n)
        vt = v_ref[0, 0, :, pl.ds(off, TK)]
        pv = jnp.dot(vt, p_buf[slot], preferred_element_type=F32)
        acc_sc[h] = al_buf[slot] * acc_sc[h] + pv

    m_sc[...] = jnp.full(m_sc.shape, -jnp.inf, F32)
    l_sc[...] = jnp.zeros(l_sc.shape, F32)
    acc_sc[...] = jnp.zeros(acc_sc.shape, F32)

    scores(0, 0)
    scores(1, 1)
    softmax(0, 0)

    def body(i, carry):
        n = 2 * i
        scores(n + 2, 0)
        softmax(n + 1, 1)
        values(n, 0)
        scores(n + 3, 1)
        softmax(n + 2, 0)
        values(n + 1, 1)
        return carry

    lax.fori_loop(0, (nblk - 2) // 2, body, 0)
    softmax(nblk - 1, 1)
    values(nblk - 2, 0)
    values(nblk - 1, 1)

    row = lax.broadcasted_iota(jnp.int32, acc_sc.shape[1:], 0)
    out_t = jnp.where(row < V_DIM, acc_sc[0] / l_sc[0], acc_sc[1] / l_sc[1])
    o_ref[0] = out_t.T


def _attn_kernel(q_ref, k_ref, v_ref, o_ref, *scratch):
    seq = k_ref.shape[2]
    outs = []
    exceed = None
    for h in range(2):
        qt = q_ref[0, h * HEAD_PAD:(h + 1) * HEAD_PAD, :]
        s0 = jnp.dot(k_ref[0, h, 0:REF_ROWS, :], qt, preferred_element_type=F32)
        m_ref = jnp.max(s0, axis=0, keepdims=True)
        cmax = m_ref
        l = acc = None
        for j in range(seq // TK):
            ks = slice(j * TK, (j + 1) * TK)
            s = jnp.dot(k_ref[0, h, ks, :], qt, preferred_element_type=F32)
            cmax = jnp.maximum(cmax, jnp.max(s, axis=0, keepdims=True))
            p = jnp.exp2(s - m_ref)
            p_sum = jnp.sum(p, axis=0, keepdims=True)
            pv = jnp.dot(v_ref[0, 0, :, ks], p.astype(BF16), preferred_element_type=F32)
            l = p_sum if l is None else l + p_sum
            acc = pv if acc is None else acc + pv
        outs.append(acc / l)
        e = jnp.max(cmax - m_ref)
        exceed = e if exceed is None else jnp.maximum(exceed, e)
    row = lax.broadcasted_iota(jnp.int32, outs[0].shape, 0)
    o_ref[0] = jnp.where(row < V_DIM, outs[0], outs[1]).T

    @pl.when(exceed > ATT_SAFE_EXP)
    def _():
        _attn_robust(q_ref, k_ref, v_ref, o_ref, *scratch)


def _attention(qt, k4, vt):
    b, _, seq = qt.shape
    pairs = MLA_HEADS // 2
    return pl.pallas_call(
        _attn_kernel, name="attn",
        grid=(b, pairs, seq // TQ),
        in_specs=[pl.BlockSpec((1, 2 * HEAD_PAD, TQ), lambda bi, j, qi: (bi, j, qi)),
                  pl.BlockSpec((1, 2, seq, HEAD_PAD), lambda bi, j, qi: (bi, j, 0, 0)),
                  pl.BlockSpec((1, 1, 2 * V_DIM, seq), lambda bi, j, qi: (bi, j, 0, 0))],
        out_specs=pl.BlockSpec((1, TQ, 2 * V_DIM), lambda bi, j, qi: (bi, qi, j)),
        out_shape=jax.ShapeDtypeStruct((b, seq, MLA_HEADS * V_DIM), F32),
        scratch_shapes=[pltpu.VMEM((ATT_SLOTS, TK, TQ), F32), pltpu.VMEM((ATT_SLOTS, TK, TQ), BF16),
                        pltpu.VMEM((ATT_SLOTS, 1, TQ), F32), pltpu.VMEM((ATT_SLOTS, 1, TQ), F32),
                        pltpu.VMEM((2, 1, TQ), F32), pltpu.VMEM((2, 1, TQ), F32),
                        pltpu.VMEM((2, 2 * V_DIM, TQ), F32)],
        compiler_params=_cparams(("parallel", "parallel", "parallel")),
    )(qt, k4, vt)


def _filter_kernel(bands_ref, w1t_ref, w1c_ref, w1s_ref, b1_ref, w2_ref, b2_ref, w3_ref,
                   fr_ref, dl_ref, kk_ref, asum_ref, *, seq):
    i = pl.program_id(0)
    hp = lax.Precision.HIGHEST
    l_idx = i * TL_FILT + lax.broadcasted_iota(jnp.int32, (1, TL_FILT), 1)
    pos_i = jnp.where(l_idx < seq, seq - l_idx, l_idx - seq)
    pos = pos_i.astype(F32)
    t = pos / float(max(seq - 1, 1))
    ang = (2.0 * math.pi * pos / seq) * bands_ref[...]
    pre = (t * w1t_ref[...]
           + jnp.dot(w1c_ref[...], jnp.cos(ang), precision=hp, preferred_element_type=F32)
           + jnp.dot(w1s_ref[...], -jnp.sin(ang), precision=hp, preferred_element_type=F32))
    fr = fr_ref[...]
    h = jnp.sin(fr * (pre + b1_ref[...]))
    h = jnp.sin(fr * (jnp.dot(w2_ref[...], h, precision=hp, preferred_element_type=F32) + b2_ref[...]))
    h = jnp.dot(w3_ref[0], h, precision=hp, preferred_element_type=F32)
    decay = jnp.exp(-t * dl_ref[...])
    kk = jnp.where(l_idx == 0, 0.0, h * decay)
    kk_ref[...] = kk

    @pl.when(i == 0)
    def _():
        asum_ref[...] = jnp.zeros(asum_ref.shape, F32)

    asum_ref[...] += jnp.sum(jnp.abs(kk), axis=1, keepdims=True)


def _filters(bands, w1t, w1c, w1s, b1, w2, b2, w3d, fr, dl, seq):
    nl = 2 * seq // TL_FILT
    half = seq // TL_FILT
    ncol = HY_ORDER * HY_C
    return pl.pallas_call(
        functools.partial(_filter_kernel, seq=seq), name="hyfilt",
        grid=(nl,),
        in_specs=[_full(bands.shape), _full(w1t.shape), _full(w1c.shape), _full(w1s.shape),
                  _full(b1.shape), _full(w2.shape), _full(b2.shape),
                  pl.BlockSpec((1, ncol, FILTER_HIDDEN), lambda i: (jnp.where(i < half, 1, 0), 0, 0)),
                  _full(fr.shape), _full(dl.shape)],
        out_specs=[pl.BlockSpec((ncol, TL_FILT), lambda i: (0, i)),
                   pl.BlockSpec((ncol, 1), lambda i: (0, 0))],
        out_shape=[jax.ShapeDtypeStruct((ncol, 2 * seq), F32),
                   jax.ShapeDtypeStruct((ncol, 1), F32)],
        compiler_params=_cparams(("arbitrary",)),
    )(bands, w1t, w1c, w1s, b1, w2, b2, w3d, fr, dl)


def _spec_kernel(kk_ref, asum_ref, f_ref, g_ref, *, lb):
    ct = kk_ref.shape[0]
    nhalf = kk_ref.shape[1] // lb
    kkn = (kk_ref[...] / asum_ref[...]).astype(BF16)
    x8 = jnp.concatenate([kkn[:, m * lb:(m + 1) * lb] for m in range(nhalf)], axis=0)
    hh = lax.dot_general(f_ref[...], x8, (((1,), (1,)), ((), ())),
                         preferred_element_type=F32)
    row = lax.broadcasted_iota(jnp.int32, (2 * lb, 1), 0)
    sgn = (1 - 2 * (row & 1)).astype(F32)
    low = row <= lb
    for dd in range(nhalf - 1):
        h_neg = hh[0:2 * lb, dd * ct:(dd + 1) * ct]
        h_pos = hh[0:2 * lb, (dd + 1) * ct:(dd + 2) * ct]
        h0 = hh[2 * lb:2 * lb + 1, dd * ct:(dd + 1) * ct]
        g_ref[dd] = h_pos + sgn * (h_neg - jnp.where(low, h0, 0.0))


def _spectra(kk, asum, fmat_ext, lb):
    ncol = kk.shape[0]
    nd = 2 * P_BLK - 1
    return pl.pallas_call(
        functools.partial(_spec_kernel, lb=lb), name="hyspec",
        grid=(ncol // CT,),
        in_specs=[pl.BlockSpec((CT, kk.shape[1]), lambda c: (c, 0)),
                  pl.BlockSpec((CT, 1), lambda c: (c, 0)),
                  _full(fmat_ext.shape)],
        out_specs=pl.BlockSpec((nd, 2 * lb, CT), lambda c: (0, 0, c)),
        out_shape=jax.ShapeDtypeStruct((nd, 2 * lb, ncol), F32),
        compiler_params=_cparams(("parallel",)),
    )(kk, asum, fmat_ext)


def _hyena_kernel(a_ref, gt_ref, g_ref, d_ref, f_ref, fi_ref, o_ref, u_sc, y_sc, *, lb):
    ct = a_ref.shape[2]
    x4 = jnp.concatenate([a_ref[0, j * lb:(j + 1) * lb, :].astype(BF16) for j in range(P_BLK)],
                         axis=1)
    u_sc[...] = jnp.dot(f_ref[...], x4, preferred_element_type=F32)

    def pointwise(r0, rows, packed_row):
        ure = [u_sc[pl.ds(r0, rows), j * ct:(j + 1) * ct] for j in range(P_BLK)]
        uim = [u_sc[pl.ds(lb + r0, rows), j * ct:(j + 1) * ct] for j in range(P_BLK)]
        if packed_row:
            first = lax.broadcasted_iota(jnp.int32, (rows, 1), 0) == 0
        for i in range(P_BLK):
            yre = yim = None
            dc = ny = None
            for j in range(P_BLK):
                dd = i - j + P_BLK - 1
                gre = g_ref[dd, pl.ds(r0, rows), :]
                gim = g_ref[dd, pl.ds(lb + r0, rows), :]
                rr = ure[j] * gre
                ii = uim[j] * gim
                tre = rr - ii
                tim = ure[j] * gim + uim[j] * gre
                yre = tre if yre is None else yre + tre
                yim = tim if yim is None else yim + tim
                if packed_row:
                    dc = rr if dc is None else dc + rr
                    ny = ii if ny is None else ny + ii
            if packed_row:
                yre = jnp.where(first, dc, yre)
                yim = jnp.where(first, ny, yim)
            y_sc[pl.ds(r0, rows), i * ct:(i + 1) * ct] = yre.astype(BF16)
            y_sc[pl.ds(lb + r0, rows), i * ct:(i + 1) * ct] = yim.astype(BF16)

    def chunk(r, carry):
        pointwise(pl.multiple_of(r * RC, RC), RC, False)
        return carry

    lax.fori_loop(0, lb // RC, chunk, 0)
    pointwise(0, 16, True)

    y4 = jnp.dot(fi_ref[...], y_sc[...], preferred_element_type=F32)
    for i in range(P_BLK):
        rs = slice(i * lb, (i + 1) * lb)
        o_ref[0, rs, :] = (y4[:, i * ct:(i + 1) * ct] + a_ref[0, rs, :] * d_ref[...]) * gt_ref[0, rs, :]


def _hyena_stage(a_arr, a_blk0, g_arr, g_blk0, gspec, g_blk0_spec, d, fmat, fimat, lb):
    b, seq, _ = a_arr.shape
    nct = HY_C // CT
    nd = 2 * P_BLK - 1
    return pl.pallas_call(
        functools.partial(_hyena_kernel, lb=lb), name="hyena",
        grid=(nct, b),
        in_specs=[pl.BlockSpec((1, seq, CT), lambda c, bi: (bi, 0, a_blk0 + c)),
                  pl.BlockSpec((1, seq, CT), lambda c, bi: (bi, 0, g_blk0 + c)),
                  pl.BlockSpec((nd, 2 * lb, CT), lambda c, bi: (0, 0, g_blk0_spec + c),
                               pipeline_mode=pl.Buffered(1)),
                  pl.BlockSpec((1, CT), lambda c, bi: (0, c)),
                  _full(fmat.shape), _full(fimat.shape)],
        out_specs=pl.BlockSpec((1, seq, CT), lambda c, bi: (bi, 0, c)),
        out_shape=jax.ShapeDtypeStruct((b, seq, HY_C), F32),
        scratch_shapes=[pltpu.VMEM((2 * lb, P_BLK * CT), F32), pltpu.VMEM((2 * lb, P_BLK * CT), BF16)],
        compiler_params=_cparams(("parallel", "parallel")),
    )(a_arr, g_arr, gspec, d, fmat, fimat)


def _group_norm_tile(xs, g):
    lane = lax.broadcasted_iota(jnp.int32, xs.shape, 1)
    lo = lane < GROUP
    sq = xs * xs
    s_lo = jnp.sum(jnp.where(lo, sq, 0.0), axis=-1, keepdims=True)
    s_hi = jnp.sum(jnp.where(lo, 0.0, sq), axis=-1, keepdims=True)
    r = jnp.where(lo, lax.rsqrt(s_lo / GROUP + EPS), lax.rsqrt(s_hi / GROUP + EPS))
    return xs * r * g


def _ffn2_kernel(x_ref, a_ref, h_ref, hg_ref, wo_ref, g_ref, wg_ref, wu_ref, wd_ref, gf_ref, o_ref):
    tiles = []
    n_a = a_ref.shape[1] // LANES
    for s in range(D_MODEL // LANES):
        src = a_ref[:, s * LANES:(s + 1) * LANES] if s < n_a else \
            h_ref[:, (s - n_a) * LANES:(s - n_a + 1) * LANES]
        tiles.append(_group_norm_tile(src, hg_ref[:, s * LANES:(s + 1) * LANES]).astype(BF16))
    mix = jnp.concatenate(tiles, axis=1)
    x = x_ref[...] + jnp.dot(mix, wo_ref[...], preferred_element_type=F32)
    y = _swiglu_residual(x, g_ref, wg_ref, wu_ref, wd_ref)
    o_ref[...] = _rms(y, gf_ref[...])


def _ffn2(x1, a2d, h2d, hg, wo, g, wg, wu, wd, gf):
    t = x1.shape[0]
    row = lambda w: pl.BlockSpec((TM_FFN, w), lambda i: (i, 0))
    return pl.pallas_call(
        _ffn2_kernel, name="ffn2",
        grid=(t // TM_FFN,),
        in_specs=[row(D_MODEL), row(a2d.shape[1]), row(h2d.shape[1]), _full(hg.shape), _full(wo.shape),
                  _full(g.shape), _full(wg.shape), _full(wu.shape), _full(wd.shape), _full(gf.shape)],
        out_specs=row(D_MODEL),
        out_shape=jax.ShapeDtypeStruct(x1.shape, F32),
        compiler_params=_cparams(("parallel",)),
    )(x1, a2d, h2d, hg, wo, g, wg, wu, wd, gf)


def _rope_tables(seq):
    inv = 1.0 / (ROPE_THETA ** (jnp.arange(0, ROPE, 2, dtype=F32) / ROPE))
    ang = jnp.arange(seq, dtype=F32)[:, None] * inv[None, :]
    cos, sin = jnp.cos(ang), jnp.sin(ang)
    z64 = jnp.zeros((seq, NOPE), F32)
    z32 = jnp.zeros((seq, HEAD_PAD - NOPE - ROPE), F32)
    cq_t = jnp.concatenate([jnp.ones((seq, NOPE), F32), cos, cos, z32], axis=1)
    ck_t = jnp.concatenate([z64, cos, cos, z32], axis=1)
    sn_t = jnp.concatenate([z64, -sin, sin, z32], axis=1)
    return cq_t, ck_t, sn_t


def _dft_matrices(lb):
    n2 = 2 * lb
    f = jnp.arange(lb, dtype=jnp.int32)[:, None]
    n = jnp.arange(lb, dtype=jnp.int32)[None, :]
    ang = (2.0 * math.pi / n2) * ((f * n) % n2).astype(F32)
    cos, sin = jnp.cos(ang), jnp.sin(ang)
    alt = (1 - 2 * (jnp.arange(lb, dtype=jnp.int32) % 2)).astype(F32)
    first = (f == 0)
    fwd = jnp.concatenate([cos, jnp.where(first, alt[None, :], -sin)], axis=0)
    inv_re = jnp.where(first, 1.0 / n2, (2.0 / n2) * cos)
    inv_im = jnp.where(first, alt[None, :] / n2, (-2.0 / n2) * sin)
    inv = jnp.concatenate([inv_re, inv_im], axis=0).T
    return fwd.astype(BF16), inv.astype(BF16)


def kernel(x, ffn1_norm_g, ffn1_w_gate, ffn1_w_up, ffn1_w_down, mix_norm_g, w_in, q_norm_g, w_uq, kv_norm_g, w_ukv, hyena_conv_w, hyena_conv_b, filt_w1, filt_b1, filt_w2, filt_b2, filt_w3, filt_freq, hyena_d, head_norm_g, w_out, ffn2_norm_g, ffn2_w_gate, ffn2_w_up, ffn2_w_down, final_norm_g):
    b, seq, d = x.shape
    t = b * seq
    lb = seq // P_BLK
    l = 0
    row = lambda v: v.reshape(1, -1)

    x1 = _ffn1(x.reshape(t, d), row(ffn1_norm_g[l]), ffn1_w_gate[l].astype(BF16),
               ffn1_w_up[l].astype(BF16), ffn1_w_down[l].astype(BF16))

    wi = w_in[l]
    o_kr = Q_RANK + KV_RANK
    half = ROPE // 2
    zc = lambda n: jnp.zeros((d, n), F32)
    kr = wi[:, o_kr:o_kr + ROPE]
    win_p = jnp.concatenate([
        wi[:, :o_kr],
        zc(NOPE), kr, zc(HEAD_PAD - NOPE - ROPE),
        zc(NOPE), kr[:, half:], kr[:, :half], zc(HEAD_PAD - NOPE - ROPE),
        wi[:, o_kr + ROPE:]], axis=1).astype(BF16)
    wq3 = w_uq[l].reshape(Q_RANK, MLA_HEADS, NOPE + ROPE)
    zq = lambda n: jnp.zeros((Q_RANK, MLA_HEADS, n), F32)
    wq_p = jnp.concatenate([wq3, zq(HEAD_PAD - NOPE - ROPE)], axis=2).reshape(Q_RANK, -1).astype(BF16)
    wqs_p = jnp.concatenate([zq(NOPE), wq3[:, :, NOPE + half:], wq3[:, :, NOPE:NOPE + half],
                             zq(HEAD_PAD - NOPE - ROPE)], axis=2).reshape(Q_RANK, -1).astype(BF16)
    wkv3 = w_ukv[l].reshape(KV_RANK, MLA_HEADS, NOPE + V_DIM)
    wk_p = jnp.concatenate([wkv3[:, :, :NOPE], jnp.zeros((KV_RANK, MLA_HEADS, HEAD_PAD - NOPE), F32)],
                           axis=2).reshape(KV_RANK, -1).astype(BF16)
    wv_p = wkv3[:, :, NOPE:].reshape(KV_RANK, -1).astype(BF16)
    cq_t, ck_t, sn_t = _rope_tables(seq)
    qt, k4, vt, u3 = _inproj(x1.reshape(b, seq, d), row(mix_norm_g[l]), win_p,
                             hyena_conv_w[l], row(hyena_conv_b[l]), row(q_norm_g[l]),
                             wq_p.T, wqs_p.T, row(kv_norm_g[l]), wk_p, wv_p.T,
                             cq_t.T, sn_t.T, ck_t, sn_t)

    a = _attention(qt, k4, vt)

    col = lambda v: v.astype(F32).reshape(-1, 1)
    bands = col(jnp.linspace(1e-4, FILTER_BANDS - 1, FILTER_BANDS, dtype=F32))
    w1 = filt_w1[l].astype(F32)
    w3d = filt_w3[l].astype(F32).reshape(FILTER_HIDDEN, HY_ORDER, 2, HY_C).transpose(2, 1, 3, 0)
    w3d = w3d.reshape(2, HY_ORDER * HY_C, FILTER_HIDDEN)
    deltas = jnp.abs(jnp.linspace(MIN_DECAY, MAX_DECAY, HY_C, dtype=F32))
    dl = col(jnp.tile(deltas, HY_ORDER))
    kk, asum = _filters(bands, col(w1[0]), w1[1:1 + FILTER_BANDS].T, w1[1 + FILTER_BANDS:].T,
                        col(filt_b1[l]), filt_w2[l].astype(F32).T, col(filt_b2[l]),
                        w3d, col(filt_freq[l]), dl, seq)
    fmat, fimat = _dft_matrices(lb)
    lag0 = jnp.zeros((BF16_ROWS, lb), BF16).at[0, 0].set(1.0)
    gspec = _spectra(kk, asum, jnp.concatenate([fmat, lag0], axis=0), lb)

    nct = HY_C // CT
    dsk = hyena_d[l].astype(F32)
    z1 = _hyena_stage(u3, 0, u3, nct, gspec, 0, dsk[0:1], fmat, fimat, lb)
    hy = _hyena_stage(z1, 0, u3, 2 * nct, gspec, nct, dsk[1:2], fmat, fimat, lb)

    out = _ffn2(x1, a.reshape(t, -1), hy.reshape(t, -1), row(head_norm_g[l]), w_out[l].astype(BF16),
                row(ffn2_norm_g[l]), ffn2_w_gate[l].astype(BF16), ffn2_w_up[l].astype(BF16),
                ffn2_w_down[l].astype(BF16), row(final_norm_g))
    return out.reshape(b, seq, d)
```

```python
import functools
import math

import jax
import jax.numpy as jnp
from jax import lax
from jax.experimental import pallas as pl
from jax.experimental.pallas import tpu as pltpu

F32 = jnp.float32
BF16 = jnp.bfloat16

D_MODEL = 1024
MLA_HEADS = 8
NOPE = 64
ROPE = 32
V_DIM = 64
Q_RANK = 256
KV_RANK = 128
ROPE_THETA = 10000.0
HY_C = 512
FILTER_BANDS = 16
FILTER_HIDDEN = 64
HY_ORDER = 2
FFN_HIDDEN = 2816
FFN_RES = 0.5
EPS = 1e-6
GROUP = 64
DECAY_TARGET = 1e-2
MAX_DECAY = math.log(DECAY_TARGET) / 0.3
MIN_DECAY = math.log(DECAY_TARGET) / 1.5

LANES = 128
HALO = 8
BF16_ROWS = 16
HEAD_PAD = 128
VMEM_LIMIT = 56 * 1024 * 1024

TM_FFN = 512
MXU_DIM = 256
FFN_CHUNK = 6 * MXU_DIM
TS_PROJ = 512
TQ = 512
TK = 1024
ATT_SLOTS = 2
REF_ROWS = 16
ATT_SAFE_EXP = 60.0
CT = 128
P_BLK = 4
RC = 32
FWD_GROUPS = 8
TL_FILT = 512


def _cparams(sem, vmem_limit=VMEM_LIMIT, flags=None):
    return pltpu.CompilerParams(dimension_semantics=sem, vmem_limit_bytes=vmem_limit, flags=flags)


def _rms(x, g):
    ms = jnp.mean(x * x, axis=-1, keepdims=True)
    return x * lax.rsqrt(ms + EPS) * g


def _swiglu_residual(x, g_ref, wg_ref, wu_ref, wd_ref):
    xn = _rms(x, g_ref[...]).astype(BF16)
    acc = jnp.zeros(x.shape, F32)
    for c0 in range(0, FFN_HIDDEN, FFN_CHUNK):
        c1 = min(c0 + FFN_CHUNK, FFN_HIDDEN)
        gate = jnp.dot(xn, wg_ref[:, c0:c1], preferred_element_type=F32)
        up = jnp.dot(xn, wu_ref[:, c0:c1], preferred_element_type=F32)
        h = (gate * jax.nn.sigmoid(gate) * up).astype(BF16)
        acc = acc + jnp.dot(h, wd_ref[c0:c1, :], preferred_element_type=F32)
    return x + FFN_RES * acc


def _ffn1_kernel(x_ref, g_ref, wg_ref, wu_ref, wd_ref, o_ref):
    o_ref[...] = _swiglu_residual(x_ref[...], g_ref, wg_ref, wu_ref, wd_ref)


def _full(shape):
    return pl.BlockSpec(shape, lambda *_: (0,) * len(shape))


def _ffn1(x2d, g, wg, wu, wd):
    t = x2d.shape[0]
    row = pl.BlockSpec((TM_FFN, D_MODEL), lambda i: (i, 0))
    return pl.pallas_call(
        _ffn1_kernel, name="ffn1",
        grid=(t // TM_FFN,),
        in_specs=[row, _full(g.shape), _full(wg.shape), _full(wu.shape), _full(wd.shape)],
        out_specs=row,
        out_shape=jax.ShapeDtypeStruct(x2d.shape, F32),
        compiler_params=_cparams(("parallel",)),
    )(x2d, g, wg, wu, wd)


def _inproj_kernel(x_ref, xp_ref, xn_ref, g_ref, win_ref, cw_ref, cb_ref, qg_ref, wqt_ref, wqst_ref,
                   kvg_ref, wk_ref, wvt_ref, cqt_ref, snt_ref, ck_ref, sn_ref,
                   q_ref, k_ref, v_ref, u_ref, *, scale):
    ts = x_ref.shape[1]
    i = pl.program_id(1)
    x_all = jnp.concatenate([xp_ref[0], x_ref[0], xn_ref[0]], axis=0)
    xn = _rms(x_all, g_ref[...]).astype(BF16)
    u_all = jnp.dot(xn, win_ref[...], preferred_element_type=F32)
    u = u_all[HALO:HALO + ts]
    c_q = u[:, 0:Q_RANK]
    c_kv = u[:, Q_RANK:Q_RANK + KV_RANK]
    o_pe = Q_RANK + KV_RANK
    kpe_a = u[:, o_pe:o_pe + LANES]
    kpe_b = u[:, o_pe + LANES:o_pe + 2 * LANES]

    o_hy = o_pe + 2 * LANES
    keep_prev = (i > 0).astype(F32)
    keep_next = (i < pl.num_programs(1) - 1).astype(F32)
    rows = ts + 2 * HALO
    uh = jnp.concatenate([u_all[0:HALO, o_hy:] * keep_prev, u_all[HALO:HALO + ts, o_hy:],
                          u_all[HALO + ts:, o_hy:] * keep_next], axis=0)
    prev = pltpu.roll(uh * cw_ref[0:1, :], 1, axis=0)
    nxt = pltpu.roll(uh * cw_ref[2:3, :], rows - 1, axis=0)
    conv = (cb_ref[...] + prev) + uh * cw_ref[1:2, :] + nxt
    u_ref[0] = conv[HALO:HALO + ts]

    nt = (((1,), (1,)), ((), ()))
    cqn = _rms(c_q, qg_ref[...]).astype(BF16)
    qt = lax.dot_general(wqt_ref[...], cqn, nt, preferred_element_type=F32)
    qst = lax.dot_general(wqst_ref[...], cqn, nt, preferred_element_type=F32)
    cq_t = cqt_ref[...]
    sn_tt = snt_ref[...]
    for h in range(MLA_HEADS):
        sl = slice(h * HEAD_PAD, (h + 1) * HEAD_PAD)
        q_ref[0, sl, :] = ((qt[sl] * cq_t + qst[sl] * sn_tt) * scale).astype(BF16)

    ckn = _rms(c_kv, kvg_ref[...]).astype(BF16)
    kn = jnp.dot(ckn, wk_ref[...], preferred_element_type=F32)
    k_pe = kpe_a * ck_ref[...] + kpe_b * sn_ref[...]
    for h in range(MLA_HEADS):
        sl = slice(h * HEAD_PAD, (h + 1) * HEAD_PAD)
        k_ref[0, h] = (kn[:, sl] + k_pe).astype(BF16)
    vt = lax.dot_general(wvt_ref[...], ckn, nt, preferred_element_type=F32)
    v_ref[0] = vt.reshape(MLA_HEADS // 2, 2 * V_DIM, vt.shape[1]).astype(BF16)


def _inproj(x1, g, win, cw, cb, qg, wqt, wqst, kvg, wk, wvt, cq_tt, sn_tt, ck_t, sn_t):
    b, seq, _ = x1.shape
    ts = TS_PROJ
    scale = (NOPE + ROPE) ** -0.5 * math.log2(math.e)
    hp = MLA_HEADS * HEAD_PAD
    hpt = ts // HALO
    last = seq // HALO - 1
    return pl.pallas_call(
        functools.partial(_inproj_kernel, scale=scale), name="inproj",
        grid=(b, seq // ts),
        in_specs=[pl.BlockSpec((1, ts, D_MODEL), lambda bi, i: (bi, i, 0)),
                  pl.BlockSpec((1, HALO, D_MODEL), lambda bi, i: (bi, jnp.maximum(i * hpt - 1, 0), 0)),
                  pl.BlockSpec((1, HALO, D_MODEL), lambda bi, i: (bi, jnp.minimum((i + 1) * hpt, last), 0)),
                  _full(g.shape), _full(win.shape), _full(cw.shape), _full(cb.shape),
                  _full(qg.shape), _full(wqt.shape),
                  _full(wqst.shape), _full(kvg.shape), _full(wk.shape), _full(wvt.shape),
                  pl.BlockSpec((HEAD_PAD, ts), lambda bi, i: (0, i)),
                  pl.BlockSpec((HEAD_PAD, ts), lambda bi, i: (0, i)),
                  pl.BlockSpec((ts, LANES), lambda bi, i: (i, 0)),
                  pl.BlockSpec((ts, LANES), lambda bi, i: (i, 0))],
        out_specs=[pl.BlockSpec((1, hp, ts), lambda bi, i: (bi, 0, i)),
                   pl.BlockSpec((1, MLA_HEADS, ts, HEAD_PAD), lambda bi, i: (bi, 0, i, 0)),
                   pl.BlockSpec((1, MLA_HEADS // 2, 2 * V_DIM, ts), lambda bi, i: (bi, 0, 0, i)),
                   pl.BlockSpec((1, ts, 3 * HY_C), lambda bi, i: (bi, i, 0))],
        out_shape=[jax.ShapeDtypeStruct((b, hp, seq), BF16),
                   jax.ShapeDtypeStruct((b, MLA_HEADS, seq, HEAD_PAD), BF16),
                   jax.ShapeDtypeStruct((b, MLA_HEADS // 2, 2 * V_DIM, seq), BF16),
                   jax.ShapeDtypeStruct((b, seq, 3 * HY_C), F32)],
        compiler_params=_cparams(("parallel", "parallel")),
    )(x1, x1, x1, g, win, cw, cb, qg, wqt, wqst, kvg, wk, wvt, cq_tt, sn_tt, ck_t, sn_t)


def _attn_robust(q_ref, k_ref, v_ref, o_ref, s_buf, p_buf, mc_buf, al_buf, m_sc, l_sc, acc_sc):
    seq = k_ref.shape[2]
    nk = seq // TK
    nblk = 2 * nk

    def split(n):
        if isinstance(n, int):
            return n // nk, (n % nk) * TK, (n // nk) * HEAD_PAD
        h = (n >= nk).astype(jnp.int32)
        return h, pl.multiple_of((n - h * nk) * TK, TK), pl.multiple_of(h * HEAD_PAD, HEAD_PAD)

    def scores(n, slot):
        h, off, qoff = split(n)
        k = k_ref[0, h, pl.ds(off, TK), :]
        s = jnp.dot(k, q_ref[0, pl.ds(qoff, HEAD_PAD), :], preferred_element_type=F32)
        s_buf[slot] = s
        mc_buf[slot] = jnp.max(s, axis=0, keepdims=True)

    def softmax(n, slot):
        h, _, _ = split(n)
        m_prev = m_sc[h]
        m_new = jnp.maximum(m_prev, mc_buf[slot])
        alpha = jnp.exp2(m_prev - m_new)
        p = jnp.exp2(s_buf[slot] - m_new)
        l_sc[h] = alpha * l_sc[h] + jnp.sum(p, axis=0, keepdims=True)
        p_buf[slot] = p.astype(BF16)
        al_buf[slot] = alpha
        m_sc[h] = m_new

    def values(n, slot):
        h, off, _ = split(n)
        vt = v_ref[0, 0, :, pl.ds(off, TK)]
        pv = jnp.dot(vt, p_buf[slot], preferred_element_type=F32)
        acc_sc[h] = al_buf[slot] * acc_sc[h] + pv

    m_sc[...] = jnp.full(m_sc.shape, -jnp.inf, F32)
    l_sc[...] = jnp.zeros(l_sc.shape, F32)
    acc_sc[...] = jnp.zeros(acc_sc.shape, F32)

    scores(0, 0)
    scores(1, 1)
    softmax(0, 0)

    def body(i, carry):
        n = 2 * i
        scores(n + 2, 0)
        softmax(n + 1, 1)
        values(n, 0)
        scores(n + 3, 1)
        softmax(n + 2, 0)
        values(n + 1, 1)
        return carry

    lax.fori_loop(0, (nblk - 2) // 2, body, 0)
    softmax(nblk - 1, 1)
    values(nblk - 2, 0)
    values(nblk - 1, 1)

    row = lax.broadcasted_iota(jnp.int32, acc_sc.shape[1:], 0)
    out_t = jnp.where(row < V_DIM, acc_sc[0] / l_sc[0], acc_sc[1] / l_sc[1])
    o_ref[0] = out_t.T


def _attn_kernel(q_ref, k_ref, v_ref, o_ref, *scratch):
    seq = k_ref.shape[2]
    outs = []
    exceed = None
    for h in range(2):
        qt = q_ref[0, h * HEAD_PAD:(h + 1) * HEAD_PAD, :]
        s0 = jnp.dot(k_ref[0, h, 0:REF_ROWS, :], qt, preferred_element_type=F32)
        m_ref = jnp.max(s0, axis=0, keepdims=True)
        cmax = m_ref
        l = acc = None
        for j in range(seq // TK):
            ks = slice(j * TK, (j + 1) * TK)
            s = jnp.dot(k_ref[0, h, ks, :], qt, preferred_element_type=F32)
            cmax = jnp.maximum(cmax, jnp.max(s, axis=0, keepdims=True))
            p = jnp.exp2(s - m_ref)
            p_sum = jnp.sum(p, axis=0, keepdims=True)
            pv = jnp.dot(v_ref[0, 0, :, ks], p.astype(BF16), preferred_element_type=F32)
            l = p_sum if l is None else l + p_sum
            acc = pv if acc is None else acc + pv
        outs.append(acc / l)
        e = jnp.max(cmax - m_ref)
        exceed = e if exceed is None else jnp.maximum(exceed, e)
    row = lax.broadcasted_iota(jnp.int32, outs[0].shape, 0)
    o_ref[0] = jnp.where(row < V_DIM, outs[0], outs[1]).T

    @pl.when(exceed > ATT_SAFE_EXP)
    def _():
        _attn_robust(q_ref, k_ref, v_ref, o_ref, *scratch)


def _attention(qt, k4, vt):
    b, _, seq = qt.shape
    pairs = MLA_HEADS // 2
    return pl.pallas_call(
        _attn_kernel, name="attn",
        grid=(b, pairs, seq // TQ),
        in_specs=[pl.BlockSpec((1, 2 * HEAD_PAD, TQ), lambda bi, j, qi: (bi, j, qi)),
                  pl.BlockSpec((1, 2, seq, HEAD_PAD), lambda bi, j, qi: (bi, j, 0, 0)),
                  pl.BlockSpec((1, 1, 2 * V_DIM, seq), lambda bi, j, qi: (bi, j, 0, 0))],
        out_specs=pl.BlockSpec((1, TQ, 2 * V_DIM), lambda bi, j, qi: (bi, qi, j)),
        out_shape=jax.ShapeDtypeStruct((b, seq, MLA_HEADS * V_DIM), F32),
        scratch_shapes=[pltpu.VMEM((ATT_SLOTS, TK, TQ), F32), pltpu.VMEM((ATT_SLOTS, TK, TQ), BF16),
                        pltpu.VMEM((ATT_SLOTS, 1, TQ), F32), pltpu.VMEM((ATT_SLOTS, 1, TQ), F32),
                        pltpu.VMEM((2, 1, TQ), F32), pltpu.VMEM((2, 1, TQ), F32),
                        pltpu.VMEM((2, 2 * V_DIM, TQ), F32)],
        compiler_params=_cparams(("parallel", "parallel", "parallel")),
    )(qt, k4, vt)


def _filter_kernel(bands_ref, w1t_ref, w1c_ref, w1s_ref, b1_ref, w2_ref, b2_ref, w3_ref,
                   fr_ref, dl_ref, kk_ref, asum_ref, *, seq):
    i = pl.program_id(0)
    hp = lax.Precision.HIGHEST
    l_idx = i * TL_FILT + lax.broadcasted_iota(jnp.int32, (1, TL_FILT), 1)
    pos_i = jnp.where(l_idx < seq, seq - l_idx, l_idx - seq)
    pos = pos_i.astype(F32)
    t = pos / float(max(seq - 1, 1))
    ang = (2.0 * math.pi * pos / seq) * bands_ref[...]
    pre = (t * w1t_ref[...]
           + jnp.dot(w1c_ref[...], jnp.cos(ang), precision=hp, preferred_element_type=F32)
           + jnp.dot(w1s_ref[...], -jnp.sin(ang), precision=hp, preferred_element_type=F32))
    fr = fr_ref[...]
    h = jnp.sin(fr * (pre + b1_ref[...]))
    h = jnp.sin(fr * (jnp.dot(w2_ref[...], h, precision=hp, preferred_element_type=F32) + b2_ref[...]))
    h = jnp.dot(w3_ref[0], h, precision=hp, preferred_element_type=F32)
    decay = jnp.exp(-t * dl_ref[...])
    kk = jnp.where(l_idx == 0, 0.0, h * decay)
    kk_ref[...] = kk

    @pl.when(i == 0)
    def _():
        asum_ref[...] = jnp.zeros(asum_ref.shape, F32)

    asum_ref[...] += jnp.sum(jnp.abs(kk), axis=1, keepdims=True)


def _filters(bands, w1t, w1c, w1s, b1, w2, b2, w3d, fr, dl, seq):
    nl = 2 * seq // TL_FILT
    half = seq // TL_FILT
    ncol = HY_ORDER * HY_C
    return pl.pallas_call(
        functools.partial(_filter_kernel, seq=seq), name="hyfilt",
        grid=(nl,),
        in_specs=[_full(bands.shape), _full(w1t.shape), _full(w1c.shape), _full(w1s.shape),
                  _full(b1.shape), _full(w2.shape), _full(b2.shape),
                  pl.BlockSpec((1, ncol, FILTER_HIDDEN), lambda i: (jnp.where(i < half, 1, 0), 0, 0)),
                  _full(fr.shape), _full(dl.shape)],
        out_specs=[pl.BlockSpec((ncol, TL_FILT), lambda i: (0, i)),
                   pl.BlockSpec((ncol, 1), lambda i: (0, 0))],
        out_shape=[jax.ShapeDtypeStruct((ncol, 2 * seq), F32),
                   jax.ShapeDtypeStruct((ncol, 1), F32)],
        compiler_params=_cparams(("arbitrary",)),
    )(bands, w1t, w1c, w1s, b1, w2, b2, w3d, fr, dl)


def _spec_kernel(kk_ref, asum_ref, f_ref, g_ref, *, lb):
    ct = kk_ref.shape[0]
    nhalf = kk_ref.shape[1] // lb
    kkn = (kk_ref[...] / asum_ref[...]).astype(BF16)
    x8 = jnp.concatenate([kkn[:, m * lb:(m + 1) * lb] for m in range(nhalf)], axis=0)
    hh = lax.dot_general(f_ref[...], x8, (((1,), (1,)), ((), ())),
                         preferred_element_type=F32)
    row = lax.broadcasted_iota(jnp.int32, (2 * lb, 1), 0)
    sgn = (1 - 2 * (row & 1)).astype(F32)
    low = (((row // RC) & 1) == 0) | (row == RC)
    for dd in range(nhalf - 1):
        h_neg = hh[0:2 * lb, dd * ct:(dd + 1) * ct]
        h_pos = hh[0:2 * lb, (dd + 1) * ct:(dd + 2) * ct]
        h0 = hh[2 * lb:2 * lb + 1, dd * ct:(dd + 1) * ct]
        g_ref[dd] = h_pos + sgn * (h_neg - jnp.where(low, h0, 0.0))


def _spectra(kk, asum, fmat_ext, lb):
    ncol = kk.shape[0]
    nd = 2 * P_BLK - 1
    return pl.pallas_call(
        functools.partial(_spec_kernel, lb=lb), name="hyspec",
        grid=(ncol // CT,),
        in_specs=[pl.BlockSpec((CT, kk.shape[1]), lambda c: (c, 0)),
                  pl.BlockSpec((CT, 1), lambda c: (c, 0)),
                  _full(fmat_ext.shape)],
        out_specs=pl.BlockSpec((nd, 2 * lb, CT), lambda c: (0, 0, c)),
        out_shape=jax.ShapeDtypeStruct((nd, 2 * lb, ncol), F32),
        compiler_params=_cparams(("parallel",)),
    )(kk, asum, fmat_ext)


def _hyena_kernel(a_ref, gt_ref, g_ref, d_ref, f_ref, fi_ref, o_ref, *, lb):
    ct = a_ref.shape[2]
    x4 = jnp.concatenate([a_ref[0, j * lb:(j + 1) * lb, :].astype(BF16) for j in range(P_BLK)],
                         axis=1)
    grp = 2 * lb // FWD_GROUPS
    us = [jnp.dot(f_ref[q * grp:(q + 1) * grp, :], x4, preferred_element_type=F32)
          for q in range(FWD_GROUPS)]
    first = lax.broadcasted_iota(jnp.int32, (RC, 1), 0) == 0
    y_rows = []
    for r in range(lb // RC):
        re0, im0 = 2 * RC * r, 2 * RC * r + RC
        u = us[re0 // grp]
        ure = [u[re0 % grp:re0 % grp + RC, j * ct:(j + 1) * ct] for j in range(P_BLK)]
        uim = [u[im0 % grp:im0 % grp + RC, j * ct:(j + 1) * ct] for j in range(P_BLK)]
        y_re, y_im = [], []
        for i in range(P_BLK):
            yre = yim = None
            dc = ny = None
            for j in range(P_BLK):
                dd = i - j + P_BLK - 1
                gre = g_ref[dd, re0:re0 + RC, :]
                gim = g_ref[dd, im0:im0 + RC, :]
                rr = ure[j] * gre
                ii = uim[j] * gim
                tre = rr - ii
                tim = ure[j] * gim + uim[j] * gre
                yre = tre if yre is None else yre + tre
                yim = tim if yim is None else yim + tim
                if r == 0:
                    dc = rr if dc is None else dc + rr
                    ny = ii if ny is None else ny + ii
            if r == 0:
                yre = jnp.where(first, dc, yre)
                yim = jnp.where(first, ny, yim)
            y_re.append(yre.astype(BF16))
            y_im.append(yim.astype(BF16))
        y_rows.append(jnp.concatenate(y_re, axis=1))
        y_rows.append(jnp.concatenate(y_im, axis=1))
    y = jnp.concatenate(y_rows, axis=0)
    y4 = jnp.dot(fi_ref[...], y, preferred_element_type=F32)
    for i in range(P_BLK):
        rs = slice(i * lb, (i + 1) * lb)
        o_ref[0, rs, :] = (y4[:, i * ct:(i + 1) * ct] + a_ref[0, rs, :] * d_ref[...]) * gt_ref[0, rs, :]


def _hyena_stage(a_arr, a_blk0, g_arr, g_blk0, gspec, g_blk0_spec, d, fmat, fimat, lb):
    b, seq, _ = a_arr.shape
    nct = HY_C // CT
    nd = 2 * P_BLK - 1
    return pl.pallas_call(
        functools.partial(_hyena_kernel, lb=lb), name="hyena",
        grid=(nct, b),
        in_specs=[pl.BlockSpec((1, seq, CT), lambda c, bi: (bi, 0, a_blk0 + c)),
                  pl.BlockSpec((1, seq, CT), lambda c, bi: (bi, 0, g_blk0 + c)),
                  pl.BlockSpec((nd, 2 * lb, CT), lambda c, bi: (0, 0, g_blk0_spec + c),
                               pipeline_mode=pl.Buffered(1)),
                  pl.BlockSpec((1, CT), lambda c, bi: (0, c)),
                  _full(fmat.shape), _full(fimat.shape)],
        out_specs=pl.BlockSpec((1, seq, CT), lambda c, bi: (bi, 0, c)),
        out_shape=jax.ShapeDtypeStruct((b, seq, HY_C), F32),
        compiler_params=_cparams(("parallel", "parallel")),
    )(a_arr, g_arr, gspec, d, fmat, fimat)


def _group_norm_tile(xs, g):
    lane = lax.broadcasted_iota(jnp.int32, xs.shape, 1)
    lo = lane < GROUP
    sq = xs * xs
    s_lo = jnp.sum(jnp.where(lo, sq, 0.0), axis=-1, keepdims=True)
    s_hi = jnp.sum(jnp.where(lo, 0.0, sq), axis=-1, keepdims=True)
    r = jnp.where(lo, lax.rsqrt(s_lo / GROUP + EPS), lax.rsqrt(s_hi / GROUP + EPS))
    return xs * r * g


def _ffn2_kernel(x_ref, a_ref, h_ref, hg_ref, wo_ref, g_ref, wg_ref, wu_ref, wd_ref, gf_ref, o_ref):
    tiles = []
    n_a = a_ref.shape[1] // LANES
    for s in range(D_MODEL // LANES):
        src = a_ref[:, s * LANES:(s + 1) * LANES] if s < n_a else \
            h_ref[:, (s - n_a) * LANES:(s - n_a + 1) * LANES]
        tiles.append(_group_norm_tile(src, hg_ref[:, s * LANES:(s + 1) * LANES]).astype(BF16))
    mix = jnp.concatenate(tiles, axis=1)
    x = x_ref[...] + jnp.dot(mix, wo_ref[...], preferred_element_type=F32)
    y = _swiglu_residual(x, g_ref, wg_ref, wu_ref, wd_ref)
    o_ref[...] = _rms(y, gf_ref[...])


def _ffn2(x1, a2d, h2d, hg, wo, g, wg, wu, wd, gf):
    t = x1.shape[0]
    row = lambda w: pl.BlockSpec((TM_FFN, w), lambda i: (i, 0))
    return pl.pallas_call(
        _ffn2_kernel, name="ffn2",
        grid=(t // TM_FFN,),
        in_specs=[row(D_MODEL), row(a2d.shape[1]), row(h2d.shape[1]), _full(hg.shape), _full(wo.shape),
                  _full(g.shape), _full(wg.shape), _full(wu.shape), _full(wd.shape), _full(gf.shape)],
        out_specs=row(D_MODEL),
        out_shape=jax.ShapeDtypeStruct(x1.shape, F32),
        compiler_params=_cparams(("parallel",)),
    )(x1, a2d, h2d, hg, wo, g, wg, wu, wd, gf)


def _rope_tables(seq):
    inv = 1.0 / (ROPE_THETA ** (jnp.arange(0, ROPE, 2, dtype=F32) / ROPE))
    ang = jnp.arange(seq, dtype=F32)[:, None] * inv[None, :]
    cos, sin = jnp.cos(ang), jnp.sin(ang)
    z64 = jnp.zeros((seq, NOPE), F32)
    z32 = jnp.zeros((seq, HEAD_PAD - NOPE - ROPE), F32)
    cq_t = jnp.concatenate([jnp.ones((seq, NOPE), F32), cos, cos, z32], axis=1)
    ck_t = jnp.concatenate([z64, cos, cos, z32], axis=1)
    sn_t = jnp.concatenate([z64, -sin, sin, z32], axis=1)
    return cq_t, ck_t, sn_t


def _dft_matrices(lb):
    n2 = 2 * lb
    f = jnp.arange(lb, dtype=jnp.int32)[:, None]
    n = jnp.arange(lb, dtype=jnp.int32)[None, :]
    ang = (2.0 * math.pi / n2) * ((f * n) % n2).astype(F32)
    cos, sin = jnp.cos(ang), jnp.sin(ang)
    alt = (1 - 2 * (jnp.arange(lb, dtype=jnp.int32) % 2)).astype(F32)
    first = (f == 0)
    fwd = jnp.concatenate([cos, jnp.where(first, alt[None, :], -sin)], axis=0)
    inv_re = jnp.where(first, 1.0 / n2, (2.0 / n2) * cos)
    inv_im = jnp.where(first, alt[None, :] / n2, (-2.0 / n2) * sin)
    inv = jnp.concatenate([inv_re, inv_im], axis=0).T
    perm = jnp.arange(n2, dtype=jnp.int32).reshape(2, lb // RC, RC).transpose(1, 0, 2).reshape(-1)
    return fwd[perm].astype(BF16), inv[:, perm].astype(BF16)


def kernel(x, ffn1_norm_g, ffn1_w_gate, ffn1_w_up, ffn1_w_down, mix_norm_g, w_in, q_norm_g, w_uq, kv_norm_g, w_ukv, hyena_conv_w, hyena_conv_b, filt_w1, filt_b1, filt_w2, filt_b2, filt_w3, filt_freq, hyena_d, head_norm_g, w_out, ffn2_norm_g, ffn2_w_gate, ffn2_w_up, ffn2_w_down, final_norm_g):
    b, seq, d = x.shape
    t = b * seq
    lb = seq // P_BLK
    l = 0
    row = lambda v: v.reshape(1, -1)

    x1 = _ffn1(x.reshape(t, d), row(ffn1_norm_g[l]), ffn1_w_gate[l].astype(BF16),
               ffn1_w_up[l].astype(BF16), ffn1_w_down[l].astype(BF16))

    wi = w_in[l]
    o_kr = Q_RANK + KV_RANK
    half = ROPE // 2
    zc = lambda n: jnp.zeros((d, n), F32)
    kr = wi[:, o_kr:o_kr + ROPE]
    win_p = jnp.concatenate([
        wi[:, :o_kr],
        zc(NOPE), kr, zc(HEAD_PAD - NOPE - ROPE),
        zc(NOPE), kr[:, half:], kr[:, :half], zc(HEAD_PAD - NOPE - ROPE),
        wi[:, o_kr + ROPE:]], axis=1).astype(BF16)
    wq3 = w_uq[l].reshape(Q_RANK, MLA_HEADS, NOPE + ROPE)
    zq = lambda n: jnp.zeros((Q_RANK, MLA_HEADS, n), F32)
    wq_p = jnp.concatenate([wq3, zq(HEAD_PAD - NOPE - ROPE)], axis=2).reshape(Q_RANK, -1).astype(BF16)
    wqs_p = jnp.concatenate([zq(NOPE), wq3[:, :, NOPE + half:], wq3[:, :, NOPE:NOPE + half],
                             zq(HEAD_PAD - NOPE - ROPE)], axis=2).reshape(Q_RANK, -1).astype(BF16)
    wkv3 = w_ukv[l].reshape(KV_RANK, MLA_HEADS, NOPE + V_DIM)
    wk_p = jnp.concatenate([wkv3[:, :, :NOPE], jnp.zeros((KV_RANK, MLA_HEADS, HEAD_PAD - NOPE), F32)],
                           axis=2).reshape(KV_RANK, -1).astype(BF16)
    wv_p = wkv3[:, :, NOPE:].reshape(KV_RANK, -1).astype(BF16)
    cq_t, ck_t, sn_t = _rope_tables(seq)
    qt, k4, vt, u3 = _inproj(x1.reshape(b, seq, d), row(mix_norm_g[l]), win_p,
                             hyena_conv_w[l], row(hyena_conv_b[l]), row(q_norm_g[l]),
                             wq_p.T, wqs_p.T, row(kv_norm_g[l]), wk_p, wv_p.T,
                             cq_t.T, sn_t.T, ck_t, sn_t)

    a = _attention(qt, k4, vt)

    col = lambda v: v.astype(F32).reshape(-1, 1)
    bands = col(jnp.linspace(1e-4, FILTER_BANDS - 1, FILTER_BANDS, dtype=F32))
    w1 = filt_w1[l].astype(F32)
    w3d = filt_w3[l].astype(F32).reshape(FILTER_HIDDEN, HY_ORDER, 2, HY_C).transpose(2, 1, 3, 0)
    w3d = w3d.reshape(2, HY_ORDER * HY_C, FILTER_HIDDEN)
    deltas = jnp.abs(jnp.linspace(MIN_DECAY, MAX_DECAY, HY_C, dtype=F32))
    dl = col(jnp.tile(deltas, HY_ORDER))
    kk, asum = _filters(bands, col(w1[0]), w1[1:1 + FILTER_BANDS].T, w1[1 + FILTER_BANDS:].T,
                        col(filt_b1[l]), filt_w2[l].astype(F32).T, col(filt_b2[l]),
                        w3d, col(filt_freq[l]), dl, seq)
    fmat, fimat = _dft_matrices(lb)
    lag0 = jnp.zeros((BF16_ROWS, lb), BF16).at[0, 0].set(1.0)
    gspec = _spectra(kk, asum, jnp.concatenate([fmat, lag0], axis=0), lb)

    nct = HY_C // CT
    dsk = hyena_d[l].astype(F32)
    z1 = _hyena_stage(u3, 0, u3, nct, gspec, 0, dsk[0:1], fmat, fimat, lb)
    hy = _hyena_stage(z1, 0, u3, 2 * nct, gspec, nct, dsk[1:2], fmat, fimat, lb)

    out = _ffn2(x1, a.reshape(t, -1), hy.reshape(t, -1), row(head_norm_g[l]), w_out[l].astype(BF16),
                row(ffn2_norm_g[l]), ffn2_w_gate[l].astype(BF16), ffn2_w_up[l].astype(BF16),
                ffn2_w_down[l].astype(BF16), row(final_norm_g))
    return out.reshape(b, seq, d)
```

```python
import functools
import math

import jax
import jax.numpy as jnp
from jax import lax
from jax.experimental import pallas as pl
from jax.experimental.pallas import tpu as pltpu

F32 = jnp.float32
BF16 = jnp.bfloat16

D_MODEL = 1024
MLA_HEADS = 8
NOPE = 64
ROPE = 32
V_DIM = 64
Q_RANK = 256
KV_RANK = 128
ROPE_THETA = 10000.0
HY_C = 512
FILTER_BANDS = 16
FILTER_HIDDEN = 64
HY_ORDER = 2
FFN_HIDDEN = 2816
FFN_RES = 0.5
EPS = 1e-6
GROUP = 64
DECAY_TARGET = 1e-2
MAX_DECAY = math.log(DECAY_TARGET) / 0.3
MIN_DECAY = math.log(DECAY_TARGET) / 1.5

LANES = 128
HALO = 8
BF16_ROWS = 16
HEAD_PAD = 128
VMEM_LIMIT = 56 * 1024 * 1024

TM_FFN = 512
MXU_DIM = 256
FFN_CHUNK = 6 * MXU_DIM
TS_PROJ = 512
TQ = 512
TK = 1024
ATT_SLOTS = 2
REF_ROWS = 16
ATT_SAFE_EXP = 60.0
F32_HUGE = 3.0e38
CT = 128
P_BLK = 4
RC = 32
FWD_GROUPS = 8
TL_FILT = 512


def _cparams(sem, vmem_limit=VMEM_LIMIT, flags=None):
    return pltpu.CompilerParams(dimension_semantics=sem, vmem_limit_bytes=vmem_limit, flags=flags)


def _rms(x, g):
    ms = jnp.mean(x * x, axis=-1, keepdims=True)
    return x * lax.rsqrt(ms + EPS) * g


def _swiglu_residual(x, g_ref, wg_ref, wu_ref, wd_ref):
    xn = _rms(x, g_ref[...]).astype(BF16)
    acc = jnp.zeros(x.shape, F32)
    for c0 in range(0, FFN_HIDDEN, FFN_CHUNK):
        c1 = min(c0 + FFN_CHUNK, FFN_HIDDEN)
        gate = jnp.dot(xn, wg_ref[:, c0:c1], preferred_element_type=F32)
        up = jnp.dot(xn, wu_ref[:, c0:c1], preferred_element_type=F32)
        h = (gate * jax.nn.sigmoid(gate) * up).astype(BF16)
        acc = acc + jnp.dot(h, wd_ref[c0:c1, :], preferred_element_type=F32)
    return x + FFN_RES * acc


def _ffn1_kernel(x_ref, g_ref, wg_ref, wu_ref, wd_ref, o_ref):
    o_ref[...] = _swiglu_residual(x_ref[...], g_ref, wg_ref, wu_ref, wd_ref)


def _full(shape):
    return pl.BlockSpec(shape, lambda *_: (0,) * len(shape))


def _ffn1(x2d, g, wg, wu, wd):
    t = x2d.shape[0]
    row = pl.BlockSpec((TM_FFN, D_MODEL), lambda i: (i, 0))
    return pl.pallas_call(
        _ffn1_kernel, name="ffn1",
        grid=(t // TM_FFN,),
        in_specs=[row, _full(g.shape), _full(wg.shape), _full(wu.shape), _full(wd.shape)],
        out_specs=row,
        out_shape=jax.ShapeDtypeStruct(x2d.shape, F32),
        compiler_params=_cparams(("parallel",)),
    )(x2d, g, wg, wu, wd)


def _inproj_kernel(x_ref, xp_ref, xn_ref, g_ref, win_ref, cw_ref, cb_ref, qg_ref, wqt_ref, wqst_ref,
                   kvg_ref, wk_ref, wvt_ref, cqt_ref, snt_ref, ck_ref, sn_ref,
                   q_ref, k_ref, v_ref, u_ref, *, scale):
    ts = x_ref.shape[1]
    i = pl.program_id(1)
    x_all = jnp.concatenate([xp_ref[0], x_ref[0], xn_ref[0]], axis=0)
    xn = _rms(x_all, g_ref[...]).astype(BF16)
    u_all = jnp.dot(xn, win_ref[...], preferred_element_type=F32)
    u = u_all[HALO:HALO + ts]
    c_q = u[:, 0:Q_RANK]
    c_kv = u[:, Q_RANK:Q_RANK + KV_RANK]
    o_pe = Q_RANK + KV_RANK
    kpe_a = u[:, o_pe:o_pe + LANES]
    kpe_b = u[:, o_pe + LANES:o_pe + 2 * LANES]

    o_hy = o_pe + 2 * LANES
    keep_prev = (i > 0).astype(F32)
    keep_next = (i < pl.num_programs(1) - 1).astype(F32)
    rows = ts + 2 * HALO
    uh = jnp.concatenate([u_all[0:HALO, o_hy:] * keep_prev, u_all[HALO:HALO + ts, o_hy:],
                          u_all[HALO + ts:, o_hy:] * keep_next], axis=0)
    prev = pltpu.roll(uh * cw_ref[0:1, :], 1, axis=0)
    nxt = pltpu.roll(uh * cw_ref[2:3, :], rows - 1, axis=0)
    conv = (cb_ref[...] + prev) + uh * cw_ref[1:2, :] + nxt
    u_ref[0] = conv[HALO:HALO + ts]

    nt = (((1,), (1,)), ((), ()))
    cqn = _rms(c_q, qg_ref[...]).astype(BF16)
    qt = lax.dot_general(wqt_ref[...], cqn, nt, preferred_element_type=F32)
    qst = lax.dot_general(wqst_ref[...], cqn, nt, preferred_element_type=F32)
    cq_t = cqt_ref[...]
    sn_tt = snt_ref[...]
    for h in range(MLA_HEADS):
        sl = slice(h * HEAD_PAD, (h + 1) * HEAD_PAD)
        q_ref[0, sl, :] = ((qt[sl] * cq_t + qst[sl] * sn_tt) * scale).astype(BF16)

    ckn = _rms(c_kv, kvg_ref[...]).astype(BF16)
    kn = jnp.dot(ckn, wk_ref[...], preferred_element_type=F32)
    k_pe = kpe_a * ck_ref[...] + kpe_b * sn_ref[...]
    for h in range(MLA_HEADS):
        sl = slice(h * HEAD_PAD, (h + 1) * HEAD_PAD)
        k_ref[0, h] = (kn[:, sl] + k_pe).astype(BF16)
    vt = lax.dot_general(wvt_ref[...], ckn, nt, preferred_element_type=F32)
    v_ref[0] = vt.reshape(MLA_HEADS // 2, 2 * V_DIM, vt.shape[1]).astype(BF16)


def _inproj(x1, g, win, cw, cb, qg, wqt, wqst, kvg, wk, wvt, cq_tt, sn_tt, ck_t, sn_t):
    b, seq, _ = x1.shape
    ts = TS_PROJ
    scale = (NOPE + ROPE) ** -0.5 * math.log2(math.e)
    hp = MLA_HEADS * HEAD_PAD
    hpt = ts // HALO
    last = seq // HALO - 1
    return pl.pallas_call(
        functools.partial(_inproj_kernel, scale=scale), name="inproj",
        grid=(b, seq // ts),
        in_specs=[pl.BlockSpec((1, ts, D_MODEL), lambda bi, i: (bi, i, 0)),
                  pl.BlockSpec((1, HALO, D_MODEL), lambda bi, i: (bi, jnp.maximum(i * hpt - 1, 0), 0)),
                  pl.BlockSpec((1, HALO, D_MODEL), lambda bi, i: (bi, jnp.minimum((i + 1) * hpt, last), 0)),
                  _full(g.shape), _full(win.shape), _full(cw.shape), _full(cb.shape),
                  _full(qg.shape), _full(wqt.shape),
                  _full(wqst.shape), _full(kvg.shape), _full(wk.shape), _full(wvt.shape),
                  pl.BlockSpec((HEAD_PAD, ts), lambda bi, i: (0, i)),
                  pl.BlockSpec((HEAD_PAD, ts), lambda bi, i: (0, i)),
                  pl.BlockSpec((ts, LANES), lambda bi, i: (i, 0)),
                  pl.BlockSpec((ts, LANES), lambda bi, i: (i, 0))],
        out_specs=[pl.BlockSpec((1, hp, ts), lambda bi, i: (bi, 0, i)),
                   pl.BlockSpec((1, MLA_HEADS, ts, HEAD_PAD), lambda bi, i: (bi, 0, i, 0)),
                   pl.BlockSpec((1, MLA_HEADS // 2, 2 * V_DIM, ts), lambda bi, i: (bi, 0, 0, i)),
                   pl.BlockSpec((1, ts, 3 * HY_C), lambda bi, i: (bi, i, 0))],
        out_shape=[jax.ShapeDtypeStruct((b, hp, seq), BF16),
                   jax.ShapeDtypeStruct((b, MLA_HEADS, seq, HEAD_PAD), BF16),
                   jax.ShapeDtypeStruct((b, MLA_HEADS // 2, 2 * V_DIM, seq), BF16),
                   jax.ShapeDtypeStruct((b, seq, 3 * HY_C), F32)],
        compiler_params=_cparams(("parallel", "parallel")),
    )(x1, x1, x1, g, win, cw, cb, qg, wqt, wqst, kvg, wk, wvt, cq_tt, sn_tt, ck_t, sn_t)


def _attn_robust(q_ref, k_ref, v_ref, o_ref, s_buf, p_buf, mc_buf, al_buf, m_sc, l_sc, acc_sc):
    seq = k_ref.shape[2]
    nk = seq // TK
    nblk = 2 * nk

    def split(n):
        if isinstance(n, int):
            return n // nk, (n % nk) * TK, (n // nk) * HEAD_PAD
        h = (n >= nk).astype(jnp.int32)
        return h, pl.multiple_of((n - h * nk) * TK, TK), pl.multiple_of(h * HEAD_PAD, HEAD_PAD)

    def scores(n, slot):
        h, off, qoff = split(n)
        k = k_ref[0, h, pl.ds(off, TK), :]
        s = jnp.dot(k, q_ref[0, pl.ds(qoff, HEAD_PAD), :], preferred_element_type=F32)
        s_buf[slot] = s
        mc_buf[slot] = jnp.max(s, axis=0, keepdims=True)

    def softmax(n, slot):
        h, _, _ = split(n)
        m_prev = m_sc[h]
        m_new = jnp.maximum(m_prev, mc_buf[slot])
        alpha = jnp.exp2(m_prev - m_new)
        p = jnp.exp2(s_buf[slot] - m_new)
        l_sc[h] = alpha * l_sc[h] + jnp.sum(p, axis=0, keepdims=True)
        p_buf[slot] = p.astype(BF16)
        al_buf[slot] = alpha
        m_sc[h] = m_new

    def values(n, slot):
        h, off, _ = split(n)
        vt = v_ref[0, 0, :, pl.ds(off, TK)]
        pv = jnp.dot(vt, p_buf[slot], preferred_element_type=F32)
        acc_sc[h] = al_buf[slot] * acc_sc[h] + pv

    m_sc[...] = jnp.full(m_sc.shape, -jnp.inf, F32)
    l_sc[...] = jnp.zeros(l_sc.shape, F32)
    acc_sc[...] = jnp.zeros(acc_sc.shape, F32)

    scores(0, 0)
    scores(1, 1)
    softmax(0, 0)

    def body(i, carry):
        n = 2 * i
        scores(n + 2, 0)
        softmax(n + 1, 1)
        values(n, 0)
        scores(n + 3, 1)
        softmax(n + 2, 0)
        values(n + 1, 1)
        return carry

    lax.fori_loop(0, (nblk - 2) // 2, body, 0)
    softmax(nblk - 1, 1)
    values(nblk - 2, 0)
    values(nblk - 1, 1)

    row = lax.broadcasted_iota(jnp.int32, acc_sc.shape[1:], 0)
    out_t = jnp.where(row < V_DIM, acc_sc[0] / l_sc[0], acc_sc[1] / l_sc[1])
    o_ref[0] = out_t.T


def _attn_kernel(q_ref, k_ref, v_ref, o_ref, *scratch):
    seq = k_ref.shape[2]
    outs = []
    bad = None
    for h in range(2):
        qt = q_ref[0, h * HEAD_PAD:(h + 1) * HEAD_PAD, :]
        s0 = jnp.dot(k_ref[0, h, 0:REF_ROWS, :], qt, preferred_element_type=F32)
        m_ref = jnp.max(s0, axis=0, keepdims=True)
        l = acc = None
        for j in range(seq // TK):
            ks = slice(j * TK, (j + 1) * TK)
            s = jnp.dot(k_ref[0, h, ks, :], qt, preferred_element_type=F32)
            p = jnp.exp2(s - m_ref)
            p_sum = jnp.sum(p, axis=0, keepdims=True)
            pv = jnp.dot(v_ref[0, 0, :, ks], p.astype(BF16), preferred_element_type=F32)
            l = p_sum if l is None else l + p_sum
            acc = pv if acc is None else acc + pv
        out = acc / l
        outs.append(out)
        b_h = jnp.maximum(jnp.max(jnp.where(l < 2.0 ** ATT_SAFE_EXP, 0.0, 1.0)),
                          jnp.max(jnp.where(jnp.abs(out) < F32_HUGE, 0.0, 1.0)))
        bad = b_h if bad is None else jnp.maximum(bad, b_h)
    row = lax.broadcasted_iota(jnp.int32, outs[0].shape, 0)
    o_ref[0] = jnp.where(row < V_DIM, outs[0], outs[1]).T

    @pl.when(bad > 0.0)
    def _():
        _attn_robust(q_ref, k_ref, v_ref, o_ref, *scratch)


def _attention(qt, k4, vt):
    b, _, seq = qt.shape
    pairs = MLA_HEADS // 2
    return pl.pallas_call(
        _attn_kernel, name="attn",
        grid=(b, pairs, seq // TQ),
        in_specs=[pl.BlockSpec((1, 2 * HEAD_PAD, TQ), lambda bi, j, qi: (bi, j, qi)),
                  pl.BlockSpec((1, 2, seq, HEAD_PAD), lambda bi, j, qi: (bi, j, 0, 0)),
                  pl.BlockSpec((1, 1, 2 * V_DIM, seq), lambda bi, j, qi: (bi, j, 0, 0))],
        out_specs=pl.BlockSpec((1, TQ, 2 * V_DIM), lambda bi, j, qi: (bi, qi, j)),
        out_shape=jax.ShapeDtypeStruct((b, seq, MLA_HEADS * V_DIM), F32),
        scratch_shapes=[pltpu.VMEM((ATT_SLOTS, TK, TQ), F32), pltpu.VMEM((ATT_SLOTS, TK, TQ), BF16),
                        pltpu.VMEM((ATT_SLOTS, 1, TQ), F32), pltpu.VMEM((ATT_SLOTS, 1, TQ), F32),
                        pltpu.VMEM((2, 1, TQ), F32), pltpu.VMEM((2, 1, TQ), F32),
                        pltpu.VMEM((2, 2 * V_DIM, TQ), F32)],
        compiler_params=_cparams(("parallel", "parallel", "parallel")),
    )(qt, k4, vt)


def _filter_kernel(bands_ref, w1t_ref, w1c_ref, w1s_ref, b1_ref, w2_ref, b2_ref, w3_ref,
                   fr_ref, dl_ref, kk_ref, asum_ref, *, seq):
    i = pl.program_id(0)
    hp = lax.Precision.HIGHEST
    l_idx = i * TL_FILT + lax.broadcasted_iota(jnp.int32, (1, TL_FILT), 1)
    pos_i = jnp.where(l_idx < seq, seq - l_idx, l_idx - seq)
    pos = pos_i.astype(F32)
    t = pos / float(max(seq - 1, 1))
    ang = (2.0 * math.pi * pos / seq) * bands_ref[...]
    pre = (t * w1t_ref[...]
           + jnp.dot(w1c_ref[...], jnp.cos(ang), precision=hp, preferred_element_type=F32)
           + jnp.dot(w1s_ref[...], -jnp.sin(ang), precision=hp, preferred_element_type=F32))
    fr = fr_ref[...]
    h = jnp.sin(fr * (pre + b1_ref[...]))
    h = jnp.sin(fr * (jnp.dot(w2_ref[...], h, precision=hp, preferred_element_type=F32) + b2_ref[...]))
    h = jnp.dot(w3_ref[0], h, precision=hp, preferred_element_type=F32)
    decay = jnp.exp(-t * dl_ref[...])
    kk = jnp.where(l_idx == 0, 0.0, h * decay)
    kk_ref[...] = kk

    @pl.when(i == 0)
    def _():
        asum_ref[...] = jnp.zeros(asum_ref.shape, F32)

    asum_ref[...] += jnp.sum(jnp.abs(kk), axis=1, keepdims=True)


def _filters(bands, w1t, w1c, w1s, b1, w2, b2, w3d, fr, dl, seq):
    nl = 2 * seq // TL_FILT
    half = seq // TL_FILT
    ncol = HY_ORDER * HY_C
    return pl.pallas_call(
        functools.partial(_filter_kernel, seq=seq), name="hyfilt",
        grid=(nl,),
        in_specs=[_full(bands.shape), _full(w1t.shape), _full(w1c.shape), _full(w1s.shape),
                  _full(b1.shape), _full(w2.shape), _full(b2.shape),
                  pl.BlockSpec((1, ncol, FILTER_HIDDEN), lambda i: (jnp.where(i < half, 1, 0), 0, 0)),
                  _full(fr.shape), _full(dl.shape)],
        out_specs=[pl.BlockSpec((ncol, TL_FILT), lambda i: (0, i)),
                   pl.BlockSpec((ncol, 1), lambda i: (0, 0))],
        out_shape=[jax.ShapeDtypeStruct((ncol, 2 * seq), F32),
                   jax.ShapeDtypeStruct((ncol, 1), F32)],
        compiler_params=_cparams(("arbitrary",)),
    )(bands, w1t, w1c, w1s, b1, w2, b2, w3d, fr, dl)


def _spec_kernel(kk_ref, asum_ref, f_ref, g_ref, *, lb):
    ct = kk_ref.shape[0]
    nhalf = kk_ref.shape[1] // lb
    kkn = (kk_ref[...] / asum_ref[...]).astype(BF16)
    x8 = jnp.concatenate([kkn[:, m * lb:(m + 1) * lb] for m in range(nhalf)], axis=0)
    hh = lax.dot_general(f_ref[...], x8, (((1,), (1,)), ((), ())),
                         preferred_element_type=F32)
    row = lax.broadcasted_iota(jnp.int32, (2 * lb, 1), 0)
    sgn = (1 - 2 * (row & 1)).astype(F32)
    low = (((row // RC) & 1) == 0) | (row == RC)
    for dd in range(nhalf - 1):
        h_neg = hh[0:2 * lb, dd * ct:(dd + 1) * ct]
        h_pos = hh[0:2 * lb, (dd + 1) * ct:(dd + 2) * ct]
        h0 = hh[2 * lb:2 * lb + 1, dd * ct:(dd + 1) * ct]
        g_ref[dd] = h_pos + sgn * (h_neg - jnp.where(low, h0, 0.0))


def _spectra(kk, asum, fmat_ext, lb):
    ncol = kk.shape[0]
    nd = 2 * P_BLK - 1
    return pl.pallas_call(
        functools.partial(_spec_kernel, lb=lb), name="hyspec",
        grid=(ncol // CT,),
        in_specs=[pl.BlockSpec((CT, kk.shape[1]), lambda c: (c, 0)),
                  pl.BlockSpec((CT, 1), lambda c: (c, 0)),
                  _full(fmat_ext.shape)],
        out_specs=pl.BlockSpec((nd, 2 * lb, CT), lambda c: (0, 0, c)),
        out_shape=jax.ShapeDtypeStruct((nd, 2 * lb, ncol), F32),
        compiler_params=_cparams(("parallel",)),
    )(kk, asum, fmat_ext)


def _hyena_kernel(a_ref, gt_ref, g_ref, d_ref, f_ref, fi_ref, o_ref, *, lb):
    ct = a_ref.shape[2]
    x4 = jnp.concatenate([a_ref[0, j * lb:(j + 1) * lb, :].astype(BF16) for j in range(P_BLK)],
                         axis=1)
    grp = 2 * lb // FWD_GROUPS
    us = [jnp.dot(f_ref[q * grp:(q + 1) * grp, :], x4, preferred_element_type=F32)
          for q in range(FWD_GROUPS)]
    first = lax.broadcasted_iota(jnp.int32, (RC, 1), 0) == 0
    y_rows = []
    for r in range(lb // RC):
        re0, im0 = 2 * RC * r, 2 * RC * r + RC
        u = us[re0 // grp]
        ure = [u[re0 % grp:re0 % grp + RC, j * ct:(j + 1) * ct] for j in range(P_BLK)]
        uim = [u[im0 % grp:im0 % grp + RC, j * ct:(j + 1) * ct] for j in range(P_BLK)]
        y_re, y_im = [], []
        for i in range(P_BLK):
            yre = yim = None
            dc = ny = None
            for j in range(P_BLK):
                dd = i - j + P_BLK - 1
                gre = g_ref[dd, re0:re0 + RC, :]
                gim = g_ref[dd, im0:im0 + RC, :]
                rr = ure[j] * gre
                ii = uim[j] * gim
                tre = rr - ii
                tim = ure[j] * gim + uim[j] * gre
                yre = tre if yre is None else yre + tre
                yim = tim if yim is None else yim + tim
                if r == 0:
                    dc = rr if dc is None else dc + rr
                    ny = ii if ny is None else ny + ii
            if r == 0:
                yre = jnp.where(first, dc, yre)
                yim = jnp.where(first, ny, yim)
            y_re.append(yre.astype(BF16))
            y_im.append(yim.astype(BF16))
        y_rows.append(jnp.concatenate(y_re, axis=1))
        y_rows.append(jnp.concatenate(y_im, axis=1))
    y = jnp.concatenate(y_rows, axis=0)
    y4 = jnp.dot(fi_ref[...], y, preferred_element_type=F32)
    for i in range(P_BLK):
        rs = slice(i * lb, (i + 1) * lb)
        o_ref[0, rs, :] = (y4[:, i * ct:(i + 1) * ct] + a_ref[0, rs, :] * d_ref[...]) * gt_ref[0, rs, :]


def _hyena_stage(a_arr, a_blk0, g_arr, g_blk0, gspec, g_blk0_spec, d, fmat, fimat, lb):
    b, seq, _ = a_arr.shape
    nct = HY_C // CT
    nd = 2 * P_BLK - 1
    return pl.pallas_call(
        functools.partial(_hyena_kernel, lb=lb), name="hyena",
        grid=(nct, b),
        in_specs=[pl.BlockSpec((1, seq, CT), lambda c, bi: (bi, 0, a_blk0 + c)),
                  pl.BlockSpec((1, seq, CT), lambda c, bi: (bi, 0, g_blk0 + c)),
                  pl.BlockSpec((nd, 2 * lb, CT), lambda c, bi: (0, 0, g_blk0_spec + c),
                               pipeline_mode=pl.Buffered(1)),
                  pl.BlockSpec((1, CT), lambda c, bi: (0, c)),
                  _full(fmat.shape), _full(fimat.shape)],
        out_specs=pl.BlockSpec((1, seq, CT), lambda c, bi: (bi, 0, c)),
        out_shape=jax.ShapeDtypeStruct((b, seq, HY_C), F32),
        compiler_params=_cparams(("parallel", "parallel")),
    )(a_arr, g_arr, gspec, d, fmat, fimat)


def _group_norm_tile(xs, g):
    lane = lax.broadcasted_iota(jnp.int32, xs.shape, 1)
    lo = lane < GROUP
    sq = xs * xs
    s_lo = jnp.sum(jnp.where(lo, sq, 0.0), axis=-1, keepdims=True)
    s_hi = jnp.sum(jnp.where(lo, 0.0, sq), axis=-1, keepdims=True)
    r = jnp.where(lo, lax.rsqrt(s_lo / GROUP + EPS), lax.rsqrt(s_hi / GROUP + EPS))
    return xs * r * g


def _ffn2_kernel(x_ref, a_ref, h_ref, hg_ref, wo_ref, g_ref, wg_ref, wu_ref, wd_ref, gf_ref, o_ref):
    tiles = []
    n_a = a_ref.shape[1] // LANES
    for s in range(D_MODEL // LANES):
        src = a_ref[:, s * LANES:(s + 1) * LANES] if s < n_a else \
            h_ref[:, (s - n_a) * LANES:(s - n_a + 1) * LANES]
        tiles.append(_group_norm_tile(src, hg_ref[:, s * LANES:(s + 1) * LANES]).astype(BF16))
    mix = jnp.concatenate(tiles, axis=1)
    x = x_ref[...] + jnp.dot(mix, wo_ref[...], preferred_element_type=F32)
    y = _swiglu_residual(x, g_ref, wg_ref, wu_ref, wd_ref)
    o_ref[...] = _rms(y, gf_ref[...])


def _ffn2(x1, a2d, h2d, hg, wo, g, wg, wu, wd, gf):
    t = x1.shape[0]
    row = lambda w: pl.BlockSpec((TM_FFN, w), lambda i: (i, 0))
    return pl.pallas_call(
        _ffn2_kernel, name="ffn2",
        grid=(t // TM_FFN,),
        in_specs=[row(D_MODEL), row(a2d.shape[1]), row(h2d.shape[1]), _full(hg.shape), _full(wo.shape),
                  _full(g.shape), _full(wg.shape), _full(wu.shape), _full(wd.shape), _full(gf.shape)],
        out_specs=row(D_MODEL),
        out_shape=jax.ShapeDtypeStruct(x1.shape, F32),
        compiler_params=_cparams(("parallel",)),
    )(x1, a2d, h2d, hg, wo, g, wg, wu, wd, gf)


def _rope_tables(seq):
    inv = 1.0 / (ROPE_THETA ** (jnp.arange(0, ROPE, 2, dtype=F32) / ROPE))
    ang = jnp.arange(seq, dtype=F32)[:, None] * inv[None, :]
    cos, sin = jnp.cos(ang), jnp.sin(ang)
    z64 = jnp.zeros((seq, NOPE), F32)
    z32 = jnp.zeros((seq, HEAD_PAD - NOPE - ROPE), F32)
    cq_t = jnp.concatenate([jnp.ones((seq, NOPE), F32), cos, cos, z32], axis=1)
    ck_t = jnp.concatenate([z64, cos, cos, z32], axis=1)
    sn_t = jnp.concatenate([z64, -sin, sin, z32], axis=1)
    return cq_t, ck_t, sn_t


def _dft_matrices(lb):
    n2 = 2 * lb
    nch = lb // RC
    two_pi = 2.0 * math.pi

    def tables(c, q, n):
        ang_c = (two_pi / (n2 // RC)) * ((c * n) % (n2 // RC)).astype(F32)
        ang_q = (two_pi / n2) * ((q * n) % n2).astype(F32)
        cc, sc, cq, sq = jnp.cos(ang_c), jnp.sin(ang_c), jnp.cos(ang_q), jnp.sin(ang_q)
        return cc * cq - sc * sq, sc * cq + cc * sq

    ar = lambda m: jnp.arange(m, dtype=jnp.int32)
    alt = lambda n: (1 - 2 * (n % 2)).astype(F32)
    c, q, n = ar(nch)[:, None, None], ar(RC)[None, :, None], ar(lb)[None, None, :]
    cos, sin = tables(c, q, n)
    f0 = (c == 0) & (q == 0)
    fwd = jnp.stack([cos, jnp.where(f0, alt(n), -sin)], axis=1).reshape(n2, lb)
    n, c, q = ar(lb)[:, None, None], ar(nch)[None, :, None], ar(RC)[None, None, :]
    cos, sin = tables(c, q, n)
    f0 = (c == 0) & (q == 0)
    inv_re = jnp.where(f0, 1.0 / n2, (2.0 / n2) * cos)
    inv_im = jnp.where(f0, alt(n) / n2, (-2.0 / n2) * sin)
    inv = jnp.stack([inv_re, inv_im], axis=2).reshape(lb, n2)
    return fwd.astype(BF16), inv.astype(BF16)


def kernel(x, ffn1_norm_g, ffn1_w_gate, ffn1_w_up, ffn1_w_down, mix_norm_g, w_in, q_norm_g, w_uq, kv_norm_g, w_ukv, hyena_conv_w, hyena_conv_b, filt_w1, filt_b1, filt_w2, filt_b2, filt_w3, filt_freq, hyena_d, head_norm_g, w_out, ffn2_norm_g, ffn2_w_gate, ffn2_w_up, ffn2_w_down, final_norm_g):
    b, seq, d = x.shape
    t = b * seq
    lb = seq // P_BLK
    l = 0
    row = lambda v: v.reshape(1, -1)

    x1 = _ffn1(x.reshape(t, d), row(ffn1_norm_g[l]), ffn1_w_gate[l].astype(BF16),
               ffn1_w_up[l].astype(BF16), ffn1_w_down[l].astype(BF16))

    wi = w_in[l]
    o_kr = Q_RANK + KV_RANK
    half = ROPE // 2
    zc = lambda n: jnp.zeros((d, n), F32)
    kr = wi[:, o_kr:o_kr + ROPE]
    win_p = jnp.concatenate([
        wi[:, :o_kr],
        zc(NOPE), kr, zc(HEAD_PAD - NOPE - ROPE),
        zc(NOPE), kr[:, half:], kr[:, :half], zc(HEAD_PAD - NOPE - ROPE),
        wi[:, o_kr + ROPE:]], axis=1).astype(BF16)
    wq3 = w_uq[l].reshape(Q_RANK, MLA_HEADS, NOPE + ROPE)
    zq = lambda n: jnp.zeros((Q_RANK, MLA_HEADS, n), F32)
    wq_p = jnp.concatenate([wq3, zq(HEAD_PAD - NOPE - ROPE)], axis=2).reshape(Q_RANK, -1).astype(BF16)
    wqs_p = jnp.concatenate([zq(NOPE), wq3[:, :, NOPE + half:], wq3[:, :, NOPE:NOPE + half],
                             zq(HEAD_PAD - NOPE - ROPE)], axis=2).reshape(Q_RANK, -1).astype(BF16)
    wkv3 = w_ukv[l].reshape(KV_RANK, MLA_HEADS, NOPE + V_DIM)
    wk_p = jnp.concatenate([wkv3[:, :, :NOPE], jnp.zeros((KV_RANK, MLA_HEADS, HEAD_PAD - NOPE), F32)],
                           axis=2).reshape(KV_RANK, -1).astype(BF16)
    wv_p = wkv3[:, :, NOPE:].reshape(KV_RANK, -1).astype(BF16)
    cq_t, ck_t, sn_t = _rope_tables(seq)
    qt, k4, vt, u3 = _inproj(x1.reshape(b, seq, d), row(mix_norm_g[l]), win_p,
                             hyena_conv_w[l], row(hyena_conv_b[l]), row(q_norm_g[l]),
                             wq_p.T, wqs_p.T, row(kv_norm_g[l]), wk_p, wv_p.T,
                             cq_t.T, sn_t.T, ck_t, sn_t)

    a = _attention(qt, k4, vt)

    col = lambda v: v.astype(F32).reshape(-1, 1)
    bands = col(jnp.linspace(1e-4, FILTER_BANDS - 1, FILTER_BANDS, dtype=F32))
    w1 = filt_w1[l].astype(F32)
    w3d = filt_w3[l].astype(F32).reshape(FILTER_HIDDEN, HY_ORDER, 2, HY_C).transpose(2, 1, 3, 0)
    w3d = w3d.reshape(2, HY_ORDER * HY_C, FILTER_HIDDEN)
    deltas = jnp.abs(jnp.linspace(MIN_DECAY, MAX_DECAY, HY_C, dtype=F32))
    dl = col(jnp.tile(deltas, HY_ORDER))
    kk, asum = _filters(bands, col(w1[0]), w1[1:1 + FILTER_BANDS].T, w1[1 + FILTER_BANDS:].T,
                        col(filt_b1[l]), filt_w2[l].astype(F32).T, col(filt_b2[l]),
                        w3d, col(filt_freq[l]), dl, seq)
    fmat, fimat = _dft_matrices(lb)
    lag0 = jnp.zeros((BF16_ROWS, lb), BF16).at[0, 0].set(1.0)
    gspec = _spectra(kk, asum, jnp.concatenate([fmat, lag0], axis=0), lb)

    nct = HY_C // CT
    dsk = hyena_d[l].astype(F32)
    z1 = _hyena_stage(u3, 0, u3, nct, gspec, 0, dsk[0:1], fmat, fimat, lb)
    hy = _hyena_stage(z1, 0, u3, 2 * nct, gspec, nct, dsk[1:2], fmat, fimat, lb)

    out = _ffn2(x1, a.reshape(t, -1), hy.reshape(t, -1), row(head_norm_g[l]), w_out[l].astype(BF16),
                row(ffn2_norm_g[l]), ffn2_w_gate[l].astype(BF16), ffn2_w_up[l].astype(BF16),
                ffn2_w_down[l].astype(BF16), row(final_norm_g))
    return out.reshape(b, seq, d)
```

```python
import functools
import math

import jax
import jax.numpy as jnp
from jax import lax
from jax.experimental import pallas as pl
from jax.experimental.pallas import tpu as pltpu

F32 = jnp.float32
BF16 = jnp.bfloat16

D_MODEL = 1024
MLA_HEADS = 8
NOPE = 64
ROPE = 32
V_DIM = 64
Q_RANK = 256
KV_RANK = 128
ROPE_THETA = 10000.0
HY_C = 512
FILTER_BANDS = 16
FILTER_HIDDEN = 64
HY_ORDER = 2
FFN_HIDDEN = 2816
FFN_RES = 0.5
EPS = 1e-6
GROUP = 64
DECAY_TARGET = 1e-2
MAX_DECAY = math.log(DECAY_TARGET) / 0.3
MIN_DECAY = math.log(DECAY_TARGET) / 1.5

LANES = 128
HALO = 8
BF16_ROWS = 16
HEAD_PAD = 128
VMEM_LIMIT = 56 * 1024 * 1024

TM_FFN = 512
MXU_DIM = 256
FFN_CHUNK = 6 * MXU_DIM
TS_PROJ = 512
TQ = 512
TK = 1024
PAIRS_PER_STEP = 2
ATT_SLOTS = 2
REF_ROWS = 16
ATT_SAFE_EXP = 60.0
F32_HUGE = 3.0e38
CT = 128
P_BLK = 4
RC = 32
FWD_GROUPS = 8
TL_FILT = 1024


def _cparams(sem, vmem_limit=VMEM_LIMIT, flags=None):
    return pltpu.CompilerParams(dimension_semantics=sem, vmem_limit_bytes=vmem_limit, flags=flags)


def _rms(x, g):
    ms = jnp.mean(x * x, axis=-1, keepdims=True)
    return x * lax.rsqrt(ms + EPS) * g


def _swiglu_residual(x, g_ref, wg_ref, wu_ref, wd_ref):
    xn = _rms(x, g_ref[...]).astype(BF16)
    acc = jnp.zeros(x.shape, F32)
    for c0 in range(0, FFN_HIDDEN, FFN_CHUNK):
        c1 = min(c0 + FFN_CHUNK, FFN_HIDDEN)
        gate = jnp.dot(xn, wg_ref[:, c0:c1], preferred_element_type=F32)
        up = jnp.dot(xn, wu_ref[:, c0:c1], preferred_element_type=F32)
        h = (gate * jax.nn.sigmoid(gate) * up).astype(BF16)
        acc = acc + jnp.dot(h, wd_ref[c0:c1, :], preferred_element_type=F32)
    return x + FFN_RES * acc


def _ffn1_kernel(x_ref, g_ref, wg_ref, wu_ref, wd_ref, o_ref):
    o_ref[...] = _swiglu_residual(x_ref[...], g_ref, wg_ref, wu_ref, wd_ref)


def _full(shape):
    return pl.BlockSpec(shape, lambda *_: (0,) * len(shape))


def _ffn1(x2d, g, wg, wu, wd):
    t = x2d.shape[0]
    row = pl.BlockSpec((TM_FFN, D_MODEL), lambda i: (i, 0))
    return pl.pallas_call(
        _ffn1_kernel, name="ffn1",
        grid=(t // TM_FFN,),
        in_specs=[row, _full(g.shape), _full(wg.shape), _full(wu.shape), _full(wd.shape)],
        out_specs=row,
        out_shape=jax.ShapeDtypeStruct(x2d.shape, F32),
        compiler_params=_cparams(("parallel",)),
    )(x2d, g, wg, wu, wd)


def _inproj_kernel(x_ref, xp_ref, xn_ref, g_ref, win_ref, cw_ref, cb_ref, qg_ref, wqt_ref, wqst_ref,
                   kvg_ref, wk_ref, wvt_ref, cqt_ref, snt_ref, ck_ref, sn_ref,
                   q_ref, k_ref, v_ref, u_ref, *, scale):
    ts = x_ref.shape[1]
    i = pl.program_id(1)
    x_all = jnp.concatenate([xp_ref[0], x_ref[0], xn_ref[0]], axis=0)
    xn = _rms(x_all, g_ref[...]).astype(BF16)
    u_all = jnp.dot(xn, win_ref[...], preferred_element_type=F32)
    u = u_all[HALO:HALO + ts]
    c_q = u[:, 0:Q_RANK]
    c_kv = u[:, Q_RANK:Q_RANK + KV_RANK]
    o_pe = Q_RANK + KV_RANK
    kpe_a = u[:, o_pe:o_pe + LANES]
    kpe_b = u[:, o_pe + LANES:o_pe + 2 * LANES]

    o_hy = o_pe + 2 * LANES
    keep_prev = (i > 0).astype(F32)
    keep_next = (i < pl.num_programs(1) - 1).astype(F32)
    rows = ts + 2 * HALO
    uh = jnp.concatenate([u_all[0:HALO, o_hy:] * keep_prev, u_all[HALO:HALO + ts, o_hy:],
                          u_all[HALO + ts:, o_hy:] * keep_next], axis=0)
    prev = pltpu.roll(uh * cw_ref[0:1, :], 1, axis=0)
    nxt = pltpu.roll(uh * cw_ref[2:3, :], rows - 1, axis=0)
    conv = (cb_ref[...] + prev) + uh * cw_ref[1:2, :] + nxt
    u_ref[0] = conv[HALO:HALO + ts]

    nt = (((1,), (1,)), ((), ()))
    cqn = _rms(c_q, qg_ref[...]).astype(BF16)
    qt = lax.dot_general(wqt_ref[...], cqn, nt, preferred_element_type=F32)
    qst = lax.dot_general(wqst_ref[...], cqn, nt, preferred_element_type=F32)
    cq_t = cqt_ref[...]
    sn_tt = snt_ref[...]
    for h in range(MLA_HEADS):
        sl = slice(h * HEAD_PAD, (h + 1) * HEAD_PAD)
        q_ref[0, sl, :] = ((qt[sl] * cq_t + qst[sl] * sn_tt) * scale).astype(BF16)

    ckn = _rms(c_kv, kvg_ref[...]).astype(BF16)
    kn = jnp.dot(ckn, wk_ref[...], preferred_element_type=F32)
    k_pe = kpe_a * ck_ref[...] + kpe_b * sn_ref[...]
    for h in range(MLA_HEADS):
        sl = slice(h * HEAD_PAD, (h + 1) * HEAD_PAD)
        k_ref[0, h] = (kn[:, sl] + k_pe).astype(BF16)
    vt = lax.dot_general(wvt_ref[...], ckn, nt, preferred_element_type=F32)
    v_ref[0] = vt.reshape(MLA_HEADS // 2, 2 * V_DIM, vt.shape[1]).astype(BF16)


def _inproj(x1, g, win, cw, cb, qg, wqt, wqst, kvg, wk, wvt, cq_tt, sn_tt, ck_t, sn_t):
    b, seq, _ = x1.shape
    ts = TS_PROJ
    scale = (NOPE + ROPE) ** -0.5 * math.log2(math.e)
    hp = MLA_HEADS * HEAD_PAD
    hpt = ts // HALO
    last = seq // HALO - 1
    return pl.pallas_call(
        functools.partial(_inproj_kernel, scale=scale), name="inproj",
        grid=(b, seq // ts),
        in_specs=[pl.BlockSpec((1, ts, D_MODEL), lambda bi, i: (bi, i, 0)),
                  pl.BlockSpec((1, HALO, D_MODEL), lambda bi, i: (bi, jnp.maximum(i * hpt - 1, 0), 0)),
                  pl.BlockSpec((1, HALO, D_MODEL), lambda bi, i: (bi, jnp.minimum((i + 1) * hpt, last), 0)),
                  _full(g.shape), _full(win.shape), _full(cw.shape), _full(cb.shape),
                  _full(qg.shape), _full(wqt.shape),
                  _full(wqst.shape), _full(kvg.shape), _full(wk.shape), _full(wvt.shape),
                  pl.BlockSpec((HEAD_PAD, ts), lambda bi, i: (0, i)),
                  pl.BlockSpec((HEAD_PAD, ts), lambda bi, i: (0, i)),
                  pl.BlockSpec((ts, LANES), lambda bi, i: (i, 0)),
                  pl.BlockSpec((ts, LANES), lambda bi, i: (i, 0))],
        out_specs=[pl.BlockSpec((1, hp, ts), lambda bi, i: (bi, 0, i)),
                   pl.BlockSpec((1, MLA_HEADS, ts, HEAD_PAD), lambda bi, i: (bi, 0, i, 0)),
                   pl.BlockSpec((1, MLA_HEADS // 2, 2 * V_DIM, ts), lambda bi, i: (bi, 0, 0, i)),
                   pl.BlockSpec((1, ts, 3 * HY_C), lambda bi, i: (bi, i, 0))],
        out_shape=[jax.ShapeDtypeStruct((b, hp, seq), BF16),
                   jax.ShapeDtypeStruct((b, MLA_HEADS, seq, HEAD_PAD), BF16),
                   jax.ShapeDtypeStruct((b, MLA_HEADS // 2, 2 * V_DIM, seq), BF16),
                   jax.ShapeDtypeStruct((b, seq, 3 * HY_C), F32)],
        compiler_params=_cparams(("parallel", "parallel")),
    )(x1, x1, x1, g, win, cw, cb, qg, wqt, wqst, kvg, wk, wvt, cq_tt, sn_tt, ck_t, sn_t)


def _attn_robust(pr, q_ref, k_ref, v_ref, o_ref, s_buf, p_buf, mc_buf, al_buf, m_sc, l_sc, acc_sc):
    seq = k_ref.shape[2]
    nk = seq // TK
    nblk = 2 * nk

    def split(n):
        if isinstance(n, int):
            return n // nk, (n % nk) * TK, (n // nk) * HEAD_PAD
        h = (n >= nk).astype(jnp.int32)
        return h, pl.multiple_of((n - h * nk) * TK, TK), pl.multiple_of(h * HEAD_PAD, HEAD_PAD)

    def scores(n, slot):
        h, off, qoff = split(n)
        k = k_ref[0, 2 * pr + h, pl.ds(off, TK), :]
        s = jnp.dot(k, q_ref[0, pl.ds(2 * HEAD_PAD * pr + qoff, HEAD_PAD), :],
                    preferred_element_type=F32)
        s_buf[slot] = s
        mc_buf[slot] = jnp.max(s, axis=0, keepdims=True)

    def softmax(n, slot):
        h, _, _ = split(n)
        m_prev = m_sc[h]
        m_new = jnp.maximum(m_prev, mc_buf[slot])
        alpha = jnp.exp2(m_prev - m_new)
        p = jnp.exp2(s_buf[slot] - m_new)
        l_sc[h] = alpha * l_sc[h] + jnp.sum(p, axis=0, keepdims=True)
        p_buf[slot] = p.astype(BF16)
        al_buf[slot] = alpha
        m_sc[h] = m_new

    def values(n, slot):
        h, off, _ = split(n)
        vt = v_ref[0, pr, :, pl.ds(off, TK)]
        pv = jnp.dot(vt, p_buf[slot], preferred_element_type=F32)
        acc_sc[h] = al_buf[slot] * acc_sc[h] + pv

    m_sc[...] = jnp.full(m_sc.shape, -jnp.inf, F32)
    l_sc[...] = jnp.zeros(l_sc.shape, F32)
    acc_sc[...] = jnp.zeros(acc_sc.shape, F32)

    scores(0, 0)
    scores(1, 1)
    softmax(0, 0)

    def body(i, carry):
        n = 2 * i
        scores(n + 2, 0)
        softmax(n + 1, 1)
        values(n, 0)
        scores(n + 3, 1)
        softmax(n + 2, 0)
        values(n + 1, 1)
        return carry

    lax.fori_loop(0, (nblk - 2) // 2, body, 0)
    softmax(nblk - 1, 1)
    values(nblk - 2, 0)
    values(nblk - 1, 1)

    row = lax.broadcasted_iota(jnp.int32, acc_sc.shape[1:], 0)
    out_t = jnp.where(row < V_DIM, acc_sc[0] / l_sc[0], acc_sc[1] / l_sc[1])
    o_ref[0, :, 2 * V_DIM * pr:2 * V_DIM * (pr + 1)] = out_t.T


def _attn_kernel(q_ref, k_ref, v_ref, o_ref, *scratch):
    seq = k_ref.shape[2]
    bads = []
    for pr in range(PAIRS_PER_STEP):
        outs = []
        bad = None
        for h in range(2):
            qrow = (2 * pr + h) * HEAD_PAD
            qt = q_ref[0, qrow:qrow + HEAD_PAD, :]
            s0 = jnp.dot(k_ref[0, 2 * pr + h, 0:REF_ROWS, :], qt, preferred_element_type=F32)
            m_ref = jnp.max(s0, axis=0, keepdims=True)
            l = acc = None
            for j in range(seq // TK):
                ks = slice(j * TK, (j + 1) * TK)
                s = jnp.dot(k_ref[0, 2 * pr + h, ks, :], qt, preferred_element_type=F32)
                p = jnp.exp2(s - m_ref)
                p_sum = jnp.sum(p, axis=0, keepdims=True)
                pv = jnp.dot(v_ref[0, pr, :, ks], p.astype(BF16), preferred_element_type=F32)
                l = p_sum if l is None else l + p_sum
                acc = pv if acc is None else acc + pv
            out = acc / l
            outs.append(out)
            b_h = jnp.maximum(jnp.max(jnp.where(l < 2.0 ** ATT_SAFE_EXP, 0.0, 1.0)),
                              jnp.max(jnp.where(jnp.abs(out) < F32_HUGE, 0.0, 1.0)))
            bad = b_h if bad is None else jnp.maximum(bad, b_h)
        row = lax.broadcasted_iota(jnp.int32, outs[0].shape, 0)
        o_ref[0, :, 2 * V_DIM * pr:2 * V_DIM * (pr + 1)] = jnp.where(row < V_DIM, outs[0], outs[1]).T
        bads.append(bad)

    for pr in range(PAIRS_PER_STEP):
        pl.when(bads[pr] > 0.0)(
            functools.partial(_attn_robust, pr, q_ref, k_ref, v_ref, o_ref, *scratch))


def _attention(qt, k4, vt):
    b, _, seq = qt.shape
    pps = PAIRS_PER_STEP
    return pl.pallas_call(
        _attn_kernel, name="attn",
        grid=(b, MLA_HEADS // (2 * pps), seq // TQ),
        in_specs=[pl.BlockSpec((1, 2 * pps * HEAD_PAD, TQ), lambda bi, j, qi: (bi, j, qi)),
                  pl.BlockSpec((1, 2 * pps, seq, HEAD_PAD), lambda bi, j, qi: (bi, j, 0, 0)),
                  pl.BlockSpec((1, pps, 2 * V_DIM, seq), lambda bi, j, qi: (bi, j, 0, 0))],
        out_specs=pl.BlockSpec((1, TQ, 2 * pps * V_DIM), lambda bi, j, qi: (bi, qi, j)),
        out_shape=jax.ShapeDtypeStruct((b, seq, MLA_HEADS * V_DIM), F32),
        scratch_shapes=[pltpu.VMEM((ATT_SLOTS, TK, TQ), F32), pltpu.VMEM((ATT_SLOTS, TK, TQ), BF16),
                        pltpu.VMEM((ATT_SLOTS, 1, TQ), F32), pltpu.VMEM((ATT_SLOTS, 1, TQ), F32),
                        pltpu.VMEM((2, 1, TQ), F32), pltpu.VMEM((2, 1, TQ), F32),
                        pltpu.VMEM((2, 2 * V_DIM, TQ), F32)],
        compiler_params=_cparams(("parallel", "parallel", "parallel")),
    )(qt, k4, vt)


def _filter_kernel(bands_ref, w1t_ref, w1c_ref, w1s_ref, b1_ref, w2_ref, b2_ref, w3_ref,
                   fr_ref, dl_ref, kk_ref, asum_ref, *, seq):
    i = pl.program_id(0)
    hp = lax.Precision.HIGHEST
    l_idx = i * TL_FILT + lax.broadcasted_iota(jnp.int32, (1, TL_FILT), 1)
    pos_i = jnp.where(l_idx < seq, seq - l_idx, l_idx - seq)
    pos = pos_i.astype(F32)
    t = pos / float(max(seq - 1, 1))
    ang = (2.0 * math.pi * pos / seq) * bands_ref[...]
    pre = (t * w1t_ref[...]
           + jnp.dot(w1c_ref[...], jnp.cos(ang), precision=hp, preferred_element_type=F32)
           + jnp.dot(w1s_ref[...], -jnp.sin(ang), precision=hp, preferred_element_type=F32))
    fr = fr_ref[...]
    h = jnp.sin(fr * (pre + b1_ref[...]))
    h = jnp.sin(fr * (jnp.dot(w2_ref[...], h, precision=hp, preferred_element_type=F32) + b2_ref[...]))
    h = jnp.dot(w3_ref[0], h.astype(BF16), preferred_element_type=F32)
    decay = jnp.exp(-t * dl_ref[...])
    kk = jnp.where(l_idx == 0, 0.0, h * decay)
    kk_ref[...] = kk

    @pl.when(i == 0)
    def _():
        asum_ref[...] = jnp.zeros(asum_ref.shape, F32)

    asum_ref[...] += jnp.sum(jnp.abs(kk), axis=1, keepdims=True)


def _filters(bands, w1t, w1c, w1s, b1, w2, b2, w3d, fr, dl, seq):
    nl = 2 * seq // TL_FILT
    half = seq // TL_FILT
    ncol = HY_ORDER * HY_C
    return pl.pallas_call(
        functools.partial(_filter_kernel, seq=seq), name="hyfilt",
        grid=(nl,),
        in_specs=[_full(bands.shape), _full(w1t.shape), _full(w1c.shape), _full(w1s.shape),
                  _full(b1.shape), _full(w2.shape), _full(b2.shape),
                  pl.BlockSpec((1, ncol, FILTER_HIDDEN), lambda i: (jnp.where(i < half, 1, 0), 0, 0)),
                  _full(fr.shape), _full(dl.shape)],
        out_specs=[pl.BlockSpec((ncol, TL_FILT), lambda i: (0, i)),
                   pl.BlockSpec((ncol, 1), lambda i: (0, 0))],
        out_shape=[jax.ShapeDtypeStruct((ncol, 2 * seq), F32),
                   jax.ShapeDtypeStruct((ncol, 1), F32)],
        compiler_params=_cparams(("arbitrary",)),
    )(bands, w1t, w1c, w1s, b1, w2, b2, w3d, fr, dl)


def _spec_kernel(kk_ref, asum_ref, f_ref, g_ref, *, lb):
    ct = kk_ref.shape[0]
    nhalf = kk_ref.shape[1] // lb
    kkn = (kk_ref[...] / asum_ref[...]).astype(BF16)
    x8 = jnp.concatenate([kkn[:, m * lb:(m + 1) * lb] for m in range(nhalf)], axis=0)
    hh = lax.dot_general(f_ref[...], x8, (((1,), (1,)), ((), ())),
                         preferred_element_type=F32)
    row = lax.broadcasted_iota(jnp.int32, (2 * lb, 1), 0)
    sgn = (1 - 2 * (row & 1)).astype(F32)
    low = (((row // RC) & 1) == 0) | (row == RC)
    for dd in range(nhalf - 1):
        h_neg = hh[0:2 * lb, dd * ct:(dd + 1) * ct]
        h_pos = hh[0:2 * lb, (dd + 1) * ct:(dd + 2) * ct]
        h0 = hh[2 * lb:2 * lb + 1, dd * ct:(dd + 1) * ct]
        g_ref[dd] = h_pos + sgn * (h_neg - jnp.where(low, h0, 0.0))


def _spectra(kk, asum, fmat_ext, lb):
    ncol = kk.shape[0]
    nd = 2 * P_BLK - 1
    return pl.pallas_call(
        functools.partial(_spec_kernel, lb=lb), name="hyspec",
        grid=(ncol // CT,),
        in_specs=[pl.BlockSpec((CT, kk.shape[1]), lambda c: (c, 0)),
                  pl.BlockSpec((CT, 1), lambda c: (c, 0)),
                  _full(fmat_ext.shape)],
        out_specs=pl.BlockSpec((nd, 2 * lb, CT), lambda c: (0, 0, c)),
        out_shape=jax.ShapeDtypeStruct((nd, 2 * lb, ncol), F32),
        compiler_params=_cparams(("parallel",)),
    )(kk, asum, fmat_ext)


def _hyena_kernel(a_ref, gt_ref, g_ref, d_ref, f_ref, fi_ref, o_ref, *, lb):
    ct = a_ref.shape[2]
    x4 = jnp.concatenate([a_ref[0, j * lb:(j + 1) * lb, :].astype(BF16) for j in range(P_BLK)],
                         axis=1)
    grp = 2 * lb // FWD_GROUPS
    us = [jnp.dot(f_ref[q * grp:(q + 1) * grp, :], x4, preferred_element_type=F32)
          for q in range(FWD_GROUPS)]
    first = lax.broadcasted_iota(jnp.int32, (RC, 1), 0) == 0
    y_rows = []
    for r in range(lb // RC):
        re0, im0 = 2 * RC * r, 2 * RC * r + RC
        u = us[re0 // grp]
        ure = [u[re0 % grp:re0 % grp + RC, j * ct:(j + 1) * ct] for j in range(P_BLK)]
        uim = [u[im0 % grp:im0 % grp + RC, j * ct:(j + 1) * ct] for j in range(P_BLK)]
        y_re, y_im = [], []
        for i in range(P_BLK):
            yre = yim = None
            dc = ny = None
            for j in range(P_BLK):
                dd = i - j + P_BLK - 1
                gre = g_ref[dd, re0:re0 + RC, :]
                gim = g_ref[dd, im0:im0 + RC, :]
                rr = ure[j] * gre
                ii = uim[j] * gim
                tre = rr - ii
                tim = ure[j] * gim + uim[j] * gre
                yre = tre if yre is None else yre + tre
                yim = tim if yim is None else yim + tim
                if r == 0:
                    dc = rr if dc is None else dc + rr
                    ny = ii if ny is None else ny + ii
            if r == 0:
                yre = jnp.where(first, dc, yre)
                yim = jnp.where(first, ny, yim)
            y_re.append(yre.astype(BF16))
            y_im.append(yim.astype(BF16))
        y_rows.append(jnp.concatenate(y_re, axis=1))
        y_rows.append(jnp.concatenate(y_im, axis=1))
    y = jnp.concatenate(y_rows, axis=0)
    y4 = jnp.dot(fi_ref[...], y, preferred_element_type=F32)
    for i in range(P_BLK):
        rs = slice(i * lb, (i + 1) * lb)
        o_ref[0, rs, :] = (y4[:, i * ct:(i + 1) * ct] + a_ref[0, rs, :] * d_ref[...]) * gt_ref[0, rs, :]


def _hyena_stage(a_arr, a_blk0, g_arr, g_blk0, gspec, g_blk0_spec, d, fmat, fimat, lb):
    b, seq, _ = a_arr.shape
    nct = HY_C // CT
    nd = 2 * P_BLK - 1
    return pl.pallas_call(
        functools.partial(_hyena_kernel, lb=lb), name="hyena",
        grid=(nct, b),
        in_specs=[pl.BlockSpec((1, seq, CT), lambda c, bi: (bi, 0, a_blk0 + c)),
                  pl.BlockSpec((1, seq, CT), lambda c, bi: (bi, 0, g_blk0 + c)),
                  pl.BlockSpec((nd, 2 * lb, CT), lambda c, bi: (0, 0, g_blk0_spec + c),
                               pipeline_mode=pl.Buffered(1)),
                  pl.BlockSpec((1, CT), lambda c, bi: (0, c)),
                  _full(fmat.shape), _full(fimat.shape)],
        out_specs=pl.BlockSpec((1, seq, CT), lambda c, bi: (bi, 0, c)),
        out_shape=jax.ShapeDtypeStruct((b, seq, HY_C), F32),
        compiler_params=_cparams(("parallel", "parallel")),
    )(a_arr, g_arr, gspec, d, fmat, fimat)


def _group_norm_tile(xs, g):
    lane = lax.broadcasted_iota(jnp.int32, xs.shape, 1)
    lo = lane < GROUP
    sq = xs * xs
    s_lo = jnp.sum(jnp.where(lo, sq, 0.0), axis=-1, keepdims=True)
    s_hi = jnp.sum(jnp.where(lo, 0.0, sq), axis=-1, keepdims=True)
    r = jnp.where(lo, lax.rsqrt(s_lo / GROUP + EPS), lax.rsqrt(s_hi / GROUP + EPS))
    return xs * r * g


def _ffn2_kernel(x_ref, a_ref, h_ref, hg_ref, wo_ref, g_ref, wg_ref, wu_ref, wd_ref, gf_ref, o_ref):
    tiles = []
    n_a = a_ref.shape[1] // LANES
    for s in range(D_MODEL // LANES):
        src = a_ref[:, s * LANES:(s + 1) * LANES] if s < n_a else \
            h_ref[:, (s - n_a) * LANES:(s - n_a + 1) * LANES]
        tiles.append(_group_norm_tile(src, hg_ref[:, s * LANES:(s + 1) * LANES]).astype(BF16))
    mix = jnp.concatenate(tiles, axis=1)
    x = x_ref[...] + jnp.dot(mix, wo_ref[...], preferred_element_type=F32)
    y = _swiglu_residual(x, g_ref, wg_ref, wu_ref, wd_ref)
    o_ref[...] = _rms(y, gf_ref[...])


def _ffn2(x1, a2d, h2d, hg, wo, g, wg, wu, wd, gf):
    t = x1.shape[0]
    row = lambda w: pl.BlockSpec((TM_FFN, w), lambda i: (i, 0))
    return pl.pallas_call(
        _ffn2_kernel, name="ffn2",
        grid=(t // TM_FFN,),
        in_specs=[row(D_MODEL), row(a2d.shape[1]), row(h2d.shape[1]), _full(hg.shape), _full(wo.shape),
                  _full(g.shape), _full(wg.shape), _full(wu.shape), _full(wd.shape), _full(gf.shape)],
        out_specs=row(D_MODEL),
        out_shape=jax.ShapeDtypeStruct(x1.shape, F32),
        compiler_params=_cparams(("parallel",)),
    )(x1, a2d, h2d, hg, wo, g, wg, wu, wd, gf)


def _rope_tables(seq):
    inv = 1.0 / (ROPE_THETA ** (jnp.arange(0, ROPE, 2, dtype=F32) / ROPE))
    ang = jnp.arange(seq, dtype=F32)[:, None] * inv[None, :]
    cos, sin = jnp.cos(ang), jnp.sin(ang)
    z64 = jnp.zeros((seq, NOPE), F32)
    z32 = jnp.zeros((seq, HEAD_PAD - NOPE - ROPE), F32)
    cq_t = jnp.concatenate([jnp.ones((seq, NOPE), F32), cos, cos, z32], axis=1)
    ck_t = jnp.concatenate([z64, cos, cos, z32], axis=1)
    sn_t = jnp.concatenate([z64, -sin, sin, z32], axis=1)
    return cq_t, ck_t, sn_t


def _dft_matrices(lb):
    n2 = 2 * lb
    nch = lb // RC
    two_pi = 2.0 * math.pi

    def tables(c, q, n):
        ang_c = (two_pi / (n2 // RC)) * ((c * n) % (n2 // RC)).astype(F32)
        ang_q = (two_pi / n2) * ((q * n) % n2).astype(F32)
        cc, sc, cq, sq = jnp.cos(ang_c), jnp.sin(ang_c), jnp.cos(ang_q), jnp.sin(ang_q)
        return cc * cq - sc * sq, sc * cq + cc * sq

    ar = lambda m: jnp.arange(m, dtype=jnp.int32)
    alt = lambda n: (1 - 2 * (n % 2)).astype(F32)
    c, q, n = ar(nch)[:, None, None], ar(RC)[None, :, None], ar(lb)[None, None, :]
    cos, sin = tables(c, q, n)
    f0 = (c == 0) & (q == 0)
    fwd = jnp.stack([cos, jnp.where(f0, alt(n), -sin)], axis=1).reshape(n2, lb)
    n, c, q = ar(lb)[:, None, None], ar(nch)[None, :, None], ar(RC)[None, None, :]
    cos, sin = tables(c, q, n)
    f0 = (c == 0) & (q == 0)
    inv_re = jnp.where(f0, 1.0 / n2, (2.0 / n2) * cos)
    inv_im = jnp.where(f0, alt(n) / n2, (-2.0 / n2) * sin)
    inv = jnp.stack([inv_re, inv_im], axis=2).reshape(lb, n2)
    return fwd.astype(BF16), inv.astype(BF16)


def kernel(x, ffn1_norm_g, ffn1_w_gate, ffn1_w_up, ffn1_w_down, mix_norm_g, w_in, q_norm_g, w_uq, kv_norm_g, w_ukv, hyena_conv_w, hyena_conv_b, filt_w1, filt_b1, filt_w2, filt_b2, filt_w3, filt_freq, hyena_d, head_norm_g, w_out, ffn2_norm_g, ffn2_w_gate, ffn2_w_up, ffn2_w_down, final_norm_g):
    b, seq, d = x.shape
    t = b * seq
    lb = seq // P_BLK
    l = 0
    row = lambda v: v.reshape(1, -1)

    x1 = _ffn1(x.reshape(t, d), row(ffn1_norm_g[l]), ffn1_w_gate[l].astype(BF16),
               ffn1_w_up[l].astype(BF16), ffn1_w_down[l].astype(BF16))

    wi = w_in[l]
    o_kr = Q_RANK + KV_RANK
    half = ROPE // 2
    zc = lambda n: jnp.zeros((d, n), F32)
    kr = wi[:, o_kr:o_kr + ROPE]
    win_p = jnp.concatenate([
        wi[:, :o_kr],
        zc(NOPE), kr, zc(HEAD_PAD - NOPE - ROPE),
        zc(NOPE), kr[:, half:], kr[:, :half], zc(HEAD_PAD - NOPE - ROPE),
        wi[:, o_kr + ROPE:]], axis=1).astype(BF16)
    wq3 = w_uq[l].reshape(Q_RANK, MLA_HEADS, NOPE + ROPE)
    zq = lambda n: jnp.zeros((Q_RANK, MLA_HEADS, n), F32)
    wq_p = jnp.concatenate([wq3, zq(HEAD_PAD - NOPE - ROPE)], axis=2).reshape(Q_RANK, -1).astype(BF16)
    wqs_p = jnp.concatenate([zq(NOPE), wq3[:, :, NOPE + half:], wq3[:, :, NOPE:NOPE + half],
                             zq(HEAD_PAD - NOPE - ROPE)], axis=2).reshape(Q_RANK, -1).astype(BF16)
    wkv3 = w_ukv[l].reshape(KV_RANK, MLA_HEADS, NOPE + V_DIM)
    wk_p = jnp.concatenate([wkv3[:, :, :NOPE], jnp.zeros((KV_RANK, MLA_HEADS, HEAD_PAD - NOPE), F32)],
                           axis=2).reshape(KV_RANK, -1).astype(BF16)
    wv_p = wkv3[:, :, NOPE:].reshape(KV_RANK, -1).astype(BF16)
    cq_t, ck_t, sn_t = _rope_tables(seq)
    qt, k4, vt, u3 = _inproj(x1.reshape(b, seq, d), row(mix_norm_g[l]), win_p,
                             hyena_conv_w[l], row(hyena_conv_b[l]), row(q_norm_g[l]),
                             wq_p.T, wqs_p.T, row(kv_norm_g[l]), wk_p, wv_p.T,
                             cq_t.T, sn_t.T, ck_t, sn_t)

    a = _attention(qt, k4, vt)

    col = lambda v: v.astype(F32).reshape(-1, 1)
    bands = col(jnp.linspace(1e-4, FILTER_BANDS - 1, FILTER_BANDS, dtype=F32))
    w1 = filt_w1[l].astype(F32)
    w3d = filt_w3[l].astype(F32).reshape(FILTER_HIDDEN, HY_ORDER, 2, HY_C).transpose(2, 1, 3, 0)
    w3d = w3d.reshape(2, HY_ORDER * HY_C, FILTER_HIDDEN).astype(BF16)
    deltas = jnp.abs(jnp.linspace(MIN_DECAY, MAX_DECAY, HY_C, dtype=F32))
    dl = col(jnp.tile(deltas, HY_ORDER))
    kk, asum = _filters(bands, col(w1[0]), w1[1:1 + FILTER_BANDS].T, w1[1 + FILTER_BANDS:].T,
                        col(filt_b1[l]), filt_w2[l].astype(F32).T, col(filt_b2[l]),
                        w3d, col(filt_freq[l]), dl, seq)
    fmat, fimat = _dft_matrices(lb)
    lag0 = jnp.zeros((BF16_ROWS, lb), BF16).at[0, 0].set(1.0)
    gspec = _spectra(kk, asum, jnp.concatenate([fmat, lag0], axis=0), lb)

    nct = HY_C // CT
    dsk = hyena_d[l].astype(F32)
    z1 = _hyena_stage(u3, 0, u3, nct, gspec, 0, dsk[0:1], fmat, fimat, lb)
    hy = _hyena_stage(z1, 0, u3, 2 * nct, gspec, nct, dsk[1:2], fmat, fimat, lb)

    out = _ffn2(x1, a.reshape(t, -1), hy.reshape(t, -1), row(head_norm_g[l]), w_out[l].astype(BF16),
                row(ffn2_norm_g[l]), ffn2_w_gate[l].astype(BF16), ffn2_w_up[l].astype(BF16),
                ffn2_w_down[l].astype(BF16), row(final_norm_g))
    return out.reshape(b, seq, d)
```

```python
import functools
import math

import jax
import jax.numpy as jnp
from jax import lax
from jax.experimental import pallas as pl
from jax.experimental.pallas import tpu as pltpu

F32 = jnp.float32
BF16 = jnp.bfloat16

D_MODEL = 1024
MLA_HEADS = 8
NOPE = 64
ROPE = 32
V_DIM = 64
Q_RANK = 256
KV_RANK = 128
ROPE_THETA = 10000.0
HY_C = 512
FILTER_BANDS = 16
FILTER_HIDDEN = 64
HY_ORDER = 2
FFN_HIDDEN = 2816
FFN_RES = 0.5
EPS = 1e-6
GROUP = 64
DECAY_TARGET = 1e-2
MAX_DECAY = math.log(DECAY_TARGET) / 0.3
MIN_DECAY = math.log(DECAY_TARGET) / 1.5

LANES = 128
HALO = 8
BF16_ROWS = 16
HEAD_PAD = 128
VMEM_LIMIT = 56 * 1024 * 1024

TM_FFN = 512
MXU_DIM = 256
FFN_CHUNK = 6 * MXU_DIM
TS_PROJ = 512
TQ = 512
TK = 1024
PAIRS_PER_STEP = 4
ATT_SLOTS = 2
REF_ROWS = 16
ATT_SAFE_EXP = 60.0
F32_HUGE = 3.0e38
CT = 128
P_BLK = 4
RC = 32
FWD_GROUPS = 8
TL_FILT = 1024


def _cparams(sem, vmem_limit=VMEM_LIMIT, flags=None):
    return pltpu.CompilerParams(dimension_semantics=sem, vmem_limit_bytes=vmem_limit, flags=flags)


def _rms(x, g):
    ms = jnp.mean(x * x, axis=-1, keepdims=True)
    return x * lax.rsqrt(ms + EPS) * g


def _swiglu_residual(x, g_ref, wg_ref, wu_ref, wd_ref):
    xn = _rms(x, g_ref[...]).astype(BF16)
    acc = jnp.zeros(x.shape, F32)
    for c0 in range(0, FFN_HIDDEN, FFN_CHUNK):
        c1 = min(c0 + FFN_CHUNK, FFN_HIDDEN)
        gate = jnp.dot(xn, wg_ref[:, c0:c1], preferred_element_type=F32)
        up = jnp.dot(xn, wu_ref[:, c0:c1], preferred_element_type=F32)
        h = (gate * jax.nn.sigmoid(gate) * up).astype(BF16)
        acc = acc + jnp.dot(h, wd_ref[c0:c1, :], preferred_element_type=F32)
    return x + FFN_RES * acc


def _ffn1_kernel(x_ref, g_ref, wg_ref, wu_ref, wd_ref, o_ref):
    o_ref[...] = _swiglu_residual(x_ref[...], g_ref, wg_ref, wu_ref, wd_ref)


def _full(shape):
    return pl.BlockSpec(shape, lambda *_: (0,) * len(shape))


def _ffn1(x2d, g, wg, wu, wd):
    t = x2d.shape[0]
    row = pl.BlockSpec((TM_FFN, D_MODEL), lambda i: (i, 0))
    return pl.pallas_call(
        _ffn1_kernel, name="ffn1",
        grid=(t // TM_FFN,),
        in_specs=[row, _full(g.shape), _full(wg.shape), _full(wu.shape), _full(wd.shape)],
        out_specs=row,
        out_shape=jax.ShapeDtypeStruct(x2d.shape, F32),
        compiler_params=_cparams(("parallel",)),
    )(x2d, g, wg, wu, wd)


def _inproj_kernel(x_ref, xp_ref, xn_ref, g_ref, win_ref, cw_ref, cb_ref, qg_ref, wqt_ref, wqst_ref,
                   kvg_ref, wk_ref, wvt_ref, cqt_ref, snt_ref, ck_ref, sn_ref,
                   q_ref, k_ref, v_ref, u_ref, *, scale):
    ts = x_ref.shape[1]
    i = pl.program_id(1)
    x_all = jnp.concatenate([xp_ref[0], x_ref[0], xn_ref[0]], axis=0)
    xn = _rms(x_all, g_ref[...]).astype(BF16)
    u_all = jnp.dot(xn, win_ref[...], preferred_element_type=F32)
    u = u_all[HALO:HALO + ts]
    c_q = u[:, 0:Q_RANK]
    c_kv = u[:, Q_RANK:Q_RANK + KV_RANK]
    o_pe = Q_RANK + KV_RANK
    kpe_a = u[:, o_pe:o_pe + LANES]
    kpe_b = u[:, o_pe + LANES:o_pe + 2 * LANES]

    o_hy = o_pe + 2 * LANES
    keep_prev = (i > 0).astype(F32)
    keep_next = (i < pl.num_programs(1) - 1).astype(F32)
    rows = ts + 2 * HALO
    uh = jnp.concatenate([u_all[0:HALO, o_hy:] * keep_prev, u_all[HALO:HALO + ts, o_hy:],
                          u_all[HALO + ts:, o_hy:] * keep_next], axis=0)
    prev = pltpu.roll(uh * cw_ref[0:1, :], 1, axis=0)
    nxt = pltpu.roll(uh * cw_ref[2:3, :], rows - 1, axis=0)
    conv = (cb_ref[...] + prev) + uh * cw_ref[1:2, :] + nxt
    u_ref[0] = conv[HALO:HALO + ts]

    nt = (((1,), (1,)), ((), ()))
    cqn = _rms(c_q, qg_ref[...]).astype(BF16)
    qt = lax.dot_general(wqt_ref[...], cqn, nt, preferred_element_type=F32)
    qst = lax.dot_general(wqst_ref[...], cqn, nt, preferred_element_type=F32)
    cq_t = cqt_ref[...]
    sn_tt = snt_ref[...]
    for h in range(MLA_HEADS):
        sl = slice(h * HEAD_PAD, (h + 1) * HEAD_PAD)
        q_ref[0, sl, :] = ((qt[sl] * cq_t + qst[sl] * sn_tt) * scale).astype(BF16)

    ckn = _rms(c_kv, kvg_ref[...]).astype(BF16)
    kn = jnp.dot(ckn, wk_ref[...], preferred_element_type=F32)
    k_pe = kpe_a * ck_ref[...] + kpe_b * sn_ref[...]
    for h in range(MLA_HEADS):
        sl = slice(h * HEAD_PAD, (h + 1) * HEAD_PAD)
        k_ref[0, h] = (kn[:, sl] + k_pe).astype(BF16)
    vt = lax.dot_general(wvt_ref[...], ckn, nt, preferred_element_type=F32)
    v_ref[0] = vt.reshape(MLA_HEADS // 2, 2 * V_DIM, vt.shape[1]).astype(BF16)


def _inproj(x1, g, win, cw, cb, qg, wqt, wqst, kvg, wk, wvt, cq_tt, sn_tt, ck_t, sn_t):
    b, seq, _ = x1.shape
    ts = TS_PROJ
    scale = (NOPE + ROPE) ** -0.5 * math.log2(math.e)
    hp = MLA_HEADS * HEAD_PAD
    hpt = ts // HALO
    last = seq // HALO - 1
    return pl.pallas_call(
        functools.partial(_inproj_kernel, scale=scale), name="inproj",
        grid=(b, seq // ts),
        in_specs=[pl.BlockSpec((1, ts, D_MODEL), lambda bi, i: (bi, i, 0)),
                  pl.BlockSpec((1, HALO, D_MODEL), lambda bi, i: (bi, jnp.maximum(i * hpt - 1, 0), 0)),
                  pl.BlockSpec((1, HALO, D_MODEL), lambda bi, i: (bi, jnp.minimum((i + 1) * hpt, last), 0)),
                  _full(g.shape), _full(win.shape), _full(cw.shape), _full(cb.shape),
                  _full(qg.shape), _full(wqt.shape),
                  _full(wqst.shape), _full(kvg.shape), _full(wk.shape), _full(wvt.shape),
                  pl.BlockSpec((HEAD_PAD, ts), lambda bi, i: (0, i)),
                  pl.BlockSpec((HEAD_PAD, ts), lambda bi, i: (0, i)),
                  pl.BlockSpec((ts, LANES), lambda bi, i: (i, 0)),
                  pl.BlockSpec((ts, LANES), lambda bi, i: (i, 0))],
        out_specs=[pl.BlockSpec((1, hp, ts), lambda bi, i: (bi, 0, i)),
                   pl.BlockSpec((1, MLA_HEADS, ts, HEAD_PAD), lambda bi, i: (bi, 0, i, 0)),
                   pl.BlockSpec((1, MLA_HEADS // 2, 2 * V_DIM, ts), lambda bi, i: (bi, 0, 0, i)),
                   pl.BlockSpec((1, ts, 3 * HY_C), lambda bi, i: (bi, i, 0))],
        out_shape=[jax.ShapeDtypeStruct((b, hp, seq), BF16),
                   jax.ShapeDtypeStruct((b, MLA_HEADS, seq, HEAD_PAD), BF16),
                   jax.ShapeDtypeStruct((b, MLA_HEADS // 2, 2 * V_DIM, seq), BF16),
                   jax.ShapeDtypeStruct((b, seq, 3 * HY_C), F32)],
        compiler_params=_cparams(("parallel", "parallel")),
    )(x1, x1, x1, g, win, cw, cb, qg, wqt, wqst, kvg, wk, wvt, cq_tt, sn_tt, ck_t, sn_t)


def _attn_robust(pr, q_ref, k_ref, v_ref, o_ref, s_buf, p_buf, mc_buf, al_buf, m_sc, l_sc, acc_sc):
    seq = k_ref.shape[2]
    nk = seq // TK
    nblk = 2 * nk

    def split(n):
        if isinstance(n, int):
            return n // nk, (n % nk) * TK, (n // nk) * HEAD_PAD
        h = (n >= nk).astype(jnp.int32)
        return h, pl.multiple_of((n - h * nk) * TK, TK), pl.multiple_of(h * HEAD_PAD, HEAD_PAD)

    def scores(n, slot):
        h, off, qoff = split(n)
        k = k_ref[0, 2 * pr + h, pl.ds(off, TK), :]
        s = jnp.dot(k, q_ref[0, pl.ds(2 * HEAD_PAD * pr + qoff, HEAD_PAD), :],
                    preferred_element_type=F32)
        s_buf[slot] = s
        mc_buf[slot] = jnp.max(s, axis=0, keepdims=True)

    def softmax(n, slot):
        h, _, _ = split(n)
        m_prev = m_sc[h]
        m_new = jnp.maximum(m_prev, mc_buf[slot])
        alpha = jnp.exp2(m_prev - m_new)
        p = jnp.exp2(s_buf[slot] - m_new)
        l_sc[h] = alpha * l_sc[h] + jnp.sum(p, axis=0, keepdims=True)
        p_buf[slot] = p.astype(BF16)
        al_buf[slot] = alpha
        m_sc[h] = m_new

    def values(n, slot):
        h, off, _ = split(n)
        vt = v_ref[0, pr, :, pl.ds(off, TK)]
        pv = jnp.dot(vt, p_buf[slot], preferred_element_type=F32)
        acc_sc[h] = al_buf[slot] * acc_sc[h] + pv

    m_sc[...] = jnp.full(m_sc.shape, -jnp.inf, F32)
    l_sc[...] = jnp.zeros(l_sc.shape, F32)
    acc_sc[...] = jnp.zeros(acc_sc.shape, F32)

    scores(0, 0)
    scores(1, 1)
    softmax(0, 0)

    def body(i, carry):
        n = 2 * i
        scores(n + 2, 0)
        softmax(n + 1, 1)
        values(n, 0)
        scores(n + 3, 1)
        softmax(n + 2, 0)
        values(n + 1, 1)
        return carry

    lax.fori_loop(0, (nblk - 2) // 2, body, 0)
    softmax(nblk - 1, 1)
    values(nblk - 2, 0)
    values(nblk - 1, 1)

    row = lax.broadcasted_iota(jnp.int32, acc_sc.shape[1:], 0)
    out_t = jnp.where(row < V_DIM, acc_sc[0] / l_sc[0], acc_sc[1] / l_sc[1])
    o_ref[0, :, 2 * V_DIM * pr:2 * V_DIM * (pr + 1)] = out_t.T


def _attn_kernel(q_ref, k_ref, v_ref, o_ref, *scratch):
    seq = k_ref.shape[2]
    bads = []
    for pr in range(PAIRS_PER_STEP):
        outs = []
        bad = None
        for h in range(2):
            qrow = (2 * pr + h) * HEAD_PAD
            qt = q_ref[0, qrow:qrow + HEAD_PAD, :]
            s0 = jnp.dot(k_ref[0, 2 * pr + h, 0:REF_ROWS, :], qt, preferred_element_type=F32)
            m_ref = jnp.max(s0, axis=0, keepdims=True)
            l = acc = None
            for j in range(seq // TK):
                ks = slice(j * TK, (j + 1) * TK)
                s = jnp.dot(k_ref[0, 2 * pr + h, ks, :], qt, preferred_element_type=F32)
                p = jnp.exp2(s - m_ref)
                p_sum = jnp.sum(p, axis=0, keepdims=True)
                pv = jnp.dot(v_ref[0, pr, :, ks], p.astype(BF16), preferred_element_type=F32)
                l = p_sum if l is None else l + p_sum
                acc = pv if acc is None else acc + pv
            out = acc / l
            outs.append(out)
            b_h = jnp.maximum(jnp.max(jnp.where(l < 2.0 ** ATT_SAFE_EXP, 0.0, 1.0)),
                              jnp.max(jnp.where(jnp.abs(out) < F32_HUGE, 0.0, 1.0)))
            bad = b_h if bad is None else jnp.maximum(bad, b_h)
        row = lax.broadcasted_iota(jnp.int32, outs[0].shape, 0)
        o_ref[0, :, 2 * V_DIM * pr:2 * V_DIM * (pr + 1)] = jnp.where(row < V_DIM, outs[0], outs[1]).T
        bads.append(bad)

    for pr in range(PAIRS_PER_STEP):
        pl.when(bads[pr] > 0.0)(
            functools.partial(_attn_robust, pr, q_ref, k_ref, v_ref, o_ref, *scratch))


def _attention(qt, k4, vt):
    b, _, seq = qt.shape
    pps = PAIRS_PER_STEP
    return pl.pallas_call(
        _attn_kernel, name="attn",
        grid=(b, MLA_HEADS // (2 * pps), seq // TQ),
        in_specs=[pl.BlockSpec((1, 2 * pps * HEAD_PAD, TQ), lambda bi, j, qi: (bi, j, qi)),
                  pl.BlockSpec((1, 2 * pps, seq, HEAD_PAD), lambda bi, j, qi: (bi, j, 0, 0)),
                  pl.BlockSpec((1, pps, 2 * V_DIM, seq), lambda bi, j, qi: (bi, j, 0, 0))],
        out_specs=pl.BlockSpec((1, TQ, 2 * pps * V_DIM), lambda bi, j, qi: (bi, qi, j)),
        out_shape=jax.ShapeDtypeStruct((b, seq, MLA_HEADS * V_DIM), F32),
        scratch_shapes=[pltpu.VMEM((ATT_SLOTS, TK, TQ), F32), pltpu.VMEM((ATT_SLOTS, TK, TQ), BF16),
                        pltpu.VMEM((ATT_SLOTS, 1, TQ), F32), pltpu.VMEM((ATT_SLOTS, 1, TQ), F32),
                        pltpu.VMEM((2, 1, TQ), F32), pltpu.VMEM((2, 1, TQ), F32),
                        pltpu.VMEM((2, 2 * V_DIM, TQ), F32)],
        compiler_params=_cparams(("parallel", "parallel", "parallel")),
    )(qt, k4, vt)


def _filter_kernel(bands_ref, w1t_ref, w1c_ref, w1s_ref, b1_ref, w2_ref, b2_ref, w3_ref,
                   fr_ref, dl_ref, kk_ref, asum_ref, *, seq):
    i = pl.program_id(0)
    hp = lax.Precision.HIGHEST
    l_idx = i * TL_FILT + lax.broadcasted_iota(jnp.int32, (1, TL_FILT), 1)
    pos_i = jnp.where(l_idx < seq, seq - l_idx, l_idx - seq)
    pos = pos_i.astype(F32)
    t = pos / float(max(seq - 1, 1))
    ang = (2.0 * math.pi * pos / seq) * bands_ref[...]
    pre = (t * w1t_ref[...]
           + jnp.dot(w1c_ref[...], jnp.cos(ang), precision=hp, preferred_element_type=F32)
           + jnp.dot(w1s_ref[...], -jnp.sin(ang), precision=hp, preferred_element_type=F32))
    fr = fr_ref[...]
    h = jnp.sin(fr * (pre + b1_ref[...]))
    h = jnp.sin(fr * (jnp.dot(w2_ref[...], h, precision=hp, preferred_element_type=F32) + b2_ref[...]))
    h = jnp.dot(w3_ref[0], h.astype(BF16), preferred_element_type=F32)
    decay = jnp.exp(-t * dl_ref[...])
    kk = jnp.where(l_idx == 0, 0.0, h * decay)
    kk_ref[...] = kk

    @pl.when(i == 0)
    def _():
        asum_ref[...] = jnp.zeros(asum_ref.shape, F32)

    asum_ref[...] += jnp.sum(jnp.abs(kk), axis=1, keepdims=True)


def _filters(bands, w1t, w1c, w1s, b1, w2, b2, w3d, fr, dl, seq):
    nl = 2 * seq // TL_FILT
    half = seq // TL_FILT
    ncol = HY_ORDER * HY_C
    return pl.pallas_call(
        functools.partial(_filter_kernel, seq=seq), name="hyfilt",
        grid=(nl,),
        in_specs=[_full(bands.shape), _full(w1t.shape), _full(w1c.shape), _full(w1s.shape),
                  _full(b1.shape), _full(w2.shape), _full(b2.shape),
                  pl.BlockSpec((1, ncol, FILTER_HIDDEN), lambda i: (jnp.where(i < half, 1, 0), 0, 0)),
                  _full(fr.shape), _full(dl.shape)],
        out_specs=[pl.BlockSpec((ncol, TL_FILT), lambda i: (0, i)),
                   pl.BlockSpec((ncol, 1), lambda i: (0, 0))],
        out_shape=[jax.ShapeDtypeStruct((ncol, 2 * seq), F32),
                   jax.ShapeDtypeStruct((ncol, 1), F32)],
        compiler_params=_cparams(("arbitrary",)),
    )(bands, w1t, w1c, w1s, b1, w2, b2, w3d, fr, dl)


def _spec_kernel(kk_ref, asum_ref, f_ref, g_ref, *, lb):
    ct = kk_ref.shape[0]
    nhalf = kk_ref.shape[1] // lb
    kkn = (kk_ref[...] / asum_ref[...]).astype(BF16)
    x8 = jnp.concatenate([kkn[:, m * lb:(m + 1) * lb] for m in range(nhalf)], axis=0)
    hh = lax.dot_general(f_ref[...], x8, (((1,), (1,)), ((), ())),
                         preferred_element_type=F32)
    row = lax.broadcasted_iota(jnp.int32, (2 * lb, 1), 0)
    sgn = (1 - 2 * (row & 1)).astype(F32)
    low = row <= lb
    for dd in range(nhalf - 1):
        h_neg = hh[0:2 * lb, dd * ct:(dd + 1) * ct]
        h_pos = hh[0:2 * lb, (dd + 1) * ct:(dd + 2) * ct]
        h0 = hh[2 * lb:2 * lb + 1, dd * ct:(dd + 1) * ct]
        g = h_pos + sgn * (h_neg - jnp.where(low, h0, 0.0))
        g_re, g_im = g[0:lb], g[lb:2 * lb]
        g_ref[dd, 0] = g_re
        g_ref[dd, 1] = g_im - g_re
        g_ref[dd, 2] = g_re + g_im


def _spectra(kk, asum, fmat_ext, lb):
    ncol = kk.shape[0]
    nd = 2 * P_BLK - 1
    return pl.pallas_call(
        functools.partial(_spec_kernel, lb=lb), name="hyspec",
        grid=(ncol // CT,),
        in_specs=[pl.BlockSpec((CT, kk.shape[1]), lambda c: (c, 0)),
                  pl.BlockSpec((CT, 1), lambda c: (c, 0)),
                  _full(fmat_ext.shape)],
        out_specs=pl.BlockSpec((nd, 3, lb, CT), lambda c: (0, 0, 0, c)),
        out_shape=jax.ShapeDtypeStruct((nd, 3, lb, ncol), F32),
        compiler_params=_cparams(("parallel",)),
    )(kk, asum, fmat_ext)


def _hyena_kernel(a_ref, gt_ref, g_ref, d_ref, f_ref, fi_ref, o_ref, *, lb):
    ct = a_ref.shape[2]
    x4 = jnp.concatenate([a_ref[0, j * lb:(j + 1) * lb, :].astype(BF16) for j in range(P_BLK)],
                         axis=1)
    grp = 2 * lb // FWD_GROUPS
    us = [jnp.dot(f_ref[q * grp:(q + 1) * grp, :], x4, preferred_element_type=F32)
          for q in range(FWD_GROUPS)]
    first = lax.broadcasted_iota(jnp.int32, (RC, 1), 0) == 0
    y_rows = []
    for r in range(lb // RC):
        re0, im0 = 2 * RC * r, 2 * RC * r + RC
        u = us[re0 // grp]
        ure = [u[re0 % grp:re0 % grp + RC, j * ct:(j + 1) * ct] for j in range(P_BLK)]
        uim = [u[im0 % grp:im0 % grp + RC, j * ct:(j + 1) * ct] for j in range(P_BLK)]
        usum = [ure[j] + uim[j] for j in range(P_BLK)]
        fs = slice(r * RC, (r + 1) * RC)
        y_re, y_im = [], []
        for i in range(P_BLK):
            k1 = k2 = k3 = None
            dc = ny = None
            for j in range(P_BLK):
                dd = i - j + P_BLK - 1
                g_c, g_dmc, g_cpd = g_ref[dd, 0, fs, :], g_ref[dd, 1, fs, :], g_ref[dd, 2, fs, :]
                t1, t2, t3 = g_c * usum[j], ure[j] * g_dmc, uim[j] * g_cpd
                k1 = t1 if k1 is None else k1 + t1
                k2 = t2 if k2 is None else k2 + t2
                k3 = t3 if k3 is None else k3 + t3
                if r == 0:
                    rr = ure[j] * g_c
                    ii = uim[j] * (g_dmc + g_c)
                    dc = rr if dc is None else dc + rr
                    ny = ii if ny is None else ny + ii
            yre = k1 - k3
            yim = k1 + k2
            if r == 0:
                yre = jnp.where(first, dc, yre)
                yim = jnp.where(first, ny, yim)
            y_re.append(yre.astype(BF16))
            y_im.append(yim.astype(BF16))
        y_rows.append(jnp.concatenate(y_re, axis=1))
        y_rows.append(jnp.concatenate(y_im, axis=1))
    y = jnp.concatenate(y_rows, axis=0)
    y4 = jnp.dot(fi_ref[...], y, preferred_element_type=F32)
    for i in range(P_BLK):
        rs = slice(i * lb, (i + 1) * lb)
        o_ref[0, rs, :] = (y4[:, i * ct:(i + 1) * ct] + a_ref[0, rs, :] * d_ref[...]) * gt_ref[0, rs, :]


def _hyena_stage(a_arr, a_blk0, g_arr, g_blk0, gspec, g_blk0_spec, d, fmat, fimat, lb):
    b, seq, _ = a_arr.shape
    nct = HY_C // CT
    nd = 2 * P_BLK - 1
    return pl.pallas_call(
        functools.partial(_hyena_kernel, lb=lb), name="hyena",
        grid=(nct, b),
        in_specs=[pl.BlockSpec((1, seq, CT), lambda c, bi: (bi, 0, a_blk0 + c)),
                  pl.BlockSpec((1, seq, CT), lambda c, bi: (bi, 0, g_blk0 + c)),
                  pl.BlockSpec((nd, 3, lb, CT), lambda c, bi: (0, 0, 0, g_blk0_spec + c),
                               pipeline_mode=pl.Buffered(1)),
                  pl.BlockSpec((1, CT), lambda c, bi: (0, c)),
                  _full(fmat.shape), _full(fimat.shape)],
        out_specs=pl.BlockSpec((1, seq, CT), lambda c, bi: (bi, 0, c)),
        out_shape=jax.ShapeDtypeStruct((b, seq, HY_C), F32),
        compiler_params=_cparams(("parallel", "parallel")),
    )(a_arr, g_arr, gspec, d, fmat, fimat)


def _group_norm_tile(xs, g):
    lane = lax.broadcasted_iota(jnp.int32, xs.shape, 1)
    lo = lane < GROUP
    sq = xs * xs
    s_lo = jnp.sum(jnp.where(lo, sq, 0.0), axis=-1, keepdims=True)
    s_hi = jnp.sum(jnp.where(lo, 0.0, sq), axis=-1, keepdims=True)
    r = jnp.where(lo, lax.rsqrt(s_lo / GROUP + EPS), lax.rsqrt(s_hi / GROUP + EPS))
    return xs * r * g


def _ffn2_kernel(x_ref, a_ref, h_ref, hg_ref, wo_ref, g_ref, wg_ref, wu_ref, wd_ref, gf_ref, o_ref):
    tiles = []
    n_a = a_ref.shape[1] // LANES
    for s in range(D_MODEL // LANES):
        src = a_ref[:, s * LANES:(s + 1) * LANES] if s < n_a else \
            h_ref[:, (s - n_a) * LANES:(s - n_a + 1) * LANES]
        tiles.append(_group_norm_tile(src, hg_ref[:, s * LANES:(s + 1) * LANES]).astype(BF16))
    mix = jnp.concatenate(tiles, axis=1)
    x = x_ref[...] + jnp.dot(mix, wo_ref[...], preferred_element_type=F32)
    y = _swiglu_residual(x, g_ref, wg_ref, wu_ref, wd_ref)
    o_ref[...] = _rms(y, gf_ref[...])


def _ffn2(x1, a2d, h2d, hg, wo, g, wg, wu, wd, gf):
    t = x1.shape[0]
    row = lambda w: pl.BlockSpec((TM_FFN, w), lambda i: (i, 0))
    return pl.pallas_call(
        _ffn2_kernel, name="ffn2",
        grid=(t // TM_FFN,),
        in_specs=[row(D_MODEL), row(a2d.shape[1]), row(h2d.shape[1]), _full(hg.shape), _full(wo.shape),
                  _full(g.shape), _full(wg.shape), _full(wu.shape), _full(wd.shape), _full(gf.shape)],
        out_specs=row(D_MODEL),
        out_shape=jax.ShapeDtypeStruct(x1.shape, F32),
        compiler_params=_cparams(("parallel",)),
    )(x1, a2d, h2d, hg, wo, g, wg, wu, wd, gf)


def _rope_tables(seq):
    inv = 1.0 / (ROPE_THETA ** (jnp.arange(0, ROPE, 2, dtype=F32) / ROPE))
    ang = jnp.arange(seq, dtype=F32)[:, None] * inv[None, :]
    cos, sin = jnp.cos(ang), jnp.sin(ang)
    z64 = jnp.zeros((seq, NOPE), F32)
    z32 = jnp.zeros((seq, HEAD_PAD - NOPE - ROPE), F32)
    cq_t = jnp.concatenate([jnp.ones((seq, NOPE), F32), cos, cos, z32], axis=1)
    ck_t = jnp.concatenate([z64, cos, cos, z32], axis=1)
    sn_t = jnp.concatenate([z64, -sin, sin, z32], axis=1)
    return cq_t, ck_t, sn_t


def _dft_matrices(lb):
    n2 = 2 * lb
    nch = lb // RC
    two_pi = 2.0 * math.pi

    def tables(c, q, n):
        ang_c = (two_pi / (n2 // RC)) * ((c * n) % (n2 // RC)).astype(F32)
        ang_q = (two_pi / n2) * ((q * n) % n2).astype(F32)
        cc, sc, cq, sq = jnp.cos(ang_c), jnp.sin(ang_c), jnp.cos(ang_q), jnp.sin(ang_q)
        return cc * cq - sc * sq, sc * cq + cc * sq

    ar = lambda m: jnp.arange(m, dtype=jnp.int32)
    alt = lambda n: (1 - 2 * (n % 2)).astype(F32)
    c, q, n = ar(nch)[:, None, None], ar(RC)[None, :, None], ar(lb)[None, None, :]
    cos, sin = tables(c, q, n)
    f0 = (c == 0) & (q == 0)
    im_rows = jnp.where(f0, alt(n), -sin)
    fwd = jnp.stack([cos, im_rows], axis=1).reshape(n2, lb)
    fwd_nat = jnp.concatenate([cos.reshape(lb, lb), im_rows.reshape(lb, lb)], axis=0)
    n, c, q = ar(lb)[:, None, None], ar(nch)[None, :, None], ar(RC)[None, None, :]
    cos, sin = tables(c, q, n)
    f0 = (c == 0) & (q == 0)
    inv_re = jnp.where(f0, 1.0 / n2, (2.0 / n2) * cos)
    inv_im = jnp.where(f0, alt(n) / n2, (-2.0 / n2) * sin)
    inv = jnp.stack([inv_re, inv_im], axis=2).reshape(lb, n2)
    return fwd.astype(BF16), inv.astype(BF16), fwd_nat.astype(BF16)


def kernel(x, ffn1_norm_g, ffn1_w_gate, ffn1_w_up, ffn1_w_down, mix_norm_g, w_in, q_norm_g, w_uq, kv_norm_g, w_ukv, hyena_conv_w, hyena_conv_b, filt_w1, filt_b1, filt_w2, filt_b2, filt_w3, filt_freq, hyena_d, head_norm_g, w_out, ffn2_norm_g, ffn2_w_gate, ffn2_w_up, ffn2_w_down, final_norm_g):
    b, seq, d = x.shape
    t = b * seq
    lb = seq // P_BLK
    l = 0
    row = lambda v: v.reshape(1, -1)

    x1 = _ffn1(x.reshape(t, d), row(ffn1_norm_g[l]), ffn1_w_gate[l].astype(BF16),
               ffn1_w_up[l].astype(BF16), ffn1_w_down[l].astype(BF16))

    wi = w_in[l]
    o_kr = Q_RANK + KV_RANK
    half = ROPE // 2
    zc = lambda n: jnp.zeros((d, n), F32)
    kr = wi[:, o_kr:o_kr + ROPE]
    win_p = jnp.concatenate([
        wi[:, :o_kr],
        zc(NOPE), kr, zc(HEAD_PAD - NOPE - ROPE),
        zc(NOPE), kr[:, half:], kr[:, :half], zc(HEAD_PAD - NOPE - ROPE),
        wi[:, o_kr + ROPE:]], axis=1).astype(BF16)
    wq3 = w_uq[l].reshape(Q_RANK, MLA_HEADS, NOPE + ROPE)
    zq = lambda n: jnp.zeros((Q_RANK, MLA_HEADS, n), F32)
    wq_p = jnp.concatenate([wq3, zq(HEAD_PAD - NOPE - ROPE)], axis=2).reshape(Q_RANK, -1).astype(BF16)
    wqs_p = jnp.concatenate([zq(NOPE), wq3[:, :, NOPE + half:], wq3[:, :, NOPE:NOPE + half],
                             zq(HEAD_PAD - NOPE - ROPE)], axis=2).reshape(Q_RANK, -1).astype(BF16)
    wkv3 = w_ukv[l].reshape(KV_RANK, MLA_HEADS, NOPE + V_DIM)
    wk_p = jnp.concatenate([wkv3[:, :, :NOPE], jnp.zeros((KV_RANK, MLA_HEADS, HEAD_PAD - NOPE), F32)],
                           axis=2).reshape(KV_RANK, -1).astype(BF16)
    wv_p = wkv3[:, :, NOPE:].reshape(KV_RANK, -1).astype(BF16)
    cq_t, ck_t, sn_t = _rope_tables(seq)
    qt, k4, vt, u3 = _inproj(x1.reshape(b, seq, d), row(mix_norm_g[l]), win_p,
                             hyena_conv_w[l], row(hyena_conv_b[l]), row(q_norm_g[l]),
                             wq_p.T, wqs_p.T, row(kv_norm_g[l]), wk_p, wv_p.T,
                             cq_t.T, sn_t.T, ck_t, sn_t)

    a = _attention(qt, k4, vt)

    col = lambda v: v.astype(F32).reshape(-1, 1)
    bands = col(jnp.linspace(1e-4, FILTER_BANDS - 1, FILTER_BANDS, dtype=F32))
    w1 = filt_w1[l].astype(F32)
    w3d = filt_w3[l].astype(F32).reshape(FILTER_HIDDEN, HY_ORDER, 2, HY_C).transpose(2, 1, 3, 0)
    w3d = w3d.reshape(2, HY_ORDER * HY_C, FILTER_HIDDEN).astype(BF16)
    deltas = jnp.abs(jnp.linspace(MIN_DECAY, MAX_DECAY, HY_C, dtype=F32))
    dl = col(jnp.tile(deltas, HY_ORDER))
    kk, asum = _filters(bands, col(w1[0]), w1[1:1 + FILTER_BANDS].T, w1[1 + FILTER_BANDS:].T,
                        col(filt_b1[l]), filt_w2[l].astype(F32).T, col(filt_b2[l]),
                        w3d, col(filt_freq[l]), dl, seq)
    fmat, fimat, fmat_nat = _dft_matrices(lb)
    lag0 = jnp.zeros((BF16_ROWS, lb), BF16).at[0, 0].set(1.0)
    gspec = _spectra(kk, asum, jnp.concatenate([fmat_nat, lag0], axis=0), lb)

    nct = HY_C // CT
    dsk = hyena_d[l].astype(F32)
    z1 = _hyena_stage(u3, 0, u3, nct, gspec, 0, dsk[0:1], fmat, fimat, lb)
    hy = _hyena_stage(z1, 0, u3, 2 * nct, gspec, nct, dsk[1:2], fmat, fimat, lb)

    out = _ffn2(x1, a.reshape(t, -1), hy.reshape(t, -1), row(head_norm_g[l]), w_out[l].astype(BF16),
                row(ffn2_norm_g[l]), ffn2_w_gate[l].astype(BF16), ffn2_w_up[l].astype(BF16),
                ffn2_w_down[l].astype(BF16), row(final_norm_g))
    return out.reshape(b, seq, d)
```

```python
import functools
import math

import jax
import jax.numpy as jnp
from jax import lax
from jax.experimental import pallas as pl
from jax.experimental.pallas import tpu as pltpu

F32 = jnp.float32
BF16 = jnp.bfloat16

D_MODEL = 1024
MLA_HEADS = 8
NOPE = 64
ROPE = 32
V_DIM = 64
Q_RANK = 256
KV_RANK = 128
ROPE_THETA = 10000.0
HY_C = 512
FILTER_BANDS = 16
FILTER_HIDDEN = 64
HY_ORDER = 2
FFN_HIDDEN = 2816
FFN_RES = 0.5
EPS = 1e-6
GROUP = 64
DECAY_TARGET = 1e-2
MAX_DECAY = math.log(DECAY_TARGET) / 0.3
MIN_DECAY = math.log(DECAY_TARGET) / 1.5

LANES = 128
HALO = 8
BF16_ROWS = 16
HEAD_PAD = 128
VMEM_LIMIT = 56 * 1024 * 1024

TM_FFN = 512
MXU_DIM = 256
FFN_CHUNK = 6 * MXU_DIM
TS_PROJ = 1024
TQ = 512
TK = 1024
PAIRS_PER_STEP = 2
ATT_SLOTS = 2
REF_ROWS = 16
ATT_SAFE_EXP = 60.0
F32_HUGE = 3.0e38
CT = 128
P_BLK = 4
RC = 32
FWD_GROUPS = 8
TL_FILT = 1024


def _cparams(sem, vmem_limit=VMEM_LIMIT, flags=None):
    return pltpu.CompilerParams(dimension_semantics=sem, vmem_limit_bytes=vmem_limit, flags=flags)


def _rms(x, g):
    ms = jnp.mean(x * x, axis=-1, keepdims=True)
    return x * lax.rsqrt(ms + EPS) * g


def _swiglu_residual(x, g_ref, wg_ref, wu_ref, wd_ref):
    xn = _rms(x, g_ref[...]).astype(BF16)
    acc = jnp.zeros(x.shape, F32)
    for c0 in range(0, FFN_HIDDEN, FFN_CHUNK):
        c1 = min(c0 + FFN_CHUNK, FFN_HIDDEN)
        gate = jnp.dot(xn, wg_ref[:, c0:c1], preferred_element_type=F32)
        up = jnp.dot(xn, wu_ref[:, c0:c1], preferred_element_type=F32)
        h = (gate * jax.nn.sigmoid(gate) * up).astype(BF16)
        acc = acc + jnp.dot(h, wd_ref[c0:c1, :], preferred_element_type=F32)
    return x + FFN_RES * acc


def _ffn1_kernel(x_ref, g_ref, wg_ref, wu_ref, wd_ref, o_ref):
    o_ref[...] = _swiglu_residual(x_ref[...], g_ref, wg_ref, wu_ref, wd_ref)


def _full(shape):
    return pl.BlockSpec(shape, lambda *_: (0,) * len(shape))


def _ffn1(x2d, g, wg, wu, wd):
    t = x2d.shape[0]
    row = pl.BlockSpec((TM_FFN, D_MODEL), lambda i: (i, 0))
    return pl.pallas_call(
        _ffn1_kernel, name="ffn1",
        grid=(t // TM_FFN,),
        in_specs=[row, _full(g.shape), _full(wg.shape), _full(wu.shape), _full(wd.shape)],
        out_specs=row,
        out_shape=jax.ShapeDtypeStruct(x2d.shape, F32),
        compiler_params=_cparams(("parallel",)),
    )(x2d, g, wg, wu, wd)


def _inproj_kernel(x_ref, xp_ref, xn_ref, g_ref, win_ref, cw_ref, cb_ref, qg_ref, wqt_ref, wqst_ref,
                   kvg_ref, wk_ref, wvt_ref, cqt_ref, snt_ref, ck_ref, sn_ref,
                   q_ref, k_ref, v_ref, u_ref, *, scale):
    ts = x_ref.shape[1]
    i = pl.program_id(1)
    x_all = jnp.concatenate([xp_ref[0], x_ref[0], xn_ref[0]], axis=0)
    xn = _rms(x_all, g_ref[...]).astype(BF16)
    u_all = jnp.dot(xn, win_ref[...], preferred_element_type=F32)
    u = u_all[HALO:HALO + ts]
    c_q = u[:, 0:Q_RANK]
    c_kv = u[:, Q_RANK:Q_RANK + KV_RANK]
    o_pe = Q_RANK + KV_RANK
    kpe_a = u[:, o_pe:o_pe + LANES]
    kpe_b = u[:, o_pe + LANES:o_pe + 2 * LANES]

    o_hy = o_pe + 2 * LANES
    keep_prev = (i > 0).astype(F32)
    keep_next = (i < pl.num_programs(1) - 1).astype(F32)
    rows = ts + 2 * HALO
    uh = jnp.concatenate([u_all[0:HALO, o_hy:] * keep_prev, u_all[HALO:HALO + ts, o_hy:],
                          u_all[HALO + ts:, o_hy:] * keep_next], axis=0)
    prev = pltpu.roll(uh * cw_ref[0:1, :], 1, axis=0)
    nxt = pltpu.roll(uh * cw_ref[2:3, :], rows - 1, axis=0)
    conv = (cb_ref[...] + prev) + uh * cw_ref[1:2, :] + nxt
    u_ref[0] = conv[HALO:HALO + ts]

    nt = (((1,), (1,)), ((), ()))
    cqn = _rms(c_q, qg_ref[...]).astype(BF16)
    qt = lax.dot_general(wqt_ref[...], cqn, nt, preferred_element_type=F32)
    qst = lax.dot_general(wqst_ref[...], cqn, nt, preferred_element_type=F32)
    cq_t = cqt_ref[...]
    sn_tt = snt_ref[...]
    for h in range(MLA_HEADS):
        sl = slice(h * HEAD_PAD, (h + 1) * HEAD_PAD)
        q_ref[0, sl, :] = ((qt[sl] * cq_t + qst[sl] * sn_tt) * scale).astype(BF16)

    ckn = _rms(c_kv, kvg_ref[...]).astype(BF16)
    kn = jnp.dot(ckn, wk_ref[...], preferred_element_type=F32)
    k_pe = kpe_a * ck_ref[...] + kpe_b * sn_ref[...]
    for h in range(MLA_HEADS):
        sl = slice(h * HEAD_PAD, (h + 1) * HEAD_PAD)
        k_ref[0, h] = (kn[:, sl] + k_pe).astype(BF16)
    vt = lax.dot_general(wvt_ref[...], ckn, nt, preferred_element_type=F32)
    v_ref[0] = vt.reshape(MLA_HEADS // 2, 2 * V_DIM, vt.shape[1]).astype(BF16)


def _inproj(x1, g, win, cw, cb, qg, wqt, wqst, kvg, wk, wvt, cq_tt, sn_tt, ck_t, sn_t):
    b, seq, _ = x1.shape
    ts = TS_PROJ
    scale = (NOPE + ROPE) ** -0.5 * math.log2(math.e)
    hp = MLA_HEADS * HEAD_PAD
    hpt = ts // HALO
    last = seq // HALO - 1
    return pl.pallas_call(
        functools.partial(_inproj_kernel, scale=scale), name="inproj",
        grid=(b, seq // ts),
        in_specs=[pl.BlockSpec((1, ts, D_MODEL), lambda bi, i: (bi, i, 0)),
                  pl.BlockSpec((1, HALO, D_MODEL), lambda bi, i: (bi, jnp.maximum(i * hpt - 1, 0), 0)),
                  pl.BlockSpec((1, HALO, D_MODEL), lambda bi, i: (bi, jnp.minimum((i + 1) * hpt, last), 0)),
                  _full(g.shape), _full(win.shape), _full(cw.shape), _full(cb.shape),
                  _full(qg.shape), _full(wqt.shape),
                  _full(wqst.shape), _full(kvg.shape), _full(wk.shape), _full(wvt.shape),
                  pl.BlockSpec((HEAD_PAD, ts), lambda bi, i: (0, i)),
                  pl.BlockSpec((HEAD_PAD, ts), lambda bi, i: (0, i)),
                  pl.BlockSpec((ts, LANES), lambda bi, i: (i, 0)),
                  pl.BlockSpec((ts, LANES), lambda bi, i: (i, 0))],
        out_specs=[pl.BlockSpec((1, hp, ts), lambda bi, i: (bi, 0, i)),
                   pl.BlockSpec((1, MLA_HEADS, ts, HEAD_PAD), lambda bi, i: (bi, 0, i, 0)),
                   pl.BlockSpec((1, MLA_HEADS // 2, 2 * V_DIM, ts), lambda bi, i: (bi, 0, 0, i)),
                   pl.BlockSpec((1, ts, 3 * HY_C), lambda bi, i: (bi, i, 0))],
        out_shape=[jax.ShapeDtypeStruct((b, hp, seq), BF16),
                   jax.ShapeDtypeStruct((b, MLA_HEADS, seq, HEAD_PAD), BF16),
                   jax.ShapeDtypeStruct((b, MLA_HEADS // 2, 2 * V_DIM, seq), BF16),
                   jax.ShapeDtypeStruct((b, seq, 3 * HY_C), F32)],
        compiler_params=_cparams(("parallel", "parallel")),
    )(x1, x1, x1, g, win, cw, cb, qg, wqt, wqst, kvg, wk, wvt, cq_tt, sn_tt, ck_t, sn_t)


def _attn_robust(pr, q_ref, k_ref, v_ref, o_ref, s_buf, p_buf, mc_buf, al_buf, m_sc, l_sc, acc_sc):
    seq = k_ref.shape[2]
    nk = seq // TK
    nblk = 2 * nk

    def split(n):
        if isinstance(n, int):
            return n // nk, (n % nk) * TK, (n // nk) * HEAD_PAD
        h = (n >= nk).astype(jnp.int32)
        return h, pl.multiple_of((n - h * nk) * TK, TK), pl.multiple_of(h * HEAD_PAD, HEAD_PAD)

    def scores(n, slot):
        h, off, qoff = split(n)
        k = k_ref[0, 2 * pr + h, pl.ds(off, TK), :]
        s = jnp.dot(k, q_ref[0, pl.ds(2 * HEAD_PAD * pr + qoff, HEAD_PAD), :],
                    preferred_element_type=F32)
        s_buf[slot] = s
        mc_buf[slot] = jnp.max(s, axis=0, keepdims=True)

    def softmax(n, slot):
        h, _, _ = split(n)
        m_prev = m_sc[h]
        m_new = jnp.maximum(m_prev, mc_buf[slot])
        alpha = jnp.exp2(m_prev - m_new)
        p = jnp.exp2(s_buf[slot] - m_new)
        l_sc[h] = alpha * l_sc[h] + jnp.sum(p, axis=0, keepdims=True)
        p_buf[slot] = p.astype(BF16)
        al_buf[slot] = alpha
        m_sc[h] = m_new

    def values(n, slot):
        h, off, _ = split(n)
        vt = v_ref[0, pr, :, pl.ds(off, TK)]
        pv = jnp.dot(vt, p_buf[slot], preferred_element_type=F32)
        acc_sc[h] = al_buf[slot] * acc_sc[h] + pv

    m_sc[...] = jnp.full(m_sc.shape, -jnp.inf, F32)
    l_sc[...] = jnp.zeros(l_sc.shape, F32)
    acc_sc[...] = jnp.zeros(acc_sc.shape, F32)

    scores(0, 0)
    scores(1, 1)
    softmax(0, 0)

    def body(i, carry):
        n = 2 * i
        scores(n + 2, 0)
        softmax(n + 1, 1)
        values(n, 0)
        scores(n + 3, 1)
        softmax(n + 2, 0)
        values(n + 1, 1)
        return carry

    lax.fori_loop(0, (nblk - 2) // 2, body, 0)
    softmax(nblk - 1, 1)
    values(nblk - 2, 0)
    values(nblk - 1, 1)

    row = lax.broadcasted_iota(jnp.int32, acc_sc.shape[1:], 0)
    out_t = jnp.where(row < V_DIM, acc_sc[0] / l_sc[0], acc_sc[1] / l_sc[1])
    o_ref[0, :, 2 * V_DIM * pr:2 * V_DIM * (pr + 1)] = out_t.T


def _attn_kernel(q_ref, k_ref, v_ref, o_ref, *scratch):
    seq = k_ref.shape[2]
    bads = []
    for pr in range(PAIRS_PER_STEP):
        outs = []
        bad = None
        for h in range(2):
            qrow = (2 * pr + h) * HEAD_PAD
            qt = q_ref[0, qrow:qrow + HEAD_PAD, :]
            s0 = jnp.dot(k_ref[0, 2 * pr + h, 0:REF_ROWS, :], qt, preferred_element_type=F32)
            m_ref = jnp.max(s0, axis=0, keepdims=True)
            l = acc = None
            for j in range(seq // TK):
                ks = slice(j * TK, (j + 1) * TK)
                s = jnp.dot(k_ref[0, 2 * pr + h, ks, :], qt, preferred_element_type=F32)
                p = jnp.exp2(s - m_ref)
                p_sum = jnp.sum(p, axis=0, keepdims=True)
                pv = jnp.dot(v_ref[0, pr, :, ks], p.astype(BF16), preferred_element_type=F32)
                l = p_sum if l is None else l + p_sum
                acc = pv if acc is None else acc + pv
            out = acc / l
            outs.append(out)
            b_h = jnp.maximum(jnp.max(jnp.where(l < 2.0 ** ATT_SAFE_EXP, 0.0, 1.0)),
                              jnp.max(jnp.where(jnp.abs(out) < F32_HUGE, 0.0, 1.0)))
            bad = b_h if bad is None else jnp.maximum(bad, b_h)
        row = lax.broadcasted_iota(jnp.int32, outs[0].shape, 0)
        o_ref[0, :, 2 * V_DIM * pr:2 * V_DIM * (pr + 1)] = jnp.where(row < V_DIM, outs[0], outs[1]).T
        bads.append(bad)

    for pr in range(PAIRS_PER_STEP):
        pl.when(bads[pr] > 0.0)(
            functools.partial(_attn_robust, pr, q_ref, k_ref, v_ref, o_ref, *scratch))


def _attention(qt, k4, vt):
    b, _, seq = qt.shape
    pps = PAIRS_PER_STEP
    return pl.pallas_call(
        _attn_kernel, name="attn",
        grid=(b, MLA_HEADS // (2 * pps), seq // TQ),
        in_specs=[pl.BlockSpec((1, 2 * pps * HEAD_PAD, TQ), lambda bi, j, qi: (bi, j, qi)),
                  pl.BlockSpec((1, 2 * pps, seq, HEAD_PAD), lambda bi, j, qi: (bi, j, 0, 0)),
                  pl.BlockSpec((1, pps, 2 * V_DIM, seq), lambda bi, j, qi: (bi, j, 0, 0))],
        out_specs=pl.BlockSpec((1, TQ, 2 * pps * V_DIM), lambda bi, j, qi: (bi, qi, j)),
        out_shape=jax.ShapeDtypeStruct((b, seq, MLA_HEADS * V_DIM), F32),
        scratch_shapes=[pltpu.VMEM((ATT_SLOTS, TK, TQ), F32), pltpu.VMEM((ATT_SLOTS, TK, TQ), BF16),
                        pltpu.VMEM((ATT_SLOTS, 1, TQ), F32), pltpu.VMEM((ATT_SLOTS, 1, TQ), F32),
                        pltpu.VMEM((2, 1, TQ), F32), pltpu.VMEM((2, 1, TQ), F32),
                        pltpu.VMEM((2, 2 * V_DIM, TQ), F32)],
        compiler_params=_cparams(("parallel", "parallel", "parallel")),
    )(qt, k4, vt)


def _filter_kernel(bands_ref, w1t_ref, w1c_ref, w1s_ref, b1_ref, w2_ref, b2_ref, w3_ref,
                   fr_ref, dl_ref, kk_ref, asum_ref, *, seq):
    i = pl.program_id(0)
    hp = lax.Precision.HIGHEST
    l_idx = i * TL_FILT + lax.broadcasted_iota(jnp.int32, (1, TL_FILT), 1)
    pos_i = jnp.where(l_idx < seq, seq - l_idx, l_idx - seq)
    pos = pos_i.astype(F32)
    t = pos / float(max(seq - 1, 1))
    ang = (2.0 * math.pi * pos / seq) * bands_ref[...]
    pre = (t * w1t_ref[...]
           + jnp.dot(w1c_ref[...], jnp.cos(ang), precision=hp, preferred_element_type=F32)
           + jnp.dot(w1s_ref[...], -jnp.sin(ang), precision=hp, preferred_element_type=F32))
    fr = fr_ref[...]
    h = jnp.sin(fr * (pre + b1_ref[...]))
    h = jnp.sin(fr * (jnp.dot(w2_ref[...], h, precision=hp, preferred_element_type=F32) + b2_ref[...]))
    h = jnp.dot(w3_ref[0], h.astype(BF16), preferred_element_type=F32)
    decay = jnp.exp(-t * dl_ref[...])
    kk = jnp.where(l_idx == 0, 0.0, h * decay)
    kk_ref[...] = kk

    @pl.when(i == 0)
    def _():
        asum_ref[...] = jnp.zeros(asum_ref.shape, F32)

    asum_ref[...] += jnp.sum(jnp.abs(kk), axis=1, keepdims=True)


def _filters(bands, w1t, w1c, w1s, b1, w2, b2, w3d, fr, dl, seq):
    nl = 2 * seq // TL_FILT
    half = seq // TL_FILT
    ncol = HY_ORDER * HY_C
    return pl.pallas_call(
        functools.partial(_filter_kernel, seq=seq), name="hyfilt",
        grid=(nl,),
        in_specs=[_full(bands.shape), _full(w1t.shape), _full(w1c.shape), _full(w1s.shape),
                  _full(b1.shape), _full(w2.shape), _full(b2.shape),
                  pl.BlockSpec((1, ncol, FILTER_HIDDEN), lambda i: (jnp.where(i < half, 1, 0), 0, 0)),
                  _full(fr.shape), _full(dl.shape)],
        out_specs=[pl.BlockSpec((ncol, TL_FILT), lambda i: (0, i)),
                   pl.BlockSpec((ncol, 1), lambda i: (0, 0))],
        out_shape=[jax.ShapeDtypeStruct((ncol, 2 * seq), F32),
                   jax.ShapeDtypeStruct((ncol, 1), F32)],
        compiler_params=_cparams(("arbitrary",)),
    )(bands, w1t, w1c, w1s, b1, w2, b2, w3d, fr, dl)


def _spec_kernel(kk_ref, asum_ref, f_ref, g_ref, *, lb):
    ct = kk_ref.shape[0]
    nhalf = kk_ref.shape[1] // lb
    kkn = (kk_ref[...] / asum_ref[...]).astype(BF16)
    x8 = jnp.concatenate([kkn[:, m * lb:(m + 1) * lb] for m in range(nhalf)], axis=0)
    hh = lax.dot_general(f_ref[...], x8, (((1,), (1,)), ((), ())),
                         preferred_element_type=F32)
    row = lax.broadcasted_iota(jnp.int32, (2 * lb, 1), 0)
    sgn = (1 - 2 * (row & 1)).astype(F32)
    low = (((row // RC) & 1) == 0) | (row == RC)
    for dd in range(nhalf - 1):
        h_neg = hh[0:2 * lb, dd * ct:(dd + 1) * ct]
        h_pos = hh[0:2 * lb, (dd + 1) * ct:(dd + 2) * ct]
        h0 = hh[2 * lb:2 * lb + 1, dd * ct:(dd + 1) * ct]
        g_ref[dd] = h_pos + sgn * (h_neg - jnp.where(low, h0, 0.0))


def _spectra(kk, asum, fmat_ext, lb):
    ncol = kk.shape[0]
    nd = 2 * P_BLK - 1
    return pl.pallas_call(
        functools.partial(_spec_kernel, lb=lb), name="hyspec",
        grid=(ncol // CT,),
        in_specs=[pl.BlockSpec((CT, kk.shape[1]), lambda c: (c, 0)),
                  pl.BlockSpec((CT, 1), lambda c: (c, 0)),
                  _full(fmat_ext.shape)],
        out_specs=pl.BlockSpec((nd, 2 * lb, CT), lambda c: (0, 0, c)),
        out_shape=jax.ShapeDtypeStruct((nd, 2 * lb, ncol), F32),
        compiler_params=_cparams(("parallel",)),
    )(kk, asum, fmat_ext)


def _hyena_kernel(a_ref, gt_ref, g_ref, d_ref, f_ref, fi_ref, o_ref, *, lb):
    ct = a_ref.shape[2]
    x4 = jnp.concatenate([a_ref[0, j * lb:(j + 1) * lb, :].astype(BF16) for j in range(P_BLK)],
                         axis=1)
    grp = 2 * lb // FWD_GROUPS
    us = [jnp.dot(f_ref[q * grp:(q + 1) * grp, :], x4, preferred_element_type=F32)
          for q in range(FWD_GROUPS)]
    first = lax.broadcasted_iota(jnp.int32, (RC, 1), 0) == 0
    y_rows = []
    for r in range(lb // RC):
        re0, im0 = 2 * RC * r, 2 * RC * r + RC
        u = us[re0 // grp]
        ure = [u[re0 % grp:re0 % grp + RC, j * ct:(j + 1) * ct] for j in range(P_BLK)]
        uim = [u[im0 % grp:im0 % grp + RC, j * ct:(j + 1) * ct] for j in range(P_BLK)]
        y_re, y_im = [], []
        for i in range(P_BLK):
            yre = yim = None
            dc = ny = None
            for j in range(P_BLK):
                dd = i - j + P_BLK - 1
                gre = g_ref[dd, re0:re0 + RC, :]
                gim = g_ref[dd, im0:im0 + RC, :]
                rr = ure[j] * gre
                ii = uim[j] * gim
                tre = rr - ii
                tim = ure[j] * gim + uim[j] * gre
                yre = tre if yre is None else yre + tre
                yim = tim if yim is None else yim + tim
                if r == 0:
                    dc = rr if dc is None else dc + rr
                    ny = ii if ny is None else ny + ii
            if r == 0:
                yre = jnp.where(first, dc, yre)
                yim = jnp.where(first, ny, yim)
            y_re.append(yre.astype(BF16))
            y_im.append(yim.astype(BF16))
        y_rows.append(jnp.concatenate(y_re, axis=1))
        y_rows.append(jnp.concatenate(y_im, axis=1))
    y = jnp.concatenate(y_rows, axis=0)
    y4 = jnp.dot(fi_ref[...], y, preferred_element_type=F32)
    for i in range(P_BLK):
        rs = slice(i * lb, (i + 1) * lb)
        o_ref[0, rs, :] = (y4[:, i * ct:(i + 1) * ct] + a_ref[0, rs, :] * d_ref[...]) * gt_ref[0, rs, :]


def _hyena_stage(a_arr, a_blk0, g_arr, g_blk0, gspec, g_blk0_spec, d, fmat, fimat, lb):
    b, seq, _ = a_arr.shape
    nct = HY_C // CT
    nd = 2 * P_BLK - 1
    return pl.pallas_call(
        functools.partial(_hyena_kernel, lb=lb), name="hyena",
        grid=(nct, b),
        in_specs=[pl.BlockSpec((1, seq, CT), lambda c, bi: (bi, 0, a_blk0 + c)),
                  pl.BlockSpec((1, seq, CT), lambda c, bi: (bi, 0, g_blk0 + c)),
                  pl.BlockSpec((nd, 2 * lb, CT), lambda c, bi: (0, 0, g_blk0_spec + c)),
                  pl.BlockSpec((1, CT), lambda c, bi: (0, c)),
                  _full(fmat.shape), _full(fimat.shape)],
        out_specs=pl.BlockSpec((1, seq, CT), lambda c, bi: (bi, 0, c)),
        out_shape=jax.ShapeDtypeStruct((b, seq, HY_C), F32),
        compiler_params=_cparams(("parallel", "parallel")),
    )(a_arr, g_arr, gspec, d, fmat, fimat)


def _group_norm_tile(xs, g):
    lane = lax.broadcasted_iota(jnp.int32, xs.shape, 1)
    lo = lane < GROUP
    sq = xs * xs
    s_lo = jnp.sum(jnp.where(lo, sq, 0.0), axis=-1, keepdims=True)
    s_hi = jnp.sum(jnp.where(lo, 0.0, sq), axis=-1, keepdims=True)
    r = jnp.where(lo, lax.rsqrt(s_lo / GROUP + EPS), lax.rsqrt(s_hi / GROUP + EPS))
    return xs * r * g


def _ffn2_kernel(x_ref, a_ref, h_ref, hg_ref, wo_ref, g_ref, wg_ref, wu_ref, wd_ref, gf_ref, o_ref):
    tiles = []
    n_a = a_ref.shape[1] // LANES
    for s in range(D_MODEL // LANES):
        src = a_ref[:, s * LANES:(s + 1) * LANES] if s < n_a else \
            h_ref[:, (s - n_a) * LANES:(s - n_a + 1) * LANES]
        tiles.append(_group_norm_tile(src, hg_ref[:, s * LANES:(s + 1) * LANES]).astype(BF16))
    mix = jnp.concatenate(tiles, axis=1)
    x = x_ref[...] + jnp.dot(mix, wo_ref[...], preferred_element_type=F32)
    y = _swiglu_residual(x, g_ref, wg_ref, wu_ref, wd_ref)
    o_ref[...] = _rms(y, gf_ref[...])


def _ffn2(x1, a2d, h2d, hg, wo, g, wg, wu, wd, gf):
    t = x1.shape[0]
    row = lambda w: pl.BlockSpec((TM_FFN, w), lambda i: (i, 0))
    return pl.pallas_call(
        _ffn2_kernel, name="ffn2",
        grid=(t // TM_FFN,),
        in_specs=[row(D_MODEL), row(a2d.shape[1]), row(h2d.shape[1]), _full(hg.shape), _full(wo.shape),
                  _full(g.shape), _full(wg.shape), _full(wu.shape), _full(wd.shape), _full(gf.shape)],
        out_specs=row(D_MODEL),
        out_shape=jax.ShapeDtypeStruct(x1.shape, F32),
        compiler_params=_cparams(("parallel",)),
    )(x1, a2d, h2d, hg, wo, g, wg, wu, wd, gf)


def _rope_tables(seq):
    inv = 1.0 / (ROPE_THETA ** (jnp.arange(0, ROPE, 2, dtype=F32) / ROPE))
    ang = jnp.arange(seq, dtype=F32)[:, None] * inv[None, :]
    cos, sin = jnp.cos(ang), jnp.sin(ang)
    z64 = jnp.zeros((seq, NOPE), F32)
    z32 = jnp.zeros((seq, HEAD_PAD - NOPE - ROPE), F32)
    cq_t = jnp.concatenate([jnp.ones((seq, NOPE), F32), cos, cos, z32], axis=1)
    ck_t = jnp.concatenate([z64, cos, cos, z32], axis=1)
    sn_t = jnp.concatenate([z64, -sin, sin, z32], axis=1)
    return cq_t, ck_t, sn_t


def _dft_matrices(lb):
    n2 = 2 * lb
    nch = lb // RC
    two_pi = 2.0 * math.pi

    def tables(c, q, n):
        ang_c = (two_pi / (n2 // RC)) * ((c * n) % (n2 // RC)).astype(F32)
        ang_q = (two_pi / n2) * ((q * n) % n2).astype(F32)
        cc, sc, cq, sq = jnp.cos(ang_c), jnp.sin(ang_c), jnp.cos(ang_q), jnp.sin(ang_q)
        return cc * cq - sc * sq, sc * cq + cc * sq

    ar = lambda m: jnp.arange(m, dtype=jnp.int32)
    alt = lambda n: (1 - 2 * (n % 2)).astype(F32)
    c, q, n = ar(nch)[:, None, None], ar(RC)[None, :, None], ar(lb)[None, None, :]
    cos, sin = tables(c, q, n)
    f0 = (c == 0) & (q == 0)
    fwd = jnp.stack([cos, jnp.where(f0, alt(n), -sin)], axis=1).reshape(n2, lb)
    n, c, q = ar(lb)[:, None, None], ar(nch)[None, :, None], ar(RC)[None, None, :]
    cos, sin = tables(c, q, n)
    f0 = (c == 0) & (q == 0)
    inv_re = jnp.where(f0, 1.0 / n2, (2.0 / n2) * cos)
    inv_im = jnp.where(f0, alt(n) / n2, (-2.0 / n2) * sin)
    inv = jnp.stack([inv_re, inv_im], axis=2).reshape(lb, n2)
    return fwd.astype(BF16), inv.astype(BF16)


def kernel(x, ffn1_norm_g, ffn1_w_gate, ffn1_w_up, ffn1_w_down, mix_norm_g, w_in, q_norm_g, w_uq, kv_norm_g, w_ukv, hyena_conv_w, hyena_conv_b, filt_w1, filt_b1, filt_w2, filt_b2, filt_w3, filt_freq, hyena_d, head_norm_g, w_out, ffn2_norm_g, ffn2_w_gate, ffn2_w_up, ffn2_w_down, final_norm_g):
    b, seq, d = x.shape
    t = b * seq
    lb = seq // P_BLK
    l = 0
    row = lambda v: v.reshape(1, -1)

    x1 = _ffn1(x.reshape(t, d), row(ffn1_norm_g[l]), ffn1_w_gate[l].astype(BF16),
               ffn1_w_up[l].astype(BF16), ffn1_w_down[l].astype(BF16))

    wi = w_in[l]
    o_kr = Q_RANK + KV_RANK
    half = ROPE // 2
    zc = lambda n: jnp.zeros((d, n), F32)
    kr = wi[:, o_kr:o_kr + ROPE]
    win_p = jnp.concatenate([
        wi[:, :o_kr],
        zc(NOPE), kr, zc(HEAD_PAD - NOPE - ROPE),
        zc(NOPE), kr[:, half:], kr[:, :half], zc(HEAD_PAD - NOPE - ROPE),
        wi[:, o_kr + ROPE:]], axis=1).astype(BF16)
    wq3 = w_uq[l].reshape(Q_RANK, MLA_HEADS, NOPE + ROPE)
    zq = lambda n: jnp.zeros((Q_RANK, MLA_HEADS, n), F32)
    wq_p = jnp.concatenate([wq3, zq(HEAD_PAD - NOPE - ROPE)], axis=2).reshape(Q_RANK, -1).astype(BF16)
    wqs_p = jnp.concatenate([zq(NOPE), wq3[:, :, NOPE + half:], wq3[:, :, NOPE:NOPE + half],
                             zq(HEAD_PAD - NOPE - ROPE)], axis=2).reshape(Q_RANK, -1).astype(BF16)
    wkv3 = w_ukv[l].reshape(KV_RANK, MLA_HEADS, NOPE + V_DIM)
    wk_p = jnp.concatenate([wkv3[:, :, :NOPE], jnp.zeros((KV_RANK, MLA_HEADS, HEAD_PAD - NOPE), F32)],
                           axis=2).reshape(KV_RANK, -1).astype(BF16)
    wv_p = wkv3[:, :, NOPE:].reshape(KV_RANK, -1).astype(BF16)
    cq_t, ck_t, sn_t = _rope_tables(seq)
    qt, k4, vt, u3 = _inproj(x1.reshape(b, seq, d), row(mix_norm_g[l]), win_p,
                             hyena_conv_w[l], row(hyena_conv_b[l]), row(q_norm_g[l]),
                             wq_p.T, wqs_p.T, row(kv_norm_g[l]), wk_p, wv_p.T,
                             cq_t.T, sn_t.T, ck_t, sn_t)

    a = _attention(qt, k4, vt)

    col = lambda v: v.astype(F32).reshape(-1, 1)
    bands = col(jnp.linspace(1e-4, FILTER_BANDS - 1, FILTER_BANDS, dtype=F32))
    w1 = filt_w1[l].astype(F32)
    w3d = filt_w3[l].astype(F32).reshape(FILTER_HIDDEN, HY_ORDER, 2, HY_C).transpose(2, 1, 3, 0)
    w3d = w3d.reshape(2, HY_ORDER * HY_C, FILTER_HIDDEN).astype(BF16)
    deltas = jnp.abs(jnp.linspace(MIN_DECAY, MAX_DECAY, HY_C, dtype=F32))
    dl = col(jnp.tile(deltas, HY_ORDER))
    kk, asum = _filters(bands, col(w1[0]), w1[1:1 + FILTER_BANDS].T, w1[1 + FILTER_BANDS:].T,
                        col(filt_b1[l]), filt_w2[l].astype(F32).T, col(filt_b2[l]),
                        w3d, col(filt_freq[l]), dl, seq)
    fmat, fimat = _dft_matrices(lb)
    lag0 = jnp.zeros((BF16_ROWS, lb), BF16).at[0, 0].set(1.0)
    gspec = _spectra(kk, asum, jnp.concatenate([fmat, lag0], axis=0), lb)

    nct = HY_C // CT
    dsk = hyena_d[l].astype(F32)
    z1 = _hyena_stage(u3, 0, u3, nct, gspec, 0, dsk[0:1], fmat, fimat, lb)
    hy = _hyena_stage(z1, 0, u3, 2 * nct, gspec, nct, dsk[1:2], fmat, fimat, lb)

    out = _ffn2(x1, a.reshape(t, -1), hy.reshape(t, -1), row(head_norm_g[l]), w_out[l].astype(BF16),
                row(ffn2_norm_g[l]), ffn2_w_gate[l].astype(BF16), ffn2_w_up[l].astype(BF16),
                ffn2_w_down[l].astype(BF16), row(final_norm_g))
    return out.reshape(b, seq, d)
```

```python
import functools
import math

import jax
import jax.numpy as jnp
from jax import lax
from jax.experimental import pallas as pl
from jax.experimental.pallas import tpu as pltpu

F32 = jnp.float32
BF16 = jnp.bfloat16

D_MODEL = 1024
MLA_HEADS = 8
NOPE = 64
ROPE = 32
V_DIM = 64
Q_RANK = 256
KV_RANK = 128
ROPE_THETA = 10000.0
HY_C = 512
FILTER_BANDS = 16
FILTER_HIDDEN = 64
HY_ORDER = 2
FFN_HIDDEN = 2816
FFN_RES = 0.5
EPS = 1e-6
GROUP = 64
DECAY_TARGET = 1e-2
MAX_DECAY = math.log(DECAY_TARGET) / 0.3
MIN_DECAY = math.log(DECAY_TARGET) / 1.5

LANES = 128
HALO = 8
BF16_ROWS = 16
HEAD_PAD = 128
VMEM_LIMIT = 56 * 1024 * 1024

TM_FFN = 512
MXU_DIM = 256
FFN_CHUNK = 6 * MXU_DIM
TS_PROJ = 1024
TQ = 512
TK = 1024
PAIRS_PER_STEP = 2
ATT_SLOTS = 2
REF_ROWS = 16
ATT_SAFE_EXP = 60.0
F32_HUGE = 3.0e38
CT = 128
P_BLK = 4
RC = 32
FWD_GROUPS = 8
TL_FILT = 1024


def _cparams(sem, vmem_limit=VMEM_LIMIT, flags=None):
    return pltpu.CompilerParams(dimension_semantics=sem, vmem_limit_bytes=vmem_limit, flags=flags)


def _rms(x, g):
    ms = jnp.mean(x * x, axis=-1, keepdims=True)
    return x * lax.rsqrt(ms + EPS) * g


def _swiglu_residual(x, xn, wg_ref, wu_ref, wd_ref):
    acc = jnp.zeros(x.shape, F32)
    for c0 in range(0, FFN_HIDDEN, FFN_CHUNK):
        c1 = min(c0 + FFN_CHUNK, FFN_HIDDEN)
        gate = jnp.dot(xn, wg_ref[:, c0:c1], preferred_element_type=F32)
        up = jnp.dot(xn, wu_ref[:, c0:c1], preferred_element_type=F32)
        h = (gate * jax.nn.sigmoid(gate) * up).astype(BF16)
        acc = acc + jnp.dot(h, wd_ref[c0:c1, :], preferred_element_type=F32)
    return x + FFN_RES * acc


def _next_tile(n):
    return lambda i: (jnp.minimum(i + 1, n - 1), 0)


def _ffn1_kernel(x_ref, g_ref, wg_ref, wu_ref, wd_ref, o_ref):
    x = x_ref[...]
    o_ref[...] = _swiglu_residual(x, _rms(x, g_ref[...]).astype(BF16), wg_ref, wu_ref, wd_ref)


def _full(shape):
    return pl.BlockSpec(shape, lambda *_: (0,) * len(shape))


def _ffn1(x2d, g, wg, wu, wd):
    t = x2d.shape[0]
    row = pl.BlockSpec((TM_FFN, D_MODEL), lambda i: (i, 0))
    return pl.pallas_call(
        _ffn1_kernel, name="ffn1",
        grid=(t // TM_FFN,),
        in_specs=[row, _full(g.shape), _full(wg.shape), _full(wu.shape), _full(wd.shape)],
        out_specs=row,
        out_shape=jax.ShapeDtypeStruct(x2d.shape, F32),
        compiler_params=_cparams(("parallel",)),
    )(x2d, g, wg, wu, wd)


def _inproj_kernel(x_ref, xp_ref, xn_ref, g_ref, win_ref, cw_ref, cb_ref, qg_ref, wqt_ref, wqst_ref,
                   kvg_ref, wk_ref, wvt_ref, cqt_ref, snt_ref, ck_ref, sn_ref,
                   q_ref, k_ref, v_ref, u_ref, *, scale):
    ts = x_ref.shape[1]
    i = pl.program_id(1)
    x_all = jnp.concatenate([xp_ref[0], x_ref[0], xn_ref[0]], axis=0)
    xn = _rms(x_all, g_ref[...]).astype(BF16)
    u_all = jnp.dot(xn, win_ref[...], preferred_element_type=F32)
    u = u_all[HALO:HALO + ts]
    c_q = u[:, 0:Q_RANK]
    c_kv = u[:, Q_RANK:Q_RANK + KV_RANK]
    o_pe = Q_RANK + KV_RANK
    kpe_a = u[:, o_pe:o_pe + LANES]
    kpe_b = u[:, o_pe + LANES:o_pe + 2 * LANES]

    o_hy = o_pe + 2 * LANES
    keep_prev = (i > 0).astype(F32)
    keep_next = (i < pl.num_programs(1) - 1).astype(F32)
    rows = ts + 2 * HALO
    uh = jnp.concatenate([u_all[0:HALO, o_hy:] * keep_prev, u_all[HALO:HALO + ts, o_hy:],
                          u_all[HALO + ts:, o_hy:] * keep_next], axis=0)
    prev = pltpu.roll(uh * cw_ref[0:1, :], 1, axis=0)
    nxt = pltpu.roll(uh * cw_ref[2:3, :], rows - 1, axis=0)
    conv = (cb_ref[...] + prev) + uh * cw_ref[1:2, :] + nxt
    u_ref[0] = conv[HALO:HALO + ts]

    nt = (((1,), (1,)), ((), ()))
    cqn = _rms(c_q, qg_ref[...]).astype(BF16)
    qt = lax.dot_general(wqt_ref[...], cqn, nt, preferred_element_type=F32)
    qst = lax.dot_general(wqst_ref[...], cqn, nt, preferred_element_type=F32)
    cq_t = cqt_ref[...]
    sn_tt = snt_ref[...]
    for h in range(MLA_HEADS):
        sl = slice(h * HEAD_PAD, (h + 1) * HEAD_PAD)
        q_ref[0, sl, :] = ((qt[sl] * cq_t + qst[sl] * sn_tt) * scale).astype(BF16)

    ckn = _rms(c_kv, kvg_ref[...]).astype(BF16)
    kn = jnp.dot(ckn, wk_ref[...], preferred_element_type=F32)
    k_pe = kpe_a * ck_ref[...] + kpe_b * sn_ref[...]
    for h in range(MLA_HEADS):
        sl = slice(h * HEAD_PAD, (h + 1) * HEAD_PAD)
        k_ref[0, h] = (kn[:, sl] + k_pe).astype(BF16)
    vt = lax.dot_general(wvt_ref[...], ckn, nt, preferred_element_type=F32)
    v_ref[0] = vt.reshape(MLA_HEADS // 2, 2 * V_DIM, vt.shape[1]).astype(BF16)


def _inproj(x1, g, win, cw, cb, qg, wqt, wqst, kvg, wk, wvt, cq_tt, sn_tt, ck_t, sn_t):
    b, seq, _ = x1.shape
    ts = TS_PROJ
    scale = (NOPE + ROPE) ** -0.5 * math.log2(math.e)
    hp = MLA_HEADS * HEAD_PAD
    hpt = ts // HALO
    last = seq // HALO - 1
    return pl.pallas_call(
        functools.partial(_inproj_kernel, scale=scale), name="inproj",
        grid=(b, seq // ts),
        in_specs=[pl.BlockSpec((1, ts, D_MODEL), lambda bi, i: (bi, i, 0)),
                  pl.BlockSpec((1, HALO, D_MODEL), lambda bi, i: (bi, jnp.maximum(i * hpt - 1, 0), 0)),
                  pl.BlockSpec((1, HALO, D_MODEL), lambda bi, i: (bi, jnp.minimum((i + 1) * hpt, last), 0)),
                  _full(g.shape), _full(win.shape), _full(cw.shape), _full(cb.shape),
                  _full(qg.shape), _full(wqt.shape),
                  _full(wqst.shape), _full(kvg.shape), _full(wk.shape), _full(wvt.shape),
                  pl.BlockSpec((HEAD_PAD, ts), lambda bi, i: (0, i)),
                  pl.BlockSpec((HEAD_PAD, ts), lambda bi, i: (0, i)),
                  pl.BlockSpec((ts, LANES), lambda bi, i: (i, 0)),
                  pl.BlockSpec((ts, LANES), lambda bi, i: (i, 0))],
        out_specs=[pl.BlockSpec((1, hp, ts), lambda bi, i: (bi, 0, i)),
                   pl.BlockSpec((1, MLA_HEADS, ts, HEAD_PAD), lambda bi, i: (bi, 0, i, 0)),
                   pl.BlockSpec((1, MLA_HEADS // 2, 2 * V_DIM, ts), lambda bi, i: (bi, 0, 0, i)),
                   pl.BlockSpec((1, ts, 3 * HY_C), lambda bi, i: (bi, i, 0))],
        out_shape=[jax.ShapeDtypeStruct((b, hp, seq), BF16),
                   jax.ShapeDtypeStruct((b, MLA_HEADS, seq, HEAD_PAD), BF16),
                   jax.ShapeDtypeStruct((b, MLA_HEADS // 2, 2 * V_DIM, seq), BF16),
                   jax.ShapeDtypeStruct((b, seq, 3 * HY_C), F32)],
        compiler_params=_cparams(("parallel", "parallel")),
    )(x1, x1, x1, g, win, cw, cb, qg, wqt, wqst, kvg, wk, wvt, cq_tt, sn_tt, ck_t, sn_t)


def _attn_robust(pr, q_ref, k_ref, v_ref, o_ref, s_buf, p_buf, mc_buf, al_buf, m_sc, l_sc, acc_sc):
    seq = k_ref.shape[2]
    nk = seq // TK
    nblk = 2 * nk

    def split(n):
        if isinstance(n, int):
            return n // nk, (n % nk) * TK, (n // nk) * HEAD_PAD
        h = (n >= nk).astype(jnp.int32)
        return h, pl.multiple_of((n - h * nk) * TK, TK), pl.multiple_of(h * HEAD_PAD, HEAD_PAD)

    def scores(n, slot):
        h, off, qoff = split(n)
        k = k_ref[0, 2 * pr + h, pl.ds(off, TK), :]
        s = jnp.dot(k, q_ref[0, pl.ds(2 * HEAD_PAD * pr + qoff, HEAD_PAD), :],
                    preferred_element_type=F32)
        s_buf[slot] = s
        mc_buf[slot] = jnp.max(s, axis=0, keepdims=True)

    def softmax(n, slot):
        h, _, _ = split(n)
        m_prev = m_sc[h]
        m_new = jnp.maximum(m_prev, mc_buf[slot])
        alpha = jnp.exp2(m_prev - m_new)
        p = jnp.exp2(s_buf[slot] - m_new)
        l_sc[h] = alpha * l_sc[h] + jnp.sum(p, axis=0, keepdims=True)
        p_buf[slot] = p.astype(BF16)
        al_buf[slot] = alpha
        m_sc[h] = m_new

    def values(n, slot):
        h, off, _ = split(n)
        vt = v_ref[0, pr, :, pl.ds(off, TK)]
        pv = jnp.dot(vt, p_buf[slot], preferred_element_type=F32)
        acc_sc[h] = al_buf[slot] * acc_sc[h] + pv

    m_sc[...] = jnp.full(m_sc.shape, -jnp.inf, F32)
    l_sc[...] = jnp.zeros(l_sc.shape, F32)
    acc_sc[...] = jnp.zeros(acc_sc.shape, F32)

    scores(0, 0)
    scores(1, 1)
    softmax(0, 0)

    def body(i, carry):
        n = 2 * i
        scores(n + 2, 0)
        softmax(n + 1, 1)
        values(n, 0)
        scores(n + 3, 1)
        softmax(n + 2, 0)
        values(n + 1, 1)
        return carry

    lax.fori_loop(0, (nblk - 2) // 2, body, 0)
    softmax(nblk - 1, 1)
    values(nblk - 2, 0)
    values(nblk - 1, 1)

    row = lax.broadcasted_iota(jnp.int32, acc_sc.shape[1:], 0)
    out_t = jnp.where(row < V_DIM, acc_sc[0] / l_sc[0], acc_sc[1] / l_sc[1])
    o_ref[0, :, 2 * V_DIM * pr:2 * V_DIM * (pr + 1)] = out_t.T


def _attn_kernel(q_ref, k_ref, v_ref, o_ref, *scratch):
    seq = k_ref.shape[2]
    bads = []
    for pr in range(PAIRS_PER_STEP):
        outs = []
        bad = None
        for h in range(2):
            qrow = (2 * pr + h) * HEAD_PAD
            qt = q_ref[0, qrow:qrow + HEAD_PAD, :]
            s0 = jnp.dot(k_ref[0, 2 * pr + h, 0:REF_ROWS, :], qt, preferred_element_type=F32)
            m_ref = jnp.max(s0, axis=0, keepdims=True)
            l = acc = None
            for j in range(seq // TK):
                ks = slice(j * TK, (j + 1) * TK)
                s = jnp.dot(k_ref[0, 2 * pr + h, ks, :], qt, preferred_element_type=F32)
                p = jnp.exp2(s - m_ref)
                p_sum = jnp.sum(p, axis=0, keepdims=True)
                pv = jnp.dot(v_ref[0, pr, :, ks], p.astype(BF16), preferred_element_type=F32)
                l = p_sum if l is None else l + p_sum
                acc = pv if acc is None else acc + pv
            out = acc / l
            outs.append(out)
            b_h = jnp.maximum(jnp.max(jnp.where(l < 2.0 ** ATT_SAFE_EXP, 0.0, 1.0)),
                              jnp.max(jnp.where(jnp.abs(out) < F32_HUGE, 0.0, 1.0)))
            bad = b_h if bad is None else jnp.maximum(bad, b_h)
        row = lax.broadcasted_iota(jnp.int32, outs[0].shape, 0)
        o_ref[0, :, 2 * V_DIM * pr:2 * V_DIM * (pr + 1)] = jnp.where(row < V_DIM, outs[0], outs[1]).T
        bads.append(bad)

    for pr in range(PAIRS_PER_STEP):
        pl.when(bads[pr] > 0.0)(
            functools.partial(_attn_robust, pr, q_ref, k_ref, v_ref, o_ref, *scratch))


def _attention(qt, k4, vt):
    b, _, seq = qt.shape
    pps = PAIRS_PER_STEP
    return pl.pallas_call(
        _attn_kernel, name="attn",
        grid=(b, MLA_HEADS // (2 * pps), seq // TQ),
        in_specs=[pl.BlockSpec((1, 2 * pps * HEAD_PAD, TQ), lambda bi, j, qi: (bi, j, qi)),
                  pl.BlockSpec((1, 2 * pps, seq, HEAD_PAD), lambda bi, j, qi: (bi, j, 0, 0)),
                  pl.BlockSpec((1, pps, 2 * V_DIM, seq), lambda bi, j, qi: (bi, j, 0, 0))],
        out_specs=pl.BlockSpec((1, TQ, 2 * pps * V_DIM), lambda bi, j, qi: (bi, qi, j)),
        out_shape=jax.ShapeDtypeStruct((b, seq, MLA_HEADS * V_DIM), F32),
        scratch_shapes=[pltpu.VMEM((ATT_SLOTS, TK, TQ), F32), pltpu.VMEM((ATT_SLOTS, TK, TQ), BF16),
                        pltpu.VMEM((ATT_SLOTS, 1, TQ), F32), pltpu.VMEM((ATT_SLOTS, 1, TQ), F32),
                        pltpu.VMEM((2, 1, TQ), F32), pltpu.VMEM((2, 1, TQ), F32),
                        pltpu.VMEM((2, 2 * V_DIM, TQ), F32)],
        compiler_params=_cparams(("parallel", "parallel", "parallel")),
    )(qt, k4, vt)


def _filter_kernel(bands_ref, w1t_ref, w1c_ref, w1s_ref, b1_ref, w2_ref, b2_ref, w3_ref,
                   fr_ref, dl_ref, kk_ref, asum_ref, *, seq):
    i = pl.program_id(0)
    hp = lax.Precision.HIGHEST
    l_idx = i * TL_FILT + lax.broadcasted_iota(jnp.int32, (1, TL_FILT), 1)
    pos_i = jnp.where(l_idx < seq, seq - l_idx, l_idx - seq)
    pos = pos_i.astype(F32)
    t = pos / float(max(seq - 1, 1))
    ang = (2.0 * math.pi * pos / seq) * bands_ref[...]
    pre = (t * w1t_ref[...]
           + jnp.dot(w1c_ref[...], jnp.cos(ang), precision=hp, preferred_element_type=F32)
           + jnp.dot(w1s_ref[...], -jnp.sin(ang), precision=hp, preferred_element_type=F32))
    fr = fr_ref[...]
    h = jnp.sin(fr * (pre + b1_ref[...]))
    h = jnp.sin(fr * (jnp.dot(w2_ref[...], h, precision=hp, preferred_element_type=F32) + b2_ref[...]))
    h = jnp.dot(w3_ref[0], h.astype(BF16), preferred_element_type=F32)
    decay = jnp.exp(-t * dl_ref[...])
    kk = jnp.where(l_idx == 0, 0.0, h * decay)
    kk_ref[...] = kk

    @pl.when(i == 0)
    def _():
        asum_ref[...] = jnp.zeros(asum_ref.shape, F32)

    asum_ref[...] += jnp.sum(jnp.abs(kk), axis=1, keepdims=True)


def _filters(bands, w1t, w1c, w1s, b1, w2, b2, w3d, fr, dl, seq):
    nl = 2 * seq // TL_FILT
    half = seq // TL_FILT
    ncol = HY_ORDER * HY_C
    return pl.pallas_call(
        functools.partial(_filter_kernel, seq=seq), name="hyfilt",
        grid=(nl,),
        in_specs=[_full(bands.shape), _full(w1t.shape), _full(w1c.shape), _full(w1s.shape),
                  _full(b1.shape), _full(w2.shape), _full(b2.shape),
                  pl.BlockSpec((1, ncol, FILTER_HIDDEN), lambda i: (jnp.where(i < half, 1, 0), 0, 0)),
                  _full(fr.shape), _full(dl.shape)],
        out_specs=[pl.BlockSpec((ncol, TL_FILT), lambda i: (0, i)),
                   pl.BlockSpec((ncol, 1), lambda i: (0, 0))],
        out_shape=[jax.ShapeDtypeStruct((ncol, 2 * seq), F32),
                   jax.ShapeDtypeStruct((ncol, 1), F32)],
        compiler_params=_cparams(("arbitrary",)),
    )(bands, w1t, w1c, w1s, b1, w2, b2, w3d, fr, dl)


def _spec_kernel(kk_ref, asum_ref, f_ref, g_ref, *, lb):
    ct = kk_ref.shape[0]
    nhalf = kk_ref.shape[1] // lb
    kkn = (kk_ref[...] / asum_ref[...]).astype(BF16)
    x8 = jnp.concatenate([kkn[:, m * lb:(m + 1) * lb] for m in range(nhalf)], axis=0)
    hh = lax.dot_general(f_ref[...], x8, (((1,), (1,)), ((), ())),
                         preferred_element_type=F32)
    row = lax.broadcasted_iota(jnp.int32, (2 * lb, 1), 0)
    sgn = (1 - 2 * (row & 1)).astype(F32)
    low = (((row // RC) & 1) == 0) | (row == RC)
    for dd in range(nhalf - 1):
        h_neg = hh[0:2 * lb, dd * ct:(dd + 1) * ct]
        h_pos = hh[0:2 * lb, (dd + 1) * ct:(dd + 2) * ct]
        h0 = hh[2 * lb:2 * lb + 1, dd * ct:(dd + 1) * ct]
        g_ref[dd] = h_pos + sgn * (h_neg - jnp.where(low, h0, 0.0))


def _spectra(kk, asum, fmat_ext, lb):
    ncol = kk.shape[0]
    nd = 2 * P_BLK - 1
    return pl.pallas_call(
        functools.partial(_spec_kernel, lb=lb), name="hyspec",
        grid=(ncol // CT,),
        in_specs=[pl.BlockSpec((CT, kk.shape[1]), lambda c: (c, 0)),
                  pl.BlockSpec((CT, 1), lambda c: (c, 0)),
                  _full(fmat_ext.shape)],
        out_specs=pl.BlockSpec((nd, 2 * lb, CT), lambda c: (0, 0, c)),
        out_shape=jax.ShapeDtypeStruct((nd, 2 * lb, ncol), F32),
        compiler_params=_cparams(("parallel",)),
    )(kk, asum, fmat_ext)


def _hyena_kernel(a_ref, gt_ref, g_ref, d_ref, f_ref, fi_ref, o_ref, *, lb):
    ct = a_ref.shape[2]
    x4 = jnp.concatenate([a_ref[0, j * lb:(j + 1) * lb, :].astype(BF16) for j in range(P_BLK)],
                         axis=1)
    grp = 2 * lb // FWD_GROUPS
    us = [jnp.dot(f_ref[q * grp:(q + 1) * grp, :], x4, preferred_element_type=F32)
          for q in range(FWD_GROUPS)]
    first = lax.broadcasted_iota(jnp.int32, (RC, 1), 0) == 0
    y_rows = []
    for r in range(lb // RC):
        re0, im0 = 2 * RC * r, 2 * RC * r + RC
        u = us[re0 // grp]
        ure = [u[re0 % grp:re0 % grp + RC, j * ct:(j + 1) * ct] for j in range(P_BLK)]
        uim = [u[im0 % grp:im0 % grp + RC, j * ct:(j + 1) * ct] for j in range(P_BLK)]
        y_re, y_im = [], []
        for i in range(P_BLK):
            yre = yim = None
            dc = ny = None
            for j in range(P_BLK):
                dd = i - j + P_BLK - 1
                gre = g_ref[dd, re0:re0 + RC, :]
                gim = g_ref[dd, im0:im0 + RC, :]
                rr = ure[j] * gre
                ii = uim[j] * gim
                tre = rr - ii
                tim = ure[j] * gim + uim[j] * gre
                yre = tre if yre is None else yre + tre
                yim = tim if yim is None else yim + tim
                if r == 0:
                    dc = rr if dc is None else dc + rr
                    ny = ii if ny is None else ny + ii
            if r == 0:
                yre = jnp.where(first, dc, yre)
                yim = jnp.where(first, ny, yim)
            y_re.append(yre.astype(BF16))
            y_im.append(yim.astype(BF16))
        y_rows.append(jnp.concatenate(y_re, axis=1))
        y_rows.append(jnp.concatenate(y_im, axis=1))
    y = jnp.concatenate(y_rows, axis=0)
    y4 = jnp.dot(fi_ref[...], y, preferred_element_type=F32)
    for i in range(P_BLK):
        rs = slice(i * lb, (i + 1) * lb)
        o_ref[0, rs, :] = (y4[:, i * ct:(i + 1) * ct] + a_ref[0, rs, :] * d_ref[...]) * gt_ref[0, rs, :]


def _hyena_stage(a_arr, a_blk0, g_arr, g_blk0, gspec, g_blk0_spec, d, fmat, fimat, lb):
    b, seq, _ = a_arr.shape
    nct = HY_C // CT
    nd = 2 * P_BLK - 1
    return pl.pallas_call(
        functools.partial(_hyena_kernel, lb=lb), name="hyena",
        grid=(nct, b),
        in_specs=[pl.BlockSpec((1, seq, CT), lambda c, bi: (bi, 0, a_blk0 + c)),
                  pl.BlockSpec((1, seq, CT), lambda c, bi: (bi, 0, g_blk0 + c)),
                  pl.BlockSpec((nd, 2 * lb, CT), lambda c, bi: (0, 0, g_blk0_spec + c)),
                  pl.BlockSpec((1, CT), lambda c, bi: (0, c)),
                  _full(fmat.shape), _full(fimat.shape)],
        out_specs=pl.BlockSpec((1, seq, CT), lambda c, bi: (bi, 0, c)),
        out_shape=jax.ShapeDtypeStruct((b, seq, HY_C), F32),
        compiler_params=_cparams(("parallel", "parallel")),
    )(a_arr, g_arr, gspec, d, fmat, fimat)


def _group_norm_tile(xs, g):
    lane = lax.broadcasted_iota(jnp.int32, xs.shape, 1)
    lo = lane < GROUP
    sq = xs * xs
    s_lo = jnp.sum(jnp.where(lo, sq, 0.0), axis=-1, keepdims=True)
    s_hi = jnp.sum(jnp.where(lo, 0.0, sq), axis=-1, keepdims=True)
    r = jnp.where(lo, lax.rsqrt(s_lo / GROUP + EPS), lax.rsqrt(s_hi / GROUP + EPS))
    return xs * r * g


def _ffn2_kernel(x0_ref, a0_ref, h0_ref, xn_ref, an_ref, hn_ref, hg_ref, wo_ref, g_ref, wg_ref, wu_ref,
                 wd_ref, gf_ref, o_ref, x2_sc, x2n_sc):
    def prepare(x_ref, a_ref, h_ref):
        tiles = []
        n_a = a_ref.shape[1] // LANES
        for s in range(D_MODEL // LANES):
            if s < n_a:
                src = a_ref[:, s * LANES:(s + 1) * LANES]
            else:
                src = h_ref[:, (s - n_a) * LANES:(s - n_a + 1) * LANES]
            tiles.append(_group_norm_tile(src, hg_ref[:, s * LANES:(s + 1) * LANES]).astype(BF16))
        mix = jnp.concatenate(tiles, axis=1)
        x2 = x_ref[...] + jnp.dot(mix, wo_ref[...], preferred_element_type=F32)
        return x2, _rms(x2, g_ref[...]).astype(BF16)

    @pl.when(pl.program_id(0) == 0)
    def _():
        x2_sc[...], x2n_sc[...] = prepare(x0_ref, a0_ref, h0_ref)

    y = _swiglu_residual(x2_sc[...], x2n_sc[...], wg_ref, wu_ref, wd_ref)
    o_ref[...] = _rms(y, gf_ref[...])
    x2_sc[...], x2n_sc[...] = prepare(xn_ref, an_ref, hn_ref)


def _ffn2(x1, a2d, h2d, hg, wo, g, wg, wu, wd, gf):
    t = x1.shape[0]
    n = t // TM_FFN
    first = lambda w: pl.BlockSpec((TM_FFN, w), lambda i: (0, 0))
    nxt = lambda w: pl.BlockSpec((TM_FFN, w), _next_tile(n))
    wa, wh = a2d.shape[1], h2d.shape[1]
    return pl.pallas_call(
        _ffn2_kernel, name="ffn2",
        grid=(n,),
        in_specs=[first(D_MODEL), first(wa), first(wh), nxt(D_MODEL), nxt(wa), nxt(wh),
                  _full(hg.shape), _full(wo.shape),
                  _full(g.shape), _full(wg.shape), _full(wu.shape), _full(wd.shape), _full(gf.shape)],
        out_specs=pl.BlockSpec((TM_FFN, D_MODEL), lambda i: (i, 0)),
        out_shape=jax.ShapeDtypeStruct(x1.shape, F32),
        scratch_shapes=[pltpu.VMEM((TM_FFN, D_MODEL), F32), pltpu.VMEM((TM_FFN, D_MODEL), BF16)],
        compiler_params=_cparams(("arbitrary",)),
    )(x1, a2d, h2d, x1, a2d, h2d, hg, wo, g, wg, wu, wd, gf)


def _rope_tables(seq):
    inv = 1.0 / (ROPE_THETA ** (jnp.arange(0, ROPE, 2, dtype=F32) / ROPE))
    ang = jnp.arange(seq, dtype=F32)[:, None] * inv[None, :]
    cos, sin = jnp.cos(ang), jnp.sin(ang)
    z64 = jnp.zeros((seq, NOPE), F32)
    z32 = jnp.zeros((seq, HEAD_PAD - NOPE - ROPE), F32)
    cq_t = jnp.concatenate([jnp.ones((seq, NOPE), F32), cos, cos, z32], axis=1)
    ck_t = jnp.concatenate([z64, cos, cos, z32], axis=1)
    sn_t = jnp.concatenate([z64, -sin, sin, z32], axis=1)
    return cq_t, ck_t, sn_t


def _dft_matrices(lb):
    n2 = 2 * lb
    nch = lb // RC
    two_pi = 2.0 * math.pi

    def tables(c, q, n):
        ang_c = (two_pi / (n2 // RC)) * ((c * n) % (n2 // RC)).astype(F32)
        ang_q = (two_pi / n2) * ((q * n) % n2).astype(F32)
        cc, sc, cq, sq = jnp.cos(ang_c), jnp.sin(ang_c), jnp.cos(ang_q), jnp.sin(ang_q)
        return cc * cq - sc * sq, sc * cq + cc * sq

    ar = lambda m: jnp.arange(m, dtype=jnp.int32)
    alt = lambda n: (1 - 2 * (n % 2)).astype(F32)
    c, q, n = ar(nch)[:, None, None], ar(RC)[None, :, None], ar(lb)[None, None, :]
    cos, sin = tables(c, q, n)
    f0 = (c == 0) & (q == 0)
    fwd = jnp.stack([cos, jnp.where(f0, alt(n), -sin)], axis=1).reshape(n2, lb)
    n, c, q = ar(lb)[:, None, None], ar(nch)[None, :, None], ar(RC)[None, None, :]
    cos, sin = tables(c, q, n)
    f0 = (c == 0) & (q == 0)
    inv_re = jnp.where(f0, 1.0 / n2, (2.0 / n2) * cos)
    inv_im = jnp.where(f0, alt(n) / n2, (-2.0 / n2) * sin)
    inv = jnp.stack([inv_re, inv_im], axis=2).reshape(lb, n2)
    return fwd.astype(BF16), inv.astype(BF16)


def kernel(x, ffn1_norm_g, ffn1_w_gate, ffn1_w_up, ffn1_w_down, mix_norm_g, w_in, q_norm_g, w_uq, kv_norm_g, w_ukv, hyena_conv_w, hyena_conv_b, filt_w1, filt_b1, filt_w2, filt_b2, filt_w3, filt_freq, hyena_d, head_norm_g, w_out, ffn2_norm_g, ffn2_w_gate, ffn2_w_up, ffn2_w_down, final_norm_g):
    b, seq, d = x.shape
    t = b * seq
    lb = seq // P_BLK
    l = 0
    row = lambda v: v.reshape(1, -1)

    x1 = _ffn1(x.reshape(t, d), row(ffn1_norm_g[l]), ffn1_w_gate[l].astype(BF16),
               ffn1_w_up[l].astype(BF16), ffn1_w_down[l].astype(BF16))

    wi = w_in[l]
    o_kr = Q_RANK + KV_RANK
    half = ROPE // 2
    zc = lambda n: jnp.zeros((d, n), F32)
    kr = wi[:, o_kr:o_kr + ROPE]
    win_p = jnp.concatenate([
        wi[:, :o_kr],
        zc(NOPE), kr, zc(HEAD_PAD - NOPE - ROPE),
        zc(NOPE), kr[:, half:], kr[:, :half], zc(HEAD_PAD - NOPE - ROPE),
        wi[:, o_kr + ROPE:]], axis=1).astype(BF16)
    wq3 = w_uq[l].reshape(Q_RANK, MLA_HEADS, NOPE + ROPE)
    zq = lambda n: jnp.zeros((Q_RANK, MLA_HEADS, n), F32)
    wq_p = jnp.concatenate([wq3, zq(HEAD_PAD - NOPE - ROPE)], axis=2).reshape(Q_RANK, -1).astype(BF16)
    wqs_p = jnp.concatenate([zq(NOPE), wq3[:, :, NOPE + half:], wq3[:, :, NOPE:NOPE + half],
                             zq(HEAD_PAD - NOPE - ROPE)], axis=2).reshape(Q_RANK, -1).astype(BF16)
    wkv3 = w_ukv[l].reshape(KV_RANK, MLA_HEADS, NOPE + V_DIM)
    wk_p = jnp.concatenate([wkv3[:, :, :NOPE], jnp.zeros((KV_RANK, MLA_HEADS, HEAD_PAD - NOPE), F32)],
                           axis=2).reshape(KV_RANK, -1).astype(BF16)
    wv_p = wkv3[:, :, NOPE:].reshape(KV_RANK, -1).astype(BF16)
    cq_t, ck_t, sn_t = _rope_tables(seq)
    qt, k4, vt, u3 = _inproj(x1.reshape(b, seq, d), row(mix_norm_g[l]), win_p,
                             hyena_conv_w[l], row(hyena_conv_b[l]), row(q_norm_g[l]),
                             wq_p.T, wqs_p.T, row(kv_norm_g[l]), wk_p, wv_p.T,
                             cq_t.T, sn_t.T, ck_t, sn_t)

    a = _attention(qt, k4, vt)

    col = lambda v: v.astype(F32).reshape(-1, 1)
    bands = col(jnp.linspace(1e-4, FILTER_BANDS - 1, FILTER_BANDS, dtype=F32))
    w1 = filt_w1[l].astype(F32)
    w3d = filt_w3[l].astype(F32).reshape(FILTER_HIDDEN, HY_ORDER, 2, HY_C).transpose(2, 1, 3, 0)
    w3d = w3d.reshape(2, HY_ORDER * HY_C, FILTER_HIDDEN).astype(BF16)
    deltas = jnp.abs(jnp.linspace(MIN_DECAY, MAX_DECAY, HY_C, dtype=F32))
    dl = col(jnp.tile(deltas, HY_ORDER))
    kk, asum = _filters(bands, col(w1[0]), w1[1:1 + FILTER_BANDS].T, w1[1 + FILTER_BANDS:].T,
                        col(filt_b1[l]), filt_w2[l].astype(F32).T, col(filt_b2[l]),
                        w3d, col(filt_freq[l]), dl, seq)
    fmat, fimat = _dft_matrices(lb)
    lag0 = jnp.zeros((BF16_ROWS, lb), BF16).at[0, 0].set(1.0)
    gspec = _spectra(kk, asum, jnp.concatenate([fmat, lag0], axis=0), lb)

    nct = HY_C // CT
    dsk = hyena_d[l].astype(F32)
    z1 = _hyena_stage(u3, 0, u3, nct, gspec, 0, dsk[0:1], fmat, fimat, lb)
    hy = _hyena_stage(z1, 0, u3, 2 * nct, gspec, nct, dsk[1:2], fmat, fimat, lb)

    out = _ffn2(x1, a.reshape(t, -1), hy.reshape(t, -1), row(head_norm_g[l]), w_out[l].astype(BF16),
                row(ffn2_norm_g[l]), ffn2_w_gate[l].astype(BF16), ffn2_w_up[l].astype(BF16),
                ffn2_w_down[l].astype(BF16), row(final_norm_g))
    return out.reshape(b, seq, d)
```

```python
import functools
import math

import jax
import jax.numpy as jnp
from jax import lax
from jax.experimental import pallas as pl
from jax.experimental.pallas import tpu as pltpu

F32 = jnp.float32
BF16 = jnp.bfloat16

D_MODEL = 1024
MLA_HEADS = 8
NOPE = 64
ROPE = 32
V_DIM = 64
Q_RANK = 256
KV_RANK = 128
ROPE_THETA = 10000.0
HY_C = 512
FILTER_BANDS = 16
FILTER_HIDDEN = 64
HY_ORDER = 2
FFN_HIDDEN = 2816
FFN_RES = 0.5
EPS = 1e-6
GROUP = 64
DECAY_TARGET = 1e-2
MAX_DECAY = math.log(DECAY_TARGET) / 0.3
MIN_DECAY = math.log(DECAY_TARGET) / 1.5

LANES = 128
HALO = 8
BF16_ROWS = 16
HEAD_PAD = 128
VMEM_LIMIT = 56 * 1024 * 1024

TM_FFN = 512
MXU_DIM = 256
FFN_CHUNK = 6 * MXU_DIM
TS_PROJ = 1024
TQ = 512
TK = 1024
PAIRS_PER_STEP = 2
ATT_SLOTS = 2
REF_ROWS = 16
ATT_SAFE_EXP = 60.0
F32_HUGE = 3.0e38
CT = 128
P_BLK = 4
RC = 32
FWD_GROUPS = 8
TL_FILT = 1024
CAST_STEPS = 8


def _cparams(sem, vmem_limit=VMEM_LIMIT, flags=None):
    return pltpu.CompilerParams(dimension_semantics=sem, vmem_limit_bytes=vmem_limit, flags=flags)


def _rms(x, g):
    ms = jnp.mean(x * x, axis=-1, keepdims=True)
    return x * lax.rsqrt(ms + EPS) * g


def _swiglu_residual(x, xn, wg_ref, wu_ref, wd_ref):
    acc = jnp.zeros(x.shape, F32)
    for c0 in range(0, FFN_HIDDEN, FFN_CHUNK):
        c1 = min(c0 + FFN_CHUNK, FFN_HIDDEN)
        gate = jnp.dot(xn, wg_ref[:, c0:c1], preferred_element_type=F32)
        up = jnp.dot(xn, wu_ref[:, c0:c1], preferred_element_type=F32)
        h = (gate * jax.nn.sigmoid(gate) * up).astype(BF16)
        acc = acc + jnp.dot(h, wd_ref[c0:c1, :], preferred_element_type=F32)
    return x + FFN_RES * acc


def _cast_kernel(*refs):
    n = len(refs) // 2
    for src, dst in zip(refs[:n], refs[n:]):
        dst[...] = src[...].astype(BF16)


def _to_bf16(ws):
    blocks = [(w.shape[0] // CAST_STEPS, w.shape[1]) for w in ws]
    specs = [pl.BlockSpec(blk, lambda i: (i, 0)) for blk in blocks]
    return pl.pallas_call(
        _cast_kernel, name="wcast",
        grid=(CAST_STEPS,),
        in_specs=specs, out_specs=specs,
        out_shape=[jax.ShapeDtypeStruct(w.shape, BF16) for w in ws],
        compiler_params=_cparams(("parallel",)),
    )(*ws)


def _next_tile(n):
    return lambda i: (jnp.minimum(i + 1, n - 1), 0)


def _ffn1_kernel(x_ref, g_ref, wg_ref, wu_ref, wd_ref, o_ref):
    x = x_ref[...]
    o_ref[...] = _swiglu_residual(x, _rms(x, g_ref[...]).astype(BF16), wg_ref, wu_ref, wd_ref)


def _full(shape):
    return pl.BlockSpec(shape, lambda *_: (0,) * len(shape))


def _ffn1(x2d, g, wg, wu, wd):
    t = x2d.shape[0]
    row = pl.BlockSpec((TM_FFN, D_MODEL), lambda i: (i, 0))
    return pl.pallas_call(
        _ffn1_kernel, name="ffn1",
        grid=(t // TM_FFN,),
        in_specs=[row, _full(g.shape), _full(wg.shape), _full(wu.shape), _full(wd.shape)],
        out_specs=row,
        out_shape=jax.ShapeDtypeStruct(x2d.shape, F32),
        compiler_params=_cparams(("parallel",)),
    )(x2d, g, wg, wu, wd)


def _inproj_kernel(x_ref, xp_ref, xn_ref, g_ref, win_ref, cw_ref, cb_ref, qg_ref, wqt_ref, wqst_ref,
                   kvg_ref, wk_ref, wvt_ref, cqt_ref, snt_ref, ck_ref, sn_ref,
                   q_ref, k_ref, v_ref, u_ref, *, scale):
    ts = x_ref.shape[1]
    i = pl.program_id(1)
    x_all = jnp.concatenate([xp_ref[0], x_ref[0], xn_ref[0]], axis=0)
    xn = _rms(x_all, g_ref[...]).astype(BF16)
    u_all = jnp.dot(xn, win_ref[...], preferred_element_type=F32)
    u = u_all[HALO:HALO + ts]
    c_q = u[:, 0:Q_RANK]
    c_kv = u[:, Q_RANK:Q_RANK + KV_RANK]
    o_pe = Q_RANK + KV_RANK
    kpe_a = u[:, o_pe:o_pe + LANES]
    kpe_b = u[:, o_pe + LANES:o_pe + 2 * LANES]

    o_hy = o_pe + 2 * LANES
    keep_prev = (i > 0).astype(F32)
    keep_next = (i < pl.num_programs(1) - 1).astype(F32)
    rows = ts + 2 * HALO
    uh = jnp.concatenate([u_all[0:HALO, o_hy:] * keep_prev, u_all[HALO:HALO + ts, o_hy:],
                          u_all[HALO + ts:, o_hy:] * keep_next], axis=0)
    prev = pltpu.roll(uh * cw_ref[0:1, :], 1, axis=0)
    nxt = pltpu.roll(uh * cw_ref[2:3, :], rows - 1, axis=0)
    conv = (cb_ref[...] + prev) + uh * cw_ref[1:2, :] + nxt
    u_ref[0] = conv[HALO:HALO + ts]

    nt = (((1,), (1,)), ((), ()))
    cqn = _rms(c_q, qg_ref[...]).astype(BF16)
    qt = lax.dot_general(wqt_ref[...], cqn, nt, preferred_element_type=F32)
    qst = lax.dot_general(wqst_ref[...], cqn, nt, preferred_element_type=F32)
    cq_t = cqt_ref[...]
    sn_tt = snt_ref[...]
    for h in range(MLA_HEADS):
        sl = slice(h * HEAD_PAD, (h + 1) * HEAD_PAD)
        q_ref[0, sl, :] = ((qt[sl] * cq_t + qst[sl] * sn_tt) * scale).astype(BF16)

    ckn = _rms(c_kv, kvg_ref[...]).astype(BF16)
    kn = jnp.dot(ckn, wk_ref[...], preferred_element_type=F32)
    k_pe = kpe_a * ck_ref[...] + kpe_b * sn_ref[...]
    for h in range(MLA_HEADS):
        sl = slice(h * HEAD_PAD, (h + 1) * HEAD_PAD)
        k_ref[0, h] = (kn[:, sl] + k_pe).astype(BF16)
    vt = lax.dot_general(wvt_ref[...], ckn, nt, preferred_element_type=F32)
    v_ref[0] = vt.reshape(MLA_HEADS // 2, 2 * V_DIM, vt.shape[1]).astype(BF16)


def _inproj(x1, g, win, cw, cb, qg, wqt, wqst, kvg, wk, wvt, cq_tt, sn_tt, ck_t, sn_t):
    b, seq, _ = x1.shape
    ts = TS_PROJ
    scale = (NOPE + ROPE) ** -0.5 * math.log2(math.e)
    hp = MLA_HEADS * HEAD_PAD
    hpt = ts // HALO
    last = seq // HALO - 1
    return pl.pallas_call(
        functools.partial(_inproj_kernel, scale=scale), name="inproj",
        grid=(b, seq // ts),
        in_specs=[pl.BlockSpec((1, ts, D_MODEL), lambda bi, i: (bi, i, 0)),
                  pl.BlockSpec((1, HALO, D_MODEL), lambda bi, i: (bi, jnp.maximum(i * hpt - 1, 0), 0)),
                  pl.BlockSpec((1, HALO, D_MODEL), lambda bi, i: (bi, jnp.minimum((i + 1) * hpt, last), 0)),
                  _full(g.shape), _full(win.shape), _full(cw.shape), _full(cb.shape),
                  _full(qg.shape), _full(wqt.shape),
                  _full(wqst.shape), _full(kvg.shape), _full(wk.shape), _full(wvt.shape),
                  pl.BlockSpec((HEAD_PAD, ts), lambda bi, i: (0, i)),
                  pl.BlockSpec((HEAD_PAD, ts), lambda bi, i: (0, i)),
                  pl.BlockSpec((ts, LANES), lambda bi, i: (i, 0)),
                  pl.BlockSpec((ts, LANES), lambda bi, i: (i, 0))],
        out_specs=[pl.BlockSpec((1, hp, ts), lambda bi, i: (bi, 0, i)),
                   pl.BlockSpec((1, MLA_HEADS, ts, HEAD_PAD), lambda bi, i: (bi, 0, i, 0)),
                   pl.BlockSpec((1, MLA_HEADS // 2, 2 * V_DIM, ts), lambda bi, i: (bi, 0, 0, i)),
                   pl.BlockSpec((1, ts, 3 * HY_C), lambda bi, i: (bi, i, 0))],
        out_shape=[jax.ShapeDtypeStruct((b, hp, seq), BF16),
                   jax.ShapeDtypeStruct((b, MLA_HEADS, seq, HEAD_PAD), BF16),
                   jax.ShapeDtypeStruct((b, MLA_HEADS // 2, 2 * V_DIM, seq), BF16),
                   jax.ShapeDtypeStruct((b, seq, 3 * HY_C), F32)],
        compiler_params=_cparams(("parallel", "parallel")),
    )(x1, x1, x1, g, win, cw, cb, qg, wqt, wqst, kvg, wk, wvt, cq_tt, sn_tt, ck_t, sn_t)


def _attn_robust(pr, q_ref, k_ref, v_ref, o_ref, s_buf, p_buf, mc_buf, al_buf, m_sc, l_sc, acc_sc):
    seq = k_ref.shape[2]
    nk = seq // TK
    nblk = 2 * nk

    def split(n):
        if isinstance(n, int):
            return n // nk, (n % nk) * TK, (n // nk) * HEAD_PAD
        h = (n >= nk).astype(jnp.int32)
        return h, pl.multiple_of((n - h * nk) * TK, TK), pl.multiple_of(h * HEAD_PAD, HEAD_PAD)

    def scores(n, slot):
        h, off, qoff = split(n)
        k = k_ref[0, 2 * pr + h, pl.ds(off, TK), :]
        s = jnp.dot(k, q_ref[0, pl.ds(2 * HEAD_PAD * pr + qoff, HEAD_PAD), :],
                    preferred_element_type=F32)
        s_buf[slot] = s
        mc_buf[slot] = jnp.max(s, axis=0, keepdims=True)

    def softmax(n, slot):
        h, _, _ = split(n)
        m_prev = m_sc[h]
        m_new = jnp.maximum(m_prev, mc_buf[slot])
        alpha = jnp.exp2(m_prev - m_new)
        p = jnp.exp2(s_buf[slot] - m_new)
        l_sc[h] = alpha * l_sc[h] + jnp.sum(p, axis=0, keepdims=True)
        p_buf[slot] = p.astype(BF16)
        al_buf[slot] = alpha
        m_sc[h] = m_new

    def values(n, slot):
        h, off, _ = split(n)
        vt = v_ref[0, pr, :, pl.ds(off, TK)]
        pv = jnp.dot(vt, p_buf[slot], preferred_element_type=F32)
        acc_sc[h] = al_buf[slot] * acc_sc[h] + pv

    m_sc[...] = jnp.full(m_sc.shape, -jnp.inf, F32)
    l_sc[...] = jnp.zeros(l_sc.shape, F32)
    acc_sc[...] = jnp.zeros(acc_sc.shape, F32)

    scores(0, 0)
    scores(1, 1)
    softmax(0, 0)

    def body(i, carry):
        n = 2 * i
        scores(n + 2, 0)
        softmax(n + 1, 1)
        values(n, 0)
        scores(n + 3, 1)
        softmax(n + 2, 0)
        values(n + 1, 1)
        return carry

    lax.fori_loop(0, (nblk - 2) // 2, body, 0)
    softmax(nblk - 1, 1)
    values(nblk - 2, 0)
    values(nblk - 1, 1)

    row = lax.broadcasted_iota(jnp.int32, acc_sc.shape[1:], 0)
    out_t = jnp.where(row < V_DIM, acc_sc[0] / l_sc[0], acc_sc[1] / l_sc[1])
    o_ref[0, :, 2 * V_DIM * pr:2 * V_DIM * (pr + 1)] = out_t.T


def _attn_kernel(q_ref, k_ref, v_ref, o_ref, *scratch):
    seq = k_ref.shape[2]
    bads = []
    for pr in range(PAIRS_PER_STEP):
        outs = []
        bad = None
        for h in range(2):
            qrow = (2 * pr + h) * HEAD_PAD
            qt = q_ref[0, qrow:qrow + HEAD_PAD, :]
            s0 = jnp.dot(k_ref[0, 2 * pr + h, 0:REF_ROWS, :], qt, preferred_element_type=F32)
            m_ref = jnp.max(s0, axis=0, keepdims=True)
            l = acc = None
            for j in range(seq // TK):
                ks = slice(j * TK, (j + 1) * TK)
                s = jnp.dot(k_ref[0, 2 * pr + h, ks, :], qt, preferred_element_type=F32)
                p = jnp.exp2(s - m_ref)
                p_sum = jnp.sum(p, axis=0, keepdims=True)
                pv = jnp.dot(v_ref[0, pr, :, ks], p.astype(BF16), preferred_element_type=F32)
                l = p_sum if l is None else l + p_sum
                acc = pv if acc is None else acc + pv
            out = acc / l
            outs.append(out)
            b_h = jnp.maximum(jnp.max(jnp.where(l < 2.0 ** ATT_SAFE_EXP, 0.0, 1.0)),
                              jnp.max(jnp.where(jnp.abs(out) < F32_HUGE, 0.0, 1.0)))
            bad = b_h if bad is None else jnp.maximum(bad, b_h)
        row = lax.broadcasted_iota(jnp.int32, outs[0].shape, 0)
        o_ref[0, :, 2 * V_DIM * pr:2 * V_DIM * (pr + 1)] = jnp.where(row < V_DIM, outs[0], outs[1]).T
        bads.append(bad)

    for pr in range(PAIRS_PER_STEP):
        pl.when(bads[pr] > 0.0)(
            functools.partial(_attn_robust, pr, q_ref, k_ref, v_ref, o_ref, *scratch))


def _attention(qt, k4, vt):
    b, _, seq = qt.shape
    pps = PAIRS_PER_STEP
    return pl.pallas_call(
        _attn_kernel, name="attn",
        grid=(b, MLA_HEADS // (2 * pps), seq // TQ),
        in_specs=[pl.BlockSpec((1, 2 * pps * HEAD_PAD, TQ), lambda bi, j, qi: (bi, j, qi)),
                  pl.BlockSpec((1, 2 * pps, seq, HEAD_PAD), lambda bi, j, qi: (bi, j, 0, 0)),
                  pl.BlockSpec((1, pps, 2 * V_DIM, seq), lambda bi, j, qi: (bi, j, 0, 0))],
        out_specs=pl.BlockSpec((1, TQ, 2 * pps * V_DIM), lambda bi, j, qi: (bi, qi, j)),
        out_shape=jax.ShapeDtypeStruct((b, seq, MLA_HEADS * V_DIM), F32),
        scratch_shapes=[pltpu.VMEM((ATT_SLOTS, TK, TQ), F32), pltpu.VMEM((ATT_SLOTS, TK, TQ), BF16),
                        pltpu.VMEM((ATT_SLOTS, 1, TQ), F32), pltpu.VMEM((ATT_SLOTS, 1, TQ), F32),
                        pltpu.VMEM((2, 1, TQ), F32), pltpu.VMEM((2, 1, TQ), F32),
                        pltpu.VMEM((2, 2 * V_DIM, TQ), F32)],
        compiler_params=_cparams(("parallel", "parallel", "parallel")),
    )(qt, k4, vt)


def _filter_kernel(bands_ref, w1t_ref, w1c_ref, w1s_ref, b1_ref, w2_ref, b2_ref, w3_ref,
                   fr_ref, dl_ref, kk_ref, asum_ref, *, seq):
    i = pl.program_id(0)
    hp = lax.Precision.HIGHEST
    l_idx = i * TL_FILT + lax.broadcasted_iota(jnp.int32, (1, TL_FILT), 1)
    pos_i = jnp.where(l_idx < seq, seq - l_idx, l_idx - seq)
    pos = pos_i.astype(F32)
    t = pos / float(max(seq - 1, 1))
    ang = (2.0 * math.pi * pos / seq) * bands_ref[...]
    pre = (t * w1t_ref[...]
           + jnp.dot(w1c_ref[...], jnp.cos(ang), precision=hp, preferred_element_type=F32)
           + jnp.dot(w1s_ref[...], -jnp.sin(ang), precision=hp, preferred_element_type=F32))
    fr = fr_ref[...]
    h = jnp.sin(fr * (pre + b1_ref[...]))
    h = jnp.sin(fr * (jnp.dot(w2_ref[...], h, precision=hp, preferred_element_type=F32) + b2_ref[...]))
    h = jnp.dot(w3_ref[0], h.astype(BF16), preferred_element_type=F32)
    decay = jnp.exp(-t * dl_ref[...])
    kk = jnp.where(l_idx == 0, 0.0, h * decay)
    kk_ref[...] = kk

    @pl.when(i == 0)
    def _():
        asum_ref[...] = jnp.zeros(asum_ref.shape, F32)

    asum_ref[...] += jnp.sum(jnp.abs(kk), axis=1, keepdims=True)


def _filters(bands, w1t, w1c, w1s, b1, w2, b2, w3d, fr, dl, seq):
    nl = 2 * seq // TL_FILT
    half = seq // TL_FILT
    ncol = HY_ORDER * HY_C
    return pl.pallas_call(
        functools.partial(_filter_kernel, seq=seq), name="hyfilt",
        grid=(nl,),
        in_specs=[_full(bands.shape), _full(w1t.shape), _full(w1c.shape), _full(w1s.shape),
                  _full(b1.shape), _full(w2.shape), _full(b2.shape),
                  pl.BlockSpec((1, ncol, FILTER_HIDDEN), lambda i: (jnp.where(i < half, 1, 0), 0, 0)),
                  _full(fr.shape), _full(dl.shape)],
        out_specs=[pl.BlockSpec((ncol, TL_FILT), lambda i: (0, i)),
                   pl.BlockSpec((ncol, 1), lambda i: (0, 0))],
        out_shape=[jax.ShapeDtypeStruct((ncol, 2 * seq), F32),
                   jax.ShapeDtypeStruct((ncol, 1), F32)],
        compiler_params=_cparams(("arbitrary",)),
    )(bands, w1t, w1c, w1s, b1, w2, b2, w3d, fr, dl)


def _spec_kernel(kk_ref, asum_ref, f_ref, g_ref, *, lb):
    ct = kk_ref.shape[0]
    nhalf = kk_ref.shape[1] // lb
    kkn = (kk_ref[...] / asum_ref[...]).astype(BF16)
    x8 = jnp.concatenate([kkn[:, m * lb:(m + 1) * lb] for m in range(nhalf)], axis=0)
    hh = lax.dot_general(f_ref[...], x8, (((1,), (1,)), ((), ())),
                         preferred_element_type=F32)
    row = lax.broadcasted_iota(jnp.int32, (2 * lb, 1), 0)
    sgn = (1 - 2 * (row & 1)).astype(F32)
    low = (((row // RC) & 1) == 0) | (row == RC)
    for dd in range(nhalf - 1):
        h_neg = hh[0:2 * lb, dd * ct:(dd + 1) * ct]
        h_pos = hh[0:2 * lb, (dd + 1) * ct:(dd + 2) * ct]
        h0 = hh[2 * lb:2 * lb + 1, dd * ct:(dd + 1) * ct]
        g_ref[dd] = h_pos + sgn * (h_neg - jnp.where(low, h0, 0.0))


def _spectra(kk, asum, fmat_ext, lb):
    ncol = kk.shape[0]
    nd = 2 * P_BLK - 1
    return pl.pallas_call(
        functools.partial(_spec_kernel, lb=lb), name="hyspec",
        grid=(ncol // CT,),
        in_specs=[pl.BlockSpec((CT, kk.shape[1]), lambda c: (c, 0)),
                  pl.BlockSpec((CT, 1), lambda c: (c, 0)),
                  _full(fmat_ext.shape)],
        out_specs=pl.BlockSpec((nd, 2 * lb, CT), lambda c: (0, 0, c)),
        out_shape=jax.ShapeDtypeStruct((nd, 2 * lb, ncol), F32),
        compiler_params=_cparams(("parallel",)),
    )(kk, asum, fmat_ext)


def _hyena_kernel(a_ref, gt_ref, g_ref, d_ref, f_ref, fi_ref, o_ref, *, lb):
    ct = a_ref.shape[2]
    x4 = jnp.concatenate([a_ref[0, j * lb:(j + 1) * lb, :].astype(BF16) for j in range(P_BLK)],
                         axis=1)
    grp = 2 * lb // FWD_GROUPS
    us = [jnp.dot(f_ref[q * grp:(q + 1) * grp, :], x4, preferred_element_type=F32)
          for q in range(FWD_GROUPS)]
    first = lax.broadcasted_iota(jnp.int32, (RC, 1), 0) == 0
    y_rows = []
    for r in range(lb // RC):
        re0, im0 = 2 * RC * r, 2 * RC * r + RC
        u = us[re0 // grp]
        ure = [u[re0 % grp:re0 % grp + RC, j * ct:(j + 1) * ct] for j in range(P_BLK)]
        uim = [u[im0 % grp:im0 % grp + RC, j * ct:(j + 1) * ct] for j in range(P_BLK)]
        y_re, y_im = [], []
        for i in range(P_BLK):
            yre = yim = None
            dc = ny = None
            for j in range(P_BLK):
                dd = i - j + P_BLK - 1
                gre = g_ref[dd, re0:re0 + RC, :]
                gim = g_ref[dd, im0:im0 + RC, :]
                rr = ure[j] * gre
                ii = uim[j] * gim
                tre = rr - ii
                tim = ure[j] * gim + uim[j] * gre
                yre = tre if yre is None else yre + tre
                yim = tim if yim is None else yim + tim
                if r == 0:
                    dc = rr if dc is None else dc + rr
                    ny = ii if ny is None else ny + ii
            if r == 0:
                yre = jnp.where(first, dc, yre)
                yim = jnp.where(first, ny, yim)
            y_re.append(yre.astype(BF16))
            y_im.append(yim.astype(BF16))
        y_rows.append(jnp.concatenate(y_re, axis=1))
        y_rows.append(jnp.concatenate(y_im, axis=1))
    y = jnp.concatenate(y_rows, axis=0)
    y4 = jnp.dot(fi_ref[...], y, preferred_element_type=F32)
    for i in range(P_BLK):
        rs = slice(i * lb, (i + 1) * lb)
        o_ref[0, rs, :] = (y4[:, i * ct:(i + 1) * ct] + a_ref[0, rs, :] * d_ref[...]) * gt_ref[0, rs, :]


def _hyena_stage(a_arr, a_blk0, g_arr, g_blk0, gspec, g_blk0_spec, d, fmat, fimat, lb):
    b, seq, _ = a_arr.shape
    nct = HY_C // CT
    nd = 2 * P_BLK - 1
    return pl.pallas_call(
        functools.partial(_hyena_kernel, lb=lb), name="hyena",
        grid=(nct, b),
        in_specs=[pl.BlockSpec((1, seq, CT), lambda c, bi: (bi, 0, a_blk0 + c)),
                  pl.BlockSpec((1, seq, CT), lambda c, bi: (bi, 0, g_blk0 + c)),
                  pl.BlockSpec((nd, 2 * lb, CT), lambda c, bi: (0, 0, g_blk0_spec + c)),
                  pl.BlockSpec((1, CT), lambda c, bi: (0, c)),
                  _full(fmat.shape), _full(fimat.shape)],
        out_specs=pl.BlockSpec((1, seq, CT), lambda c, bi: (bi, 0, c)),
        out_shape=jax.ShapeDtypeStruct((b, seq, HY_C), F32),
        compiler_params=_cparams(("parallel", "parallel")),
    )(a_arr, g_arr, gspec, d, fmat, fimat)


def _group_norm_tile(xs, g):
    lane = lax.broadcasted_iota(jnp.int32, xs.shape, 1)
    lo = lane < GROUP
    sq = xs * xs
    s_lo = jnp.sum(jnp.where(lo, sq, 0.0), axis=-1, keepdims=True)
    s_hi = jnp.sum(jnp.where(lo, 0.0, sq), axis=-1, keepdims=True)
    r = jnp.where(lo, lax.rsqrt(s_lo / GROUP + EPS), lax.rsqrt(s_hi / GROUP + EPS))
    return xs * r * g


def _ffn2_kernel(x0_ref, a0_ref, h0_ref, xn_ref, an_ref, hn_ref, hg_ref, wo_ref, g_ref, wg_ref, wu_ref,
                 wd_ref, gf_ref, o_ref, x2_sc, x2n_sc):
    def prepare(x_ref, a_ref, h_ref):
        tiles = []
        n_a = a_ref.shape[1] // LANES
        for s in range(D_MODEL // LANES):
            if s < n_a:
                src = a_ref[:, s * LANES:(s + 1) * LANES]
            else:
                src = h_ref[:, (s - n_a) * LANES:(s - n_a + 1) * LANES]
            tiles.append(_group_norm_tile(src, hg_ref[:, s * LANES:(s + 1) * LANES]).astype(BF16))
        mix = jnp.concatenate(tiles, axis=1)
        x2 = x_ref[...] + jnp.dot(mix, wo_ref[...], preferred_element_type=F32)
        return x2, _rms(x2, g_ref[...]).astype(BF16)

    @pl.when(pl.program_id(0) == 0)
    def _():
        x2_sc[...], x2n_sc[...] = prepare(x0_ref, a0_ref, h0_ref)

    y = _swiglu_residual(x2_sc[...], x2n_sc[...], wg_ref, wu_ref, wd_ref)
    o_ref[...] = _rms(y, gf_ref[...])
    x2_sc[...], x2n_sc[...] = prepare(xn_ref, an_ref, hn_ref)


def _ffn2(x1, a2d, h2d, hg, wo, g, wg, wu, wd, gf):
    t = x1.shape[0]
    n = t // TM_FFN
    first = lambda w: pl.BlockSpec((TM_FFN, w), lambda i: (0, 0))
    nxt = lambda w: pl.BlockSpec((TM_FFN, w), _next_tile(n))
    wa, wh = a2d.shape[1], h2d.shape[1]
    return pl.pallas_call(
        _ffn2_kernel, name="ffn2",
        grid=(n,),
        in_specs=[first(D_MODEL), first(wa), first(wh), nxt(D_MODEL), nxt(wa), nxt(wh),
                  _full(hg.shape), _full(wo.shape),
                  _full(g.shape), _full(wg.shape), _full(wu.shape), _full(wd.shape), _full(gf.shape)],
        out_specs=pl.BlockSpec((TM_FFN, D_MODEL), lambda i: (i, 0)),
        out_shape=jax.ShapeDtypeStruct(x1.shape, F32),
        scratch_shapes=[pltpu.VMEM((TM_FFN, D_MODEL), F32), pltpu.VMEM((TM_FFN, D_MODEL), BF16)],
        compiler_params=_cparams(("arbitrary",)),
    )(x1, a2d, h2d, x1, a2d, h2d, hg, wo, g, wg, wu, wd, gf)


def _rope_tables(seq):
    inv = 1.0 / (ROPE_THETA ** (jnp.arange(0, ROPE, 2, dtype=F32) / ROPE))
    ang = jnp.arange(seq, dtype=F32)[:, None] * inv[None, :]
    cos, sin = jnp.cos(ang), jnp.sin(ang)
    z64 = jnp.zeros((seq, NOPE), F32)
    z32 = jnp.zeros((seq, HEAD_PAD - NOPE - ROPE), F32)
    cq_t = jnp.concatenate([jnp.ones((seq, NOPE), F32), cos, cos, z32], axis=1)
    ck_t = jnp.concatenate([z64, cos, cos, z32], axis=1)
    sn_t = jnp.concatenate([z64, -sin, sin, z32], axis=1)
    return cq_t, ck_t, sn_t


def _dft_matrices(lb):
    n2 = 2 * lb
    nch = lb // RC
    two_pi = 2.0 * math.pi

    def tables(c, q, n):
        ang_c = (two_pi / (n2 // RC)) * ((c * n) % (n2 // RC)).astype(F32)
        ang_q = (two_pi / n2) * ((q * n) % n2).astype(F32)
        cc, sc, cq, sq = jnp.cos(ang_c), jnp.sin(ang_c), jnp.cos(ang_q), jnp.sin(ang_q)
        return cc * cq - sc * sq, sc * cq + cc * sq

    ar = lambda m: jnp.arange(m, dtype=jnp.int32)
    alt = lambda n: (1 - 2 * (n % 2)).astype(F32)
    c, q, n = ar(nch)[:, None, None], ar(RC)[None, :, None], ar(lb)[None, None, :]
    cos, sin = tables(c, q, n)
    f0 = (c == 0) & (q == 0)
    fwd = jnp.stack([cos, jnp.where(f0, alt(n), -sin)], axis=1).reshape(n2, lb)
    n, c, q = ar(lb)[:, None, None], ar(nch)[None, :, None], ar(RC)[None, None, :]
    cos, sin = tables(c, q, n)
    f0 = (c == 0) & (q == 0)
    inv_re = jnp.where(f0, 1.0 / n2, (2.0 / n2) * cos)
    inv_im = jnp.where(f0, alt(n) / n2, (-2.0 / n2) * sin)
    inv = jnp.stack([inv_re, inv_im], axis=2).reshape(lb, n2)
    return fwd.astype(BF16), inv.astype(BF16)


def kernel(x, ffn1_norm_g, ffn1_w_gate, ffn1_w_up, ffn1_w_down, mix_norm_g, w_in, q_norm_g, w_uq, kv_norm_g, w_ukv, hyena_conv_w, hyena_conv_b, filt_w1, filt_b1, filt_w2, filt_b2, filt_w3, filt_freq, hyena_d, head_norm_g, w_out, ffn2_norm_g, ffn2_w_gate, ffn2_w_up, ffn2_w_down, final_norm_g):
    b, seq, d = x.shape
    t = b * seq
    lb = seq // P_BLK
    l = 0
    row = lambda v: v.reshape(1, -1)

    x1 = _ffn1(x.reshape(t, d), row(ffn1_norm_g[l]),
               *_to_bf16([ffn1_w_gate[l], ffn1_w_up[l], ffn1_w_down[l]]))

    wi = w_in[l]
    o_kr = Q_RANK + KV_RANK
    half = ROPE // 2
    zc = lambda n: jnp.zeros((d, n), F32)
    kr = wi[:, o_kr:o_kr + ROPE]
    win_p = jnp.concatenate([
        wi[:, :o_kr],
        zc(NOPE), kr, zc(HEAD_PAD - NOPE - ROPE),
        zc(NOPE), kr[:, half:], kr[:, :half], zc(HEAD_PAD - NOPE - ROPE),
        wi[:, o_kr + ROPE:]], axis=1).astype(BF16)
    wq3 = w_uq[l].reshape(Q_RANK, MLA_HEADS, NOPE + ROPE)
    zq = lambda n: jnp.zeros((Q_RANK, MLA_HEADS, n), F32)
    wq_p = jnp.concatenate([wq3, zq(HEAD_PAD - NOPE - ROPE)], axis=2).reshape(Q_RANK, -1).astype(BF16)
    wqs_p = jnp.concatenate([zq(NOPE), wq3[:, :, NOPE + half:], wq3[:, :, NOPE:NOPE + half],
                             zq(HEAD_PAD - NOPE - ROPE)], axis=2).reshape(Q_RANK, -1).astype(BF16)
    wkv3 = w_ukv[l].reshape(KV_RANK, MLA_HEADS, NOPE + V_DIM)
    wk_p = jnp.concatenate([wkv3[:, :, :NOPE], jnp.zeros((KV_RANK, MLA_HEADS, HEAD_PAD - NOPE), F32)],
                           axis=2).reshape(KV_RANK, -1).astype(BF16)
    wv_p = wkv3[:, :, NOPE:].reshape(KV_RANK, -1).astype(BF16)
    cq_t, ck_t, sn_t = _rope_tables(seq)
    qt, k4, vt, u3 = _inproj(x1.reshape(b, seq, d), row(mix_norm_g[l]), win_p,
                             hyena_conv_w[l], row(hyena_conv_b[l]), row(q_norm_g[l]),
                             wq_p.T, wqs_p.T, row(kv_norm_g[l]), wk_p, wv_p.T,
                             cq_t.T, sn_t.T, ck_t, sn_t)

    a = _attention(qt, k4, vt)

    col = lambda v: v.astype(F32).reshape(-1, 1)
    bands = col(jnp.linspace(1e-4, FILTER_BANDS - 1, FILTER_BANDS, dtype=F32))
    w1 = filt_w1[l].astype(F32)
    w3d = filt_w3[l].astype(F32).reshape(FILTER_HIDDEN, HY_ORDER, 2, HY_C).transpose(2, 1, 3, 0)
    w3d = w3d.reshape(2, HY_ORDER * HY_C, FILTER_HIDDEN).astype(BF16)
    deltas = jnp.abs(jnp.linspace(MIN_DECAY, MAX_DECAY, HY_C, dtype=F32))
    dl = col(jnp.tile(deltas, HY_ORDER))
    kk, asum = _filters(bands, col(w1[0]), w1[1:1 + FILTER_BANDS].T, w1[1 + FILTER_BANDS:].T,
                        col(filt_b1[l]), filt_w2[l].astype(F32).T, col(filt_b2[l]),
                        w3d, col(filt_freq[l]), dl, seq)
    fmat, fimat = _dft_matrices(lb)
    lag0 = jnp.zeros((BF16_ROWS, lb), BF16).at[0, 0].set(1.0)
    gspec = _spectra(kk, asum, jnp.concatenate([fmat, lag0], axis=0), lb)

    nct = HY_C // CT
    dsk = hyena_d[l].astype(F32)
    z1 = _hyena_stage(u3, 0, u3, nct, gspec, 0, dsk[0:1], fmat, fimat, lb)
    hy = _hyena_stage(z1, 0, u3, 2 * nct, gspec, nct, dsk[1:2], fmat, fimat, lb)

    out = _ffn2(x1, a.reshape(t, -1), hy.reshape(t, -1), row(head_norm_g[l]), w_out[l].astype(BF16),
                row(ffn2_norm_g[l]), *_to_bf16([ffn2_w_gate[l], ffn2_w_up[l], ffn2_w_down[l]]),
                row(final_norm_g))
    return out.reshape(b, seq, d)
```

```python
import functools
import math

import jax
import jax.numpy as jnp
from jax import lax
from jax.experimental import pallas as pl
from jax.experimental.pallas import tpu as pltpu

F32 = jnp.float32
BF16 = jnp.bfloat16

D_MODEL = 1024
MLA_HEADS = 8
NOPE = 64
ROPE = 32
V_DIM = 64
Q_RANK = 256
KV_RANK = 128
ROPE_THETA = 10000.0
HY_C = 512
FILTER_BANDS = 16
FILTER_HIDDEN = 64
HY_ORDER = 2
FFN_HIDDEN = 2816
FFN_RES = 0.5
EPS = 1e-6
GROUP = 64
DECAY_TARGET = 1e-2
MAX_DECAY = math.log(DECAY_TARGET) / 0.3
MIN_DECAY = math.log(DECAY_TARGET) / 1.5

LANES = 128
HALO = 8
BF16_ROWS = 16
HEAD_PAD = 128
VMEM_LIMIT = 56 * 1024 * 1024

TM_FFN = 512
MXU_DIM = 256
FFN_CHUNK = 6 * MXU_DIM
TS_PROJ = 1024
TQ = 512
TK = 1024
PAIRS_PER_STEP = 2
ATT_SLOTS = 2
REF_ROWS = 16
ATT_SAFE_EXP = 60.0
F32_HUGE = 3.0e38
CT = 128
P_BLK = 4
RC = 16
FWD_GROUPS = 8
TL_FILT = 1024


def _cparams(sem, vmem_limit=VMEM_LIMIT, flags=None):
    return pltpu.CompilerParams(dimension_semantics=sem, vmem_limit_bytes=vmem_limit, flags=flags)


def _rms(x, g):
    ms = jnp.mean(x * x, axis=-1, keepdims=True)
    return x * lax.rsqrt(ms + EPS) * g


def _swiglu_residual(x, xn, wg_ref, wu_ref, wd_ref):
    acc = jnp.zeros(x.shape, F32)
    for c0 in range(0, FFN_HIDDEN, FFN_CHUNK):
        c1 = min(c0 + FFN_CHUNK, FFN_HIDDEN)
        gate = jnp.dot(xn, wg_ref[:, c0:c1], preferred_element_type=F32)
        up = jnp.dot(xn, wu_ref[:, c0:c1], preferred_element_type=F32)
        h = (gate * jax.nn.sigmoid(gate) * up).astype(BF16)
        acc = acc + jnp.dot(h, wd_ref[c0:c1, :], preferred_element_type=F32)
    return x + FFN_RES * acc


def _next_tile(n):
    return lambda i: (jnp.minimum(i + 1, n - 1), 0)


def _ffn1_kernel(x_ref, g_ref, wg_ref, wu_ref, wd_ref, o_ref):
    x = x_ref[...]
    o_ref[...] = _swiglu_residual(x, _rms(x, g_ref[...]).astype(BF16), wg_ref, wu_ref, wd_ref)


def _full(shape):
    return pl.BlockSpec(shape, lambda *_: (0,) * len(shape))


def _ffn1(x2d, g, wg, wu, wd):
    t = x2d.shape[0]
    row = pl.BlockSpec((TM_FFN, D_MODEL), lambda i: (i, 0))
    return pl.pallas_call(
        _ffn1_kernel, name="ffn1",
        grid=(t // TM_FFN,),
        in_specs=[row, _full(g.shape), _full(wg.shape), _full(wu.shape), _full(wd.shape)],
        out_specs=row,
        out_shape=jax.ShapeDtypeStruct(x2d.shape, F32),
        compiler_params=_cparams(("parallel",)),
    )(x2d, g, wg, wu, wd)


def _inproj_kernel(x_ref, xp_ref, xn_ref, g_ref, win_ref, cw_ref, cb_ref, qg_ref, wqt_ref, wqst_ref,
                   kvg_ref, wk_ref, wvt_ref, cqt_ref, snt_ref, ck_ref, sn_ref,
                   q_ref, k_ref, v_ref, u_ref, *, scale):
    ts = x_ref.shape[1]
    i = pl.program_id(1)
    x_all = jnp.concatenate([xp_ref[0], x_ref[0], xn_ref[0]], axis=0)
    xn = _rms(x_all, g_ref[...]).astype(BF16)
    u_all = jnp.dot(xn, win_ref[...], preferred_element_type=F32)
    u = u_all[HALO:HALO + ts]
    c_q = u[:, 0:Q_RANK]
    c_kv = u[:, Q_RANK:Q_RANK + KV_RANK]
    o_pe = Q_RANK + KV_RANK
    kpe_a = u[:, o_pe:o_pe + LANES]
    kpe_b = u[:, o_pe + LANES:o_pe + 2 * LANES]

    o_hy = o_pe + 2 * LANES
    keep_prev = (i > 0).astype(F32)
    keep_next = (i < pl.num_programs(1) - 1).astype(F32)
    rows = ts + 2 * HALO
    uh = jnp.concatenate([u_all[0:HALO, o_hy:] * keep_prev, u_all[HALO:HALO + ts, o_hy:],
                          u_all[HALO + ts:, o_hy:] * keep_next], axis=0)
    prev = pltpu.roll(uh * cw_ref[0:1, :], 1, axis=0)
    nxt = pltpu.roll(uh * cw_ref[2:3, :], rows - 1, axis=0)
    conv = (cb_ref[...] + prev) + uh * cw_ref[1:2, :] + nxt
    u_ref[0] = conv[HALO:HALO + ts]

    nt = (((1,), (1,)), ((), ()))
    cqn = _rms(c_q, qg_ref[...]).astype(BF16)
    qt = lax.dot_general(wqt_ref[...], cqn, nt, preferred_element_type=F32)
    qst = lax.dot_general(wqst_ref[...], cqn, nt, preferred_element_type=F32)
    cq_t = cqt_ref[...]
    sn_tt = snt_ref[...]
    for h in range(MLA_HEADS):
        sl = slice(h * HEAD_PAD, (h + 1) * HEAD_PAD)
        q_ref[0, sl, :] = ((qt[sl] * cq_t + qst[sl] * sn_tt) * scale).astype(BF16)

    ckn = _rms(c_kv, kvg_ref[...]).astype(BF16)
    kn = jnp.dot(ckn, wk_ref[...], preferred_element_type=F32)
    k_pe = kpe_a * ck_ref[...] + kpe_b * sn_ref[...]
    for h in range(MLA_HEADS):
        sl = slice(h * HEAD_PAD, (h + 1) * HEAD_PAD)
        k_ref[0, h] = (kn[:, sl] + k_pe).astype(BF16)
    vt = lax.dot_general(wvt_ref[...], ckn, nt, preferred_element_type=F32)
    v_ref[0] = vt.reshape(MLA_HEADS // 2, 2 * V_DIM, vt.shape[1]).astype(BF16)


def _inproj(x1, g, win, cw, cb, qg, wqt, wqst, kvg, wk, wvt, cq_tt, sn_tt, ck_t, sn_t):
    b, seq, _ = x1.shape
    ts = TS_PROJ
    scale = (NOPE + ROPE) ** -0.5 * math.log2(math.e)
    hp = MLA_HEADS * HEAD_PAD
    hpt = ts // HALO
    last = seq // HALO - 1
    return pl.pallas_call(
        functools.partial(_inproj_kernel, scale=scale), name="inproj",
        grid=(b, seq // ts),
        in_specs=[pl.BlockSpec((1, ts, D_MODEL), lambda bi, i: (bi, i, 0)),
                  pl.BlockSpec((1, HALO, D_MODEL), lambda bi, i: (bi, jnp.maximum(i * hpt - 1, 0), 0)),
                  pl.BlockSpec((1, HALO, D_MODEL), lambda bi, i: (bi, jnp.minimum((i + 1) * hpt, last), 0)),
                  _full(g.shape), _full(win.shape), _full(cw.shape), _full(cb.shape),
                  _full(qg.shape), _full(wqt.shape),
                  _full(wqst.shape), _full(kvg.shape), _full(wk.shape), _full(wvt.shape),
                  pl.BlockSpec((HEAD_PAD, ts), lambda bi, i: (0, i)),
                  pl.BlockSpec((HEAD_PAD, ts), lambda bi, i: (0, i)),
                  pl.BlockSpec((ts, LANES), lambda bi, i: (i, 0)),
                  pl.BlockSpec((ts, LANES), lambda bi, i: (i, 0))],
        out_specs=[pl.BlockSpec((1, hp, ts), lambda bi, i: (bi, 0, i)),
                   pl.BlockSpec((1, MLA_HEADS, ts, HEAD_PAD), lambda bi, i: (bi, 0, i, 0)),
                   pl.BlockSpec((1, MLA_HEADS // 2, 2 * V_DIM, ts), lambda bi, i: (bi, 0, 0, i)),
                   pl.BlockSpec((1, ts, 3 * HY_C), lambda bi, i: (bi, i, 0))],
        out_shape=[jax.ShapeDtypeStruct((b, hp, seq), BF16),
                   jax.ShapeDtypeStruct((b, MLA_HEADS, seq, HEAD_PAD), BF16),
                   jax.ShapeDtypeStruct((b, MLA_HEADS // 2, 2 * V_DIM, seq), BF16),
                   jax.ShapeDtypeStruct((b, seq, 3 * HY_C), F32)],
        compiler_params=_cparams(("parallel", "parallel")),
    )(x1, x1, x1, g, win, cw, cb, qg, wqt, wqst, kvg, wk, wvt, cq_tt, sn_tt, ck_t, sn_t)


def _attn_robust(pr, q_ref, k_ref, v_ref, o_ref, s_buf, p_buf, mc_buf, al_buf, m_sc, l_sc, acc_sc):
    seq = k_ref.shape[2]
    nk = seq // TK
    nblk = 2 * nk

    def split(n):
        if isinstance(n, int):
            return n // nk, (n % nk) * TK, (n // nk) * HEAD_PAD
        h = (n >= nk).astype(jnp.int32)
        return h, pl.multiple_of((n - h * nk) * TK, TK), pl.multiple_of(h * HEAD_PAD, HEAD_PAD)

    def scores(n, slot):
        h, off, qoff = split(n)
        k = k_ref[0, 2 * pr + h, pl.ds(off, TK), :]
        s = jnp.dot(k, q_ref[0, pl.ds(2 * HEAD_PAD * pr + qoff, HEAD_PAD), :],
                    preferred_element_type=F32)
        s_buf[slot] = s
        mc_buf[slot] = jnp.max(s, axis=0, keepdims=True)

    def softmax(n, slot):
        h, _, _ = split(n)
        m_prev = m_sc[h]
        m_new = jnp.maximum(m_prev, mc_buf[slot])
        alpha = jnp.exp2(m_prev - m_new)
        p = jnp.exp2(s_buf[slot] - m_new)
        l_sc[h] = alpha * l_sc[h] + jnp.sum(p, axis=0, keepdims=True)
        p_buf[slot] = p.astype(BF16)
        al_buf[slot] = alpha
        m_sc[h] = m_new

    def values(n, slot):
        h, off, _ = split(n)
        vt = v_ref[0, pr, :, pl.ds(off, TK)]
        pv = jnp.dot(vt, p_buf[slot], preferred_element_type=F32)
        acc_sc[h] = al_buf[slot] * acc_sc[h] + pv

    m_sc[...] = jnp.full(m_sc.shape, -jnp.inf, F32)
    l_sc[...] = jnp.zeros(l_sc.shape, F32)
    acc_sc[...] = jnp.zeros(acc_sc.shape, F32)

    scores(0, 0)
    scores(1, 1)
    softmax(0, 0)

    def body(i, carry):
        n = 2 * i
        scores(n + 2, 0)
        softmax(n + 1, 1)
        values(n, 0)
        scores(n + 3, 1)
        softmax(n + 2, 0)
        values(n + 1, 1)
        return carry

    lax.fori_loop(0, (nblk - 2) // 2, body, 0)
    softmax(nblk - 1, 1)
    values(nblk - 2, 0)
    values(nblk - 1, 1)

    row = lax.broadcasted_iota(jnp.int32, acc_sc.shape[1:], 0)
    out_t = jnp.where(row < V_DIM, acc_sc[0] / l_sc[0], acc_sc[1] / l_sc[1])
    o_ref[0, :, 2 * V_DIM * pr:2 * V_DIM * (pr + 1)] = out_t.T


def _attn_kernel(q_ref, k_ref, v_ref, o_ref, *scratch):
    seq = k_ref.shape[2]
    bads = []
    for pr in range(PAIRS_PER_STEP):
        outs = []
        bad = None
        for h in range(2):
            qrow = (2 * pr + h) * HEAD_PAD
            qt = q_ref[0, qrow:qrow + HEAD_PAD, :]
            s0 = jnp.dot(k_ref[0, 2 * pr + h, 0:REF_ROWS, :], qt, preferred_element_type=F32)
            m_ref = jnp.max(s0, axis=0, keepdims=True)
            l = acc = None
            for j in range(seq // TK):
                ks = slice(j * TK, (j + 1) * TK)
                s = jnp.dot(k_ref[0, 2 * pr + h, ks, :], qt, preferred_element_type=F32)
                p = jnp.exp2(s - m_ref)
                p_sum = jnp.sum(p, axis=0, keepdims=True)
                pv = jnp.dot(v_ref[0, pr, :, ks], p.astype(BF16), preferred_element_type=F32)
                l = p_sum if l is None else l + p_sum
                acc = pv if acc is None else acc + pv
            out = acc / l
            outs.append(out)
            b_h = jnp.maximum(jnp.max(jnp.where(l < 2.0 ** ATT_SAFE_EXP, 0.0, 1.0)),
                              jnp.max(jnp.where(jnp.abs(out) < F32_HUGE, 0.0, 1.0)))
            bad = b_h if bad is None else jnp.maximum(bad, b_h)
        row = lax.broadcasted_iota(jnp.int32, outs[0].shape, 0)
        o_ref[0, :, 2 * V_DIM * pr:2 * V_DIM * (pr + 1)] = jnp.where(row < V_DIM, outs[0], outs[1]).T
        bads.append(bad)

    for pr in range(PAIRS_PER_STEP):
        pl.when(bads[pr] > 0.0)(
            functools.partial(_attn_robust, pr, q_ref, k_ref, v_ref, o_ref, *scratch))


def _attention(qt, k4, vt):
    b, _, seq = qt.shape
    pps = PAIRS_PER_STEP
    return pl.pallas_call(
        _attn_kernel, name="attn",
        grid=(b, MLA_HEADS // (2 * pps), seq // TQ),
        in_specs=[pl.BlockSpec((1, 2 * pps * HEAD_PAD, TQ), lambda bi, j, qi: (bi, j, qi)),
                  pl.BlockSpec((1, 2 * pps, seq, HEAD_PAD), lambda bi, j, qi: (bi, j, 0, 0)),
                  pl.BlockSpec((1, pps, 2 * V_DIM, seq), lambda bi, j, qi: (bi, j, 0, 0))],
        out_specs=pl.BlockSpec((1, TQ, 2 * pps * V_DIM), lambda bi, j, qi: (bi, qi, j)),
        out_shape=jax.ShapeDtypeStruct((b, seq, MLA_HEADS * V_DIM), F32),
        scratch_shapes=[pltpu.VMEM((ATT_SLOTS, TK, TQ), F32), pltpu.VMEM((ATT_SLOTS, TK, TQ), BF16),
                        pltpu.VMEM((ATT_SLOTS, 1, TQ), F32), pltpu.VMEM((ATT_SLOTS, 1, TQ), F32),
                        pltpu.VMEM((2, 1, TQ), F32), pltpu.VMEM((2, 1, TQ), F32),
                        pltpu.VMEM((2, 2 * V_DIM, TQ), F32)],
        compiler_params=_cparams(("parallel", "parallel", "parallel")),
    )(qt, k4, vt)


def _filter_kernel(bands_ref, w1t_ref, w1c_ref, w1s_ref, b1_ref, w2_ref, b2_ref, w3_ref,
                   fr_ref, dl_ref, kk_ref, asum_ref, *, seq):
    i = pl.program_id(0)
    hp = lax.Precision.HIGHEST
    l_idx = i * TL_FILT + lax.broadcasted_iota(jnp.int32, (1, TL_FILT), 1)
    pos_i = jnp.where(l_idx < seq, seq - l_idx, l_idx - seq)
    pos = pos_i.astype(F32)
    t = pos / float(max(seq - 1, 1))
    ang = (2.0 * math.pi * pos / seq) * bands_ref[...]
    pre = (t * w1t_ref[...]
           + jnp.dot(w1c_ref[...], jnp.cos(ang), precision=hp, preferred_element_type=F32)
           + jnp.dot(w1s_ref[...], -jnp.sin(ang), precision=hp, preferred_element_type=F32))
    fr = fr_ref[...]
    h = jnp.sin(fr * (pre + b1_ref[...]))
    h = jnp.sin(fr * (jnp.dot(w2_ref[...], h, precision=hp, preferred_element_type=F32) + b2_ref[...]))
    h = jnp.dot(w3_ref[0], h.astype(BF16), preferred_element_type=F32)
    decay = jnp.exp(-t * dl_ref[...])
    kk = jnp.where(l_idx == 0, 0.0, h * decay)
    kk_ref[...] = kk

    @pl.when(i == 0)
    def _():
        asum_ref[...] = jnp.zeros(asum_ref.shape, F32)

    asum_ref[...] += jnp.sum(jnp.abs(kk), axis=1, keepdims=True)


def _filters(bands, w1t, w1c, w1s, b1, w2, b2, w3d, fr, dl, seq):
    nl = 2 * seq // TL_FILT
    half = seq // TL_FILT
    ncol = HY_ORDER * HY_C
    return pl.pallas_call(
        functools.partial(_filter_kernel, seq=seq), name="hyfilt",
        grid=(nl,),
        in_specs=[_full(bands.shape), _full(w1t.shape), _full(w1c.shape), _full(w1s.shape),
                  _full(b1.shape), _full(w2.shape), _full(b2.shape),
                  pl.BlockSpec((1, ncol, FILTER_HIDDEN), lambda i: (jnp.where(i < half, 1, 0), 0, 0)),
                  _full(fr.shape), _full(dl.shape)],
        out_specs=[pl.BlockSpec((ncol, TL_FILT), lambda i: (0, i)),
                   pl.BlockSpec((ncol, 1), lambda i: (0, 0))],
        out_shape=[jax.ShapeDtypeStruct((ncol, 2 * seq), F32),
                   jax.ShapeDtypeStruct((ncol, 1), F32)],
        compiler_params=_cparams(("arbitrary",)),
    )(bands, w1t, w1c, w1s, b1, w2, b2, w3d, fr, dl)


def _spec_kernel(kk_ref, asum_ref, f_ref, g_ref, *, lb):
    ct = kk_ref.shape[0]
    nhalf = kk_ref.shape[1] // lb
    kkn = (kk_ref[...] / asum_ref[...]).astype(BF16)
    x8 = jnp.concatenate([kkn[:, m * lb:(m + 1) * lb] for m in range(nhalf)], axis=0)
    hh = lax.dot_general(f_ref[...], x8, (((1,), (1,)), ((), ())),
                         preferred_element_type=F32)
    row = lax.broadcasted_iota(jnp.int32, (2 * lb, 1), 0)
    sgn = (1 - 2 * (row & 1)).astype(F32)
    low = (((row // RC) & 1) == 0) | (row == RC)
    for dd in range(nhalf - 1):
        h_neg = hh[0:2 * lb, dd * ct:(dd + 1) * ct]
        h_pos = hh[0:2 * lb, (dd + 1) * ct:(dd + 2) * ct]
        h0 = hh[2 * lb:2 * lb + 1, dd * ct:(dd + 1) * ct]
        g_ref[dd] = h_pos + sgn * (h_neg - jnp.where(low, h0, 0.0))


def _spectra(kk, asum, fmat_ext, lb):
    ncol = kk.shape[0]
    nd = 2 * P_BLK - 1
    return pl.pallas_call(
        functools.partial(_spec_kernel, lb=lb), name="hyspec",
        grid=(ncol // CT,),
        in_specs=[pl.BlockSpec((CT, kk.shape[1]), lambda c: (c, 0)),
                  pl.BlockSpec((CT, 1), lambda c: (c, 0)),
                  _full(fmat_ext.shape)],
        out_specs=pl.BlockSpec((nd, 2 * lb, CT), lambda c: (0, 0, c)),
        out_shape=jax.ShapeDtypeStruct((nd, 2 * lb, ncol), F32),
        compiler_params=_cparams(("parallel",)),
    )(kk, asum, fmat_ext)


def _hyena_kernel(a_ref, gt_ref, g_ref, d_ref, f_ref, fi_ref, o_ref, *, lb):
    ct = a_ref.shape[2]
    x4 = jnp.concatenate([a_ref[0, j * lb:(j + 1) * lb, :].astype(BF16) for j in range(P_BLK)],
                         axis=1)
    grp = 2 * lb // FWD_GROUPS
    us = [jnp.dot(f_ref[q * grp:(q + 1) * grp, :], x4, preferred_element_type=F32)
          for q in range(FWD_GROUPS)]
    first = lax.broadcasted_iota(jnp.int32, (RC, 1), 0) == 0
    y_rows = []
    for r in range(lb // RC):
        re0, im0 = 2 * RC * r, 2 * RC * r + RC
        u = us[re0 // grp]
        ure = [u[re0 % grp:re0 % grp + RC, j * ct:(j + 1) * ct] for j in range(P_BLK)]
        uim = [u[im0 % grp:im0 % grp + RC, j * ct:(j + 1) * ct] for j in range(P_BLK)]
        y_re, y_im = [], []
        for i in range(P_BLK):
            yre = yim = None
            dc = ny = None
            for j in range(P_BLK):
                dd = i - j + P_BLK - 1
                gre = g_ref[dd, re0:re0 + RC, :]
                gim = g_ref[dd, im0:im0 + RC, :]
                rr = ure[j] * gre
                ii = uim[j] * gim
                tre = rr - ii
                tim = ure[j] * gim + uim[j] * gre
                yre = tre if yre is None else yre + tre
                yim = tim if yim is None else yim + tim
                if r == 0:
                    dc = rr if dc is None else dc + rr
                    ny = ii if ny is None else ny + ii
            if r == 0:
                yre = jnp.where(first, dc, yre)
                yim = jnp.where(first, ny, yim)
            y_re.append(yre.astype(BF16))
            y_im.append(yim.astype(BF16))
        y_rows.append(jnp.concatenate(y_re, axis=1))
        y_rows.append(jnp.concatenate(y_im, axis=1))
    y = jnp.concatenate(y_rows, axis=0)
    y4 = jnp.dot(fi_ref[...], y, preferred_element_type=F32)
    for i in range(P_BLK):
        rs = slice(i * lb, (i + 1) * lb)
        o_ref[0, rs, :] = (y4[:, i * ct:(i + 1) * ct] + a_ref[0, rs, :] * d_ref[...]) * gt_ref[0, rs, :]


def _hyena_stage(a_arr, a_blk0, g_arr, g_blk0, gspec, g_blk0_spec, d, fmat, fimat, lb):
    b, seq, _ = a_arr.shape
    nct = HY_C // CT
    nd = 2 * P_BLK - 1
    return pl.pallas_call(
        functools.partial(_hyena_kernel, lb=lb), name="hyena",
        grid=(nct, b),
        in_specs=[pl.BlockSpec((1, seq, CT), lambda c, bi: (bi, 0, a_blk0 + c)),
                  pl.BlockSpec((1, seq, CT), lambda c, bi: (bi, 0, g_blk0 + c)),
                  pl.BlockSpec((nd, 2 * lb, CT), lambda c, bi: (0, 0, g_blk0_spec + c)),
                  pl.BlockSpec((1, CT), lambda c, bi: (0, c)),
                  _full(fmat.shape), _full(fimat.shape)],
        out_specs=pl.BlockSpec((1, seq, CT), lambda c, bi: (bi, 0, c)),
        out_shape=jax.ShapeDtypeStruct((b, seq, HY_C), F32),
        compiler_params=_cparams(("parallel", "parallel")),
    )(a_arr, g_arr, gspec, d, fmat, fimat)


def _group_norm_tile(xs, g):
    lane = lax.broadcasted_iota(jnp.int32, xs.shape, 1)
    lo = lane < GROUP
    sq = xs * xs
    s_lo = jnp.sum(jnp.where(lo, sq, 0.0), axis=-1, keepdims=True)
    s_hi = jnp.sum(jnp.where(lo, 0.0, sq), axis=-1, keepdims=True)
    r = jnp.where(lo, lax.rsqrt(s_lo / GROUP + EPS), lax.rsqrt(s_hi / GROUP + EPS))
    return xs * r * g


def _ffn2_kernel(x0_ref, a0_ref, h0_ref, xn_ref, an_ref, hn_ref, hg_ref, wo_ref, g_ref, wg_ref, wu_ref,
                 wd_ref, gf_ref, o_ref, x2_sc, x2n_sc):
    def prepare(x_ref, a_ref, h_ref):
        tiles = []
        n_a = a_ref.shape[1] // LANES
        for s in range(D_MODEL // LANES):
            if s < n_a:
                src = a_ref[:, s * LANES:(s + 1) * LANES]
            else:
                src = h_ref[:, (s - n_a) * LANES:(s - n_a + 1) * LANES]
            tiles.append(_group_norm_tile(src, hg_ref[:, s * LANES:(s + 1) * LANES]).astype(BF16))
        mix = jnp.concatenate(tiles, axis=1)
        x2 = x_ref[...] + jnp.dot(mix, wo_ref[...], preferred_element_type=F32)
        return x2, _rms(x2, g_ref[...]).astype(BF16)

    @pl.when(pl.program_id(0) == 0)
    def _():
        x2_sc[...], x2n_sc[...] = prepare(x0_ref, a0_ref, h0_ref)

    y = _swiglu_residual(x2_sc[...], x2n_sc[...], wg_ref, wu_ref, wd_ref)
    o_ref[...] = _rms(y, gf_ref[...])
    x2_sc[...], x2n_sc[...] = prepare(xn_ref, an_ref, hn_ref)


def _ffn2(x1, a2d, h2d, hg, wo, g, wg, wu, wd, gf):
    t = x1.shape[0]
    n = t // TM_FFN
    first = lambda w: pl.BlockSpec((TM_FFN, w), lambda i: (0, 0))
    nxt = lambda w: pl.BlockSpec((TM_FFN, w), _next_tile(n))
    wa, wh = a2d.shape[1], h2d.shape[1]
    return pl.pallas_call(
        _ffn2_kernel, name="ffn2",
        grid=(n,),
        in_specs=[first(D_MODEL), first(wa), first(wh), nxt(D_MODEL), nxt(wa), nxt(wh),
                  _full(hg.shape), _full(wo.shape),
                  _full(g.shape), _full(wg.shape), _full(wu.shape), _full(wd.shape), _full(gf.shape)],
        out_specs=pl.BlockSpec((TM_FFN, D_MODEL), lambda i: (i, 0)),
        out_shape=jax.ShapeDtypeStruct(x1.shape, F32),
        scratch_shapes=[pltpu.VMEM((TM_FFN, D_MODEL), F32), pltpu.VMEM((TM_FFN, D_MODEL), BF16)],
        compiler_params=_cparams(("arbitrary",)),
    )(x1, a2d, h2d, x1, a2d, h2d, hg, wo, g, wg, wu, wd, gf)


def _rope_tables(seq):
    inv = 1.0 / (ROPE_THETA ** (jnp.arange(0, ROPE, 2, dtype=F32) / ROPE))
    ang = jnp.arange(seq, dtype=F32)[:, None] * inv[None, :]
    cos, sin = jnp.cos(ang), jnp.sin(ang)
    z64 = jnp.zeros((seq, NOPE), F32)
    z32 = jnp.zeros((seq, HEAD_PAD - NOPE - ROPE), F32)
    cq_t = jnp.concatenate([jnp.ones((seq, NOPE), F32), cos, cos, z32], axis=1)
    ck_t = jnp.concatenate([z64, cos, cos, z32], axis=1)
    sn_t = jnp.concatenate([z64, -sin, sin, z32], axis=1)
    return cq_t, ck_t, sn_t


def _dft_matrices(lb):
    n2 = 2 * lb
    nch = lb // RC
    two_pi = 2.0 * math.pi

    def tables(c, q, n):
        ang_c = (two_pi / (n2 // RC)) * ((c * n) % (n2 // RC)).astype(F32)
        ang_q = (two_pi / n2) * ((q * n) % n2).astype(F32)
        cc, sc, cq, sq = jnp.cos(ang_c), jnp.sin(ang_c), jnp.cos(ang_q), jnp.sin(ang_q)
        return cc * cq - sc * sq, sc * cq + cc * sq

    ar = lambda m: jnp.arange(m, dtype=jnp.int32)
    alt = lambda n: (1 - 2 * (n % 2)).astype(F32)
    c, q, n = ar(nch)[:, None, None], ar(RC)[None, :, None], ar(lb)[None, None, :]
    cos, sin = tables(c, q, n)
    f0 = (c == 0) & (q == 0)
    fwd = jnp.stack([cos, jnp.where(f0, alt(n), -sin)], axis=1).reshape(n2, lb)
    n, c, q = ar(lb)[:, None, None], ar(nch)[None, :, None], ar(RC)[None, None, :]
    cos, sin = tables(c, q, n)
    f0 = (c == 0) & (q == 0)
    inv_re = jnp.where(f0, 1.0 / n2, (2.0 / n2) * cos)
    inv_im = jnp.where(f0, alt(n) / n2, (-2.0 / n2) * sin)
    inv = jnp.stack([inv_re, inv_im], axis=2).reshape(lb, n2)
    return fwd.astype(BF16), inv.astype(BF16)


def kernel(x, ffn1_norm_g, ffn1_w_gate, ffn1_w_up, ffn1_w_down, mix_norm_g, w_in, q_norm_g, w_uq, kv_norm_g, w_ukv, hyena_conv_w, hyena_conv_b, filt_w1, filt_b1, filt_w2, filt_b2, filt_w3, filt_freq, hyena_d, head_norm_g, w_out, ffn2_norm_g, ffn2_w_gate, ffn2_w_up, ffn2_w_down, final_norm_g):
    b, seq, d = x.shape
    t = b * seq
    lb = seq // P_BLK
    l = 0
    row = lambda v: v.reshape(1, -1)

    x1 = _ffn1(x.reshape(t, d), row(ffn1_norm_g[l]), ffn1_w_gate[l].astype(BF16),
               ffn1_w_up[l].astype(BF16), ffn1_w_down[l].astype(BF16))

    wi = w_in[l]
    o_kr = Q_RANK + KV_RANK
    half = ROPE // 2
    zc = lambda n: jnp.zeros((d, n), F32)
    kr = wi[:, o_kr:o_kr + ROPE]
    win_p = jnp.concatenate([
        wi[:, :o_kr],
        zc(NOPE), kr, zc(HEAD_PAD - NOPE - ROPE),
        zc(NOPE), kr[:, half:], kr[:, :half], zc(HEAD_PAD - NOPE - ROPE),
        wi[:, o_kr + ROPE:]], axis=1).astype(BF16)
    wq3 = w_uq[l].reshape(Q_RANK, MLA_HEADS, NOPE + ROPE)
    zq = lambda n: jnp.zeros((Q_RANK, MLA_HEADS, n), F32)
    wq_p = jnp.concatenate([wq3, zq(HEAD_PAD - NOPE - ROPE)], axis=2).reshape(Q_RANK, -1).astype(BF16)
    wqs_p = jnp.concatenate([zq(NOPE), wq3[:, :, NOPE + half:], wq3[:, :, NOPE:NOPE + half],
                             zq(HEAD_PAD - NOPE - ROPE)], axis=2).reshape(Q_RANK, -1).astype(BF16)
    wkv3 = w_ukv[l].reshape(KV_RANK, MLA_HEADS, NOPE + V_DIM)
    wk_p = jnp.concatenate([wkv3[:, :, :NOPE], jnp.zeros((KV_RANK, MLA_HEADS, HEAD_PAD - NOPE), F32)],
                           axis=2).reshape(KV_RANK, -1).astype(BF16)
    wv_p = wkv3[:, :, NOPE:].reshape(KV_RANK, -1).astype(BF16)
    cq_t, ck_t, sn_t = _rope_tables(seq)
    qt, k4, vt, u3 = _inproj(x1.reshape(b, seq, d), row(mix_norm_g[l]), win_p,
                             hyena_conv_w[l], row(hyena_conv_b[l]), row(q_norm_g[l]),
                             wq_p.T, wqs_p.T, row(kv_norm_g[l]), wk_p, wv_p.T,
                             cq_t.T, sn_t.T, ck_t, sn_t)

    a = _attention(qt, k4, vt)

    col = lambda v: v.astype(F32).reshape(-1, 1)
    bands = col(jnp.linspace(1e-4, FILTER_BANDS - 1, FILTER_BANDS, dtype=F32))
    w1 = filt_w1[l].astype(F32)
    w3d = filt_w3[l].astype(F32).reshape(FILTER_HIDDEN, HY_ORDER, 2, HY_C).transpose(2, 1, 3, 0)
    w3d = w3d.reshape(2, HY_ORDER * HY_C, FILTER_HIDDEN).astype(BF16)
    deltas = jnp.abs(jnp.linspace(MIN_DECAY, MAX_DECAY, HY_C, dtype=F32))
    dl = col(jnp.tile(deltas, HY_ORDER))
    kk, asum = _filters(bands, col(w1[0]), w1[1:1 + FILTER_BANDS].T, w1[1 + FILTER_BANDS:].T,
                        col(filt_b1[l]), filt_w2[l].astype(F32).T, col(filt_b2[l]),
                        w3d, col(filt_freq[l]), dl, seq)
    fmat, fimat = _dft_matrices(lb)
    lag0 = jnp.zeros((BF16_ROWS, lb), BF16).at[0, 0].set(1.0)
    gspec = _spectra(kk, asum, jnp.concatenate([fmat, lag0], axis=0), lb)

    nct = HY_C // CT
    dsk = hyena_d[l].astype(F32)
    z1 = _hyena_stage(u3, 0, u3, nct, gspec, 0, dsk[0:1], fmat, fimat, lb)
    hy = _hyena_stage(z1, 0, u3, 2 * nct, gspec, nct, dsk[1:2], fmat, fimat, lb)

    out = _ffn2(x1, a.reshape(t, -1), hy.reshape(t, -1), row(head_norm_g[l]), w_out[l].astype(BF16),
                row(ffn2_norm_g[l]), ffn2_w_gate[l].astype(BF16), ffn2_w_up[l].astype(BF16),
                ffn2_w_down[l].astype(BF16), row(final_norm_g))
    return out.reshape(b, seq, d)
```

```python
import functools
import math

import jax
import jax.numpy as jnp
from jax import lax
from jax.experimental import pallas as pl
from jax.experimental.pallas import tpu as pltpu

F32 = jnp.float32
BF16 = jnp.bfloat16

D_MODEL = 1024
MLA_HEADS = 8
NOPE = 64
ROPE = 32
V_DIM = 64
Q_RANK = 256
KV_RANK = 128
ROPE_THETA = 10000.0
HY_C = 512
FILTER_BANDS = 16
FILTER_HIDDEN = 64
HY_ORDER = 2
FFN_HIDDEN = 2816
FFN_RES = 0.5
EPS = 1e-6
GROUP = 64
DECAY_TARGET = 1e-2
MAX_DECAY = math.log(DECAY_TARGET) / 0.3
MIN_DECAY = math.log(DECAY_TARGET) / 1.5

LANES = 128
HALO = 8
BF16_ROWS = 16
HEAD_PAD = 128
VMEM_LIMIT = 56 * 1024 * 1024

TM_FFN = 512
MXU_DIM = 256
FFN_CHUNK = 6 * MXU_DIM
TS_PROJ = 1024
TQ = 512
TK = 1024
PAIRS_PER_STEP = 2
ATT_SLOTS = 2
REF_ROWS = 16
ATT_SAFE_EXP = 60.0
F32_HUGE = 3.0e38
CT = 128
P_BLK = 4
RC = 16
FWD_GROUPS = 8
TL_FILT = 1024


def _cparams(sem, vmem_limit=VMEM_LIMIT, flags=None):
    return pltpu.CompilerParams(dimension_semantics=sem, vmem_limit_bytes=vmem_limit, flags=flags)


def _rms(x, g):
    ms = jnp.mean(x * x, axis=-1, keepdims=True)
    return x * lax.rsqrt(ms + EPS) * g


def _swiglu_residual(x, xn, wg_ref, wu_ref, wd_ref):
    acc = jnp.zeros(x.shape, F32)
    for c0 in range(0, FFN_HIDDEN, FFN_CHUNK):
        c1 = min(c0 + FFN_CHUNK, FFN_HIDDEN)
        gate = jnp.dot(xn, wg_ref[:, c0:c1], preferred_element_type=F32)
        up = jnp.dot(xn, wu_ref[:, c0:c1], preferred_element_type=F32)
        h = (gate * jax.nn.sigmoid(gate) * up).astype(BF16)
        acc = acc + jnp.dot(h, wd_ref[c0:c1, :], preferred_element_type=F32)
    return x + FFN_RES * acc


def _next_tile(n):
    return lambda i: (jnp.minimum(i + 1, n - 1), 0)


def _ffn1_kernel(x_ref, g_ref, wg_ref, wu_ref, wd_ref, o_ref):
    x = x_ref[...]
    o_ref[...] = _swiglu_residual(x, _rms(x, g_ref[...]).astype(BF16), wg_ref, wu_ref, wd_ref)


def _full(shape):
    return pl.BlockSpec(shape, lambda *_: (0,) * len(shape))


def _ffn1(x2d, g, wg, wu, wd):
    t = x2d.shape[0]
    row = pl.BlockSpec((TM_FFN, D_MODEL), lambda i: (i, 0))
    return pl.pallas_call(
        _ffn1_kernel, name="ffn1",
        grid=(t // TM_FFN,),
        in_specs=[row, _full(g.shape), _full(wg.shape), _full(wu.shape), _full(wd.shape)],
        out_specs=row,
        out_shape=jax.ShapeDtypeStruct(x2d.shape, F32),
        compiler_params=_cparams(("parallel",)),
    )(x2d, g, wg, wu, wd)


def _inproj_kernel(x_ref, xp_ref, xn_ref, g_ref, win_ref, cw_ref, cb_ref, qg_ref, wqt_ref, wqst_ref,
                   kvg_ref, wk_ref, wvt_ref, cqt_ref, snt_ref, ck_ref, sn_ref,
                   q_ref, k_ref, v_ref, u_ref, *, scale):
    ts = x_ref.shape[1]
    i = pl.program_id(1)
    x_all = jnp.concatenate([xp_ref[0], x_ref[0], xn_ref[0]], axis=0)
    xn = _rms(x_all, g_ref[...]).astype(BF16)
    u_all = jnp.dot(xn, win_ref[...], preferred_element_type=F32)
    u = u_all[HALO:HALO + ts]
    c_q = u[:, 0:Q_RANK]
    c_kv = u[:, Q_RANK:Q_RANK + KV_RANK]
    o_pe = Q_RANK + KV_RANK
    kpe_a = u[:, o_pe:o_pe + LANES]
    kpe_b = u[:, o_pe + LANES:o_pe + 2 * LANES]

    o_hy = o_pe + 2 * LANES
    keep_prev = (i > 0).astype(F32)
    keep_next = (i < pl.num_programs(1) - 1).astype(F32)
    rows = ts + 2 * HALO
    uh = jnp.concatenate([u_all[0:HALO, o_hy:] * keep_prev, u_all[HALO:HALO + ts, o_hy:],
                          u_all[HALO + ts:, o_hy:] * keep_next], axis=0)
    prev = pltpu.roll(uh * cw_ref[0:1, :], 1, axis=0)
    nxt = pltpu.roll(uh * cw_ref[2:3, :], rows - 1, axis=0)
    conv = (cb_ref[...] + prev) + uh * cw_ref[1:2, :] + nxt
    u_ref[0] = conv[HALO:HALO + ts]

    nt = (((1,), (1,)), ((), ()))
    cqn = _rms(c_q, qg_ref[...]).astype(BF16)
    qt = lax.dot_general(wqt_ref[...], cqn, nt, preferred_element_type=F32)
    qst = lax.dot_general(wqst_ref[...], cqn, nt, preferred_element_type=F32)
    cq_t = cqt_ref[...]
    sn_tt = snt_ref[...]
    for h in range(MLA_HEADS):
        sl = slice(h * HEAD_PAD, (h + 1) * HEAD_PAD)
        q_ref[0, sl, :] = ((qt[sl] * cq_t + qst[sl] * sn_tt) * scale).astype(BF16)

    ckn = _rms(c_kv, kvg_ref[...]).astype(BF16)
    kn = jnp.dot(ckn, wk_ref[...], preferred_element_type=F32)
    k_pe = kpe_a * ck_ref[...] + kpe_b * sn_ref[...]
    for h in range(MLA_HEADS):
        sl = slice(h * HEAD_PAD, (h + 1) * HEAD_PAD)
        k_ref[0, h] = (kn[:, sl] + k_pe).astype(BF16)
    vt = lax.dot_general(wvt_ref[...], ckn, nt, preferred_element_type=F32)
    v_ref[0] = vt.reshape(MLA_HEADS // 2, 2 * V_DIM, vt.shape[1]).astype(BF16)


def _inproj(x1, g, win, cw, cb, qg, wqt, wqst, kvg, wk, wvt, cq_tt, sn_tt, ck_t, sn_t):
    b, seq, _ = x1.shape
    ts = TS_PROJ
    scale = (NOPE + ROPE) ** -0.5 * math.log2(math.e)
    hp = MLA_HEADS * HEAD_PAD
    hpt = ts // HALO
    last = seq // HALO - 1
    return pl.pallas_call(
        functools.partial(_inproj_kernel, scale=scale), name="inproj",
        grid=(b, seq // ts),
        in_specs=[pl.BlockSpec((1, ts, D_MODEL), lambda bi, i: (bi, i, 0)),
                  pl.BlockSpec((1, HALO, D_MODEL), lambda bi, i: (bi, jnp.maximum(i * hpt - 1, 0), 0)),
                  pl.BlockSpec((1, HALO, D_MODEL), lambda bi, i: (bi, jnp.minimum((i + 1) * hpt, last), 0)),
                  _full(g.shape), _full(win.shape), _full(cw.shape), _full(cb.shape),
                  _full(qg.shape), _full(wqt.shape),
                  _full(wqst.shape), _full(kvg.shape), _full(wk.shape), _full(wvt.shape),
                  pl.BlockSpec((HEAD_PAD, ts), lambda bi, i: (0, i)),
                  pl.BlockSpec((HEAD_PAD, ts), lambda bi, i: (0, i)),
                  pl.BlockSpec((ts, LANES), lambda bi, i: (i, 0)),
                  pl.BlockSpec((ts, LANES), lambda bi, i: (i, 0))],
        out_specs=[pl.BlockSpec((1, hp, ts), lambda bi, i: (bi, 0, i)),
                   pl.BlockSpec((1, MLA_HEADS, ts, HEAD_PAD), lambda bi, i: (bi, 0, i, 0)),
                   pl.BlockSpec((1, MLA_HEADS // 2, 2 * V_DIM, ts), lambda bi, i: (bi, 0, 0, i)),
                   pl.BlockSpec((1, ts, 3 * HY_C), lambda bi, i: (bi, i, 0))],
        out_shape=[jax.ShapeDtypeStruct((b, hp, seq), BF16),
                   jax.ShapeDtypeStruct((b, MLA_HEADS, seq, HEAD_PAD), BF16),
                   jax.ShapeDtypeStruct((b, MLA_HEADS // 2, 2 * V_DIM, seq), BF16),
                   jax.ShapeDtypeStruct((b, seq, 3 * HY_C), F32)],
        compiler_params=_cparams(("parallel", "parallel")),
    )(x1, x1, x1, g, win, cw, cb, qg, wqt, wqst, kvg, wk, wvt, cq_tt, sn_tt, ck_t, sn_t)


def _head_norm_t(out_t, hg_col):
    row = lax.broadcasted_iota(jnp.int32, out_t.shape, 0)
    sq = out_t * out_t
    s_lo = jnp.sum(sq[:V_DIM], axis=0, keepdims=True)
    s_hi = jnp.sum(sq[V_DIM:], axis=0, keepdims=True)
    r = jnp.where(row < V_DIM, lax.rsqrt(s_lo / V_DIM + EPS), lax.rsqrt(s_hi / V_DIM + EPS))
    return out_t * r * hg_col


def _attn_robust(pr, q_ref, k_ref, v_ref, hg_ref, o_ref, s_buf, p_buf, mc_buf, al_buf, m_sc, l_sc,
                 acc_sc):
    seq = k_ref.shape[2]
    nk = seq // TK
    nblk = 2 * nk

    def split(n):
        if isinstance(n, int):
            return n // nk, (n % nk) * TK, (n // nk) * HEAD_PAD
        h = (n >= nk).astype(jnp.int32)
        return h, pl.multiple_of((n - h * nk) * TK, TK), pl.multiple_of(h * HEAD_PAD, HEAD_PAD)

    def scores(n, slot):
        h, off, qoff = split(n)
        k = k_ref[0, 2 * pr + h, pl.ds(off, TK), :]
        s = jnp.dot(k, q_ref[0, pl.ds(2 * HEAD_PAD * pr + qoff, HEAD_PAD), :],
                    preferred_element_type=F32)
        s_buf[slot] = s
        mc_buf[slot] = jnp.max(s, axis=0, keepdims=True)

    def softmax(n, slot):
        h, _, _ = split(n)
        m_prev = m_sc[h]
        m_new = jnp.maximum(m_prev, mc_buf[slot])
        alpha = jnp.exp2(m_prev - m_new)
        p = jnp.exp2(s_buf[slot] - m_new)
        l_sc[h] = alpha * l_sc[h] + jnp.sum(p, axis=0, keepdims=True)
        p_buf[slot] = p.astype(BF16)
        al_buf[slot] = alpha
        m_sc[h] = m_new

    def values(n, slot):
        h, off, _ = split(n)
        vt = v_ref[0, pr, :, pl.ds(off, TK)]
        pv = jnp.dot(vt, p_buf[slot], preferred_element_type=F32)
        acc_sc[h] = al_buf[slot] * acc_sc[h] + pv

    m_sc[...] = jnp.full(m_sc.shape, -jnp.inf, F32)
    l_sc[...] = jnp.zeros(l_sc.shape, F32)
    acc_sc[...] = jnp.zeros(acc_sc.shape, F32)

    scores(0, 0)
    scores(1, 1)
    softmax(0, 0)

    def body(i, carry):
        n = 2 * i
        scores(n + 2, 0)
        softmax(n + 1, 1)
        values(n, 0)
        scores(n + 3, 1)
        softmax(n + 2, 0)
        values(n + 1, 1)
        return carry

    lax.fori_loop(0, (nblk - 2) // 2, body, 0)
    softmax(nblk - 1, 1)
    values(nblk - 2, 0)
    values(nblk - 1, 1)

    row = lax.broadcasted_iota(jnp.int32, acc_sc.shape[1:], 0)
    out_t = jnp.where(row < V_DIM, acc_sc[0] / l_sc[0], acc_sc[1] / l_sc[1])
    out_t = _head_norm_t(out_t, hg_ref[2 * V_DIM * pr:2 * V_DIM * (pr + 1), :])
    o_ref[0, :, 2 * V_DIM * pr:2 * V_DIM * (pr + 1)] = out_t.T


def _attn_kernel(q_ref, k_ref, v_ref, hg_ref, o_ref, *scratch):
    seq = k_ref.shape[2]
    bads = []
    for pr in range(PAIRS_PER_STEP):
        outs = []
        bad = None
        for h in range(2):
            qrow = (2 * pr + h) * HEAD_PAD
            qt = q_ref[0, qrow:qrow + HEAD_PAD, :]
            s0 = jnp.dot(k_ref[0, 2 * pr + h, 0:REF_ROWS, :], qt, preferred_element_type=F32)
            m_ref = jnp.max(s0, axis=0, keepdims=True)
            l = acc = None
            for j in range(seq // TK):
                ks = slice(j * TK, (j + 1) * TK)
                s = jnp.dot(k_ref[0, 2 * pr + h, ks, :], qt, preferred_element_type=F32)
                p = jnp.exp2(s - m_ref)
                p_sum = jnp.sum(p, axis=0, keepdims=True)
                pv = jnp.dot(v_ref[0, pr, :, ks], p.astype(BF16), preferred_element_type=F32)
                l = p_sum if l is None else l + p_sum
                acc = pv if acc is None else acc + pv
            out = acc / l
            outs.append(out)
            b_h = jnp.maximum(jnp.max(jnp.where(l < 2.0 ** ATT_SAFE_EXP, 0.0, 1.0)),
                              jnp.max(jnp.where(jnp.abs(out) < F32_HUGE, 0.0, 1.0)))
            bad = b_h if bad is None else jnp.maximum(bad, b_h)
        row = lax.broadcasted_iota(jnp.int32, outs[0].shape, 0)
        out_t = _head_norm_t(jnp.where(row < V_DIM, outs[0], outs[1]),
                             hg_ref[2 * V_DIM * pr:2 * V_DIM * (pr + 1), :])
        o_ref[0, :, 2 * V_DIM * pr:2 * V_DIM * (pr + 1)] = out_t.T
        bads.append(bad)

    for pr in range(PAIRS_PER_STEP):
        pl.when(bads[pr] > 0.0)(
            functools.partial(_attn_robust, pr, q_ref, k_ref, v_ref, hg_ref, o_ref, *scratch))


def _attention(qt, k4, vt, hg_col):
    b, _, seq = qt.shape
    pps = PAIRS_PER_STEP
    return pl.pallas_call(
        _attn_kernel, name="attn",
        grid=(b, MLA_HEADS // (2 * pps), seq // TQ),
        in_specs=[pl.BlockSpec((1, 2 * pps * HEAD_PAD, TQ), lambda bi, j, qi: (bi, j, qi)),
                  pl.BlockSpec((1, 2 * pps, seq, HEAD_PAD), lambda bi, j, qi: (bi, j, 0, 0)),
                  pl.BlockSpec((1, pps, 2 * V_DIM, seq), lambda bi, j, qi: (bi, j, 0, 0)),
                  pl.BlockSpec((2 * pps * V_DIM, 1), lambda bi, j, qi: (j, 0))],
        out_specs=pl.BlockSpec((1, TQ, 2 * pps * V_DIM), lambda bi, j, qi: (bi, qi, j)),
        out_shape=jax.ShapeDtypeStruct((b, seq, MLA_HEADS * V_DIM), F32),
        scratch_shapes=[pltpu.VMEM((ATT_SLOTS, TK, TQ), F32), pltpu.VMEM((ATT_SLOTS, TK, TQ), BF16),
                        pltpu.VMEM((ATT_SLOTS, 1, TQ), F32), pltpu.VMEM((ATT_SLOTS, 1, TQ), F32),
                        pltpu.VMEM((2, 1, TQ), F32), pltpu.VMEM((2, 1, TQ), F32),
                        pltpu.VMEM((2, 2 * V_DIM, TQ), F32)],
        compiler_params=_cparams(("parallel", "parallel", "parallel")),
    )(qt, k4, vt, hg_col)


def _filter_kernel(bands_ref, w1t_ref, w1c_ref, w1s_ref, b1_ref, w2_ref, b2_ref, w3_ref,
                   fr_ref, dl_ref, kk_ref, asum_ref, *, seq):
    i = pl.program_id(0)
    hp = lax.Precision.HIGHEST
    l_idx = i * TL_FILT + lax.broadcasted_iota(jnp.int32, (1, TL_FILT), 1)
    pos_i = jnp.where(l_idx < seq, seq - l_idx, l_idx - seq)
    pos = pos_i.astype(F32)
    t = pos / float(max(seq - 1, 1))
    ang = (2.0 * math.pi * pos / seq) * bands_ref[...]
    pre = (t * w1t_ref[...]
           + jnp.dot(w1c_ref[...], jnp.cos(ang), precision=hp, preferred_element_type=F32)
           + jnp.dot(w1s_ref[...], -jnp.sin(ang), precision=hp, preferred_element_type=F32))
    fr = fr_ref[...]
    h = jnp.sin(fr * (pre + b1_ref[...]))
    h = jnp.sin(fr * (jnp.dot(w2_ref[...], h, precision=hp, preferred_element_type=F32) + b2_ref[...]))
    h = jnp.dot(w3_ref[0], h.astype(BF16), preferred_element_type=F32)
    decay = jnp.exp(-t * dl_ref[...])
    kk = jnp.where(l_idx == 0, 0.0, h * decay)
    kk_ref[...] = kk

    @pl.when(i == 0)
    def _():
        asum_ref[...] = jnp.zeros(asum_ref.shape, F32)

    asum_ref[...] += jnp.sum(jnp.abs(kk), axis=1, keepdims=True)


def _filters(bands, w1t, w1c, w1s, b1, w2, b2, w3d, fr, dl, seq):
    nl = 2 * seq // TL_FILT
    half = seq // TL_FILT
    ncol = HY_ORDER * HY_C
    return pl.pallas_call(
        functools.partial(_filter_kernel, seq=seq), name="hyfilt",
        grid=(nl,),
        in_specs=[_full(bands.shape), _full(w1t.shape), _full(w1c.shape), _full(w1s.shape),
                  _full(b1.shape), _full(w2.shape), _full(b2.shape),
                  pl.BlockSpec((1, ncol, FILTER_HIDDEN), lambda i: (jnp.where(i < half, 1, 0), 0, 0)),
                  _full(fr.shape), _full(dl.shape)],
        out_specs=[pl.BlockSpec((ncol, TL_FILT), lambda i: (0, i)),
                   pl.BlockSpec((ncol, 1), lambda i: (0, 0))],
        out_shape=[jax.ShapeDtypeStruct((ncol, 2 * seq), F32),
                   jax.ShapeDtypeStruct((ncol, 1), F32)],
        compiler_params=_cparams(("arbitrary",)),
    )(bands, w1t, w1c, w1s, b1, w2, b2, w3d, fr, dl)


def _spec_kernel(kk_ref, asum_ref, f_ref, g_ref, *, lb):
    ct = kk_ref.shape[0]
    nhalf = kk_ref.shape[1] // lb
    kkn = (kk_ref[...] / asum_ref[...]).astype(BF16)
    x8 = jnp.concatenate([kkn[:, m * lb:(m + 1) * lb] for m in range(nhalf)], axis=0)
    hh = lax.dot_general(f_ref[...], x8, (((1,), (1,)), ((), ())),
                         preferred_element_type=F32)
    row = lax.broadcasted_iota(jnp.int32, (2 * lb, 1), 0)
    sgn = (1 - 2 * (row & 1)).astype(F32)
    low = (((row // RC) & 1) == 0) | (row == RC)
    for dd in range(nhalf - 1):
        h_neg = hh[0:2 * lb, dd * ct:(dd + 1) * ct]
        h_pos = hh[0:2 * lb, (dd + 1) * ct:(dd + 2) * ct]
        h0 = hh[2 * lb:2 * lb + 1, dd * ct:(dd + 1) * ct]
        g_ref[dd] = h_pos + sgn * (h_neg - jnp.where(low, h0, 0.0))


def _spectra(kk, asum, fmat_ext, lb):
    ncol = kk.shape[0]
    nd = 2 * P_BLK - 1
    return pl.pallas_call(
        functools.partial(_spec_kernel, lb=lb), name="hyspec",
        grid=(ncol // CT,),
        in_specs=[pl.BlockSpec((CT, kk.shape[1]), lambda c: (c, 0)),
                  pl.BlockSpec((CT, 1), lambda c: (c, 0)),
                  _full(fmat_ext.shape)],
        out_specs=pl.BlockSpec((nd, 2 * lb, CT), lambda c: (0, 0, c)),
        out_shape=jax.ShapeDtypeStruct((nd, 2 * lb, ncol), F32),
        compiler_params=_cparams(("parallel",)),
    )(kk, asum, fmat_ext)


def _hyena_kernel(a_ref, gt_ref, g_ref, d_ref, f_ref, fi_ref, o_ref, *, lb):
    ct = a_ref.shape[2]
    x4 = jnp.concatenate([a_ref[0, j * lb:(j + 1) * lb, :].astype(BF16) for j in range(P_BLK)],
                         axis=1)
    grp = 2 * lb // FWD_GROUPS
    us = [jnp.dot(f_ref[q * grp:(q + 1) * grp, :], x4, preferred_element_type=F32)
          for q in range(FWD_GROUPS)]
    first = lax.broadcasted_iota(jnp.int32, (RC, 1), 0) == 0
    y_rows = []
    for r in range(lb // RC):
        re0, im0 = 2 * RC * r, 2 * RC * r + RC
        u = us[re0 // grp]
        ure = [u[re0 % grp:re0 % grp + RC, j * ct:(j + 1) * ct] for j in range(P_BLK)]
        uim = [u[im0 % grp:im0 % grp + RC, j * ct:(j + 1) * ct] for j in range(P_BLK)]
        y_re, y_im = [], []
        for i in range(P_BLK):
            yre = yim = None
            dc = ny = None
            for j in range(P_BLK):
                dd = i - j + P_BLK - 1
                gre = g_ref[dd, re0:re0 + RC, :]
                gim = g_ref[dd, im0:im0 + RC, :]
                rr = ure[j] * gre
                ii = uim[j] * gim
                tre = rr - ii
                tim = ure[j] * gim + uim[j] * gre
                yre = tre if yre is None else yre + tre
                yim = tim if yim is None else yim + tim
                if r == 0:
                    dc = rr if dc is None else dc + rr
                    ny = ii if ny is None else ny + ii
            if r == 0:
                yre = jnp.where(first, dc, yre)
                yim = jnp.where(first, ny, yim)
            y_re.append(yre.astype(BF16))
            y_im.append(yim.astype(BF16))
        y_rows.append(jnp.concatenate(y_re, axis=1))
        y_rows.append(jnp.concatenate(y_im, axis=1))
    y = jnp.concatenate(y_rows, axis=0)
    y4 = jnp.dot(fi_ref[...], y, preferred_element_type=F32)
    for i in range(P_BLK):
        rs = slice(i * lb, (i + 1) * lb)
        o_ref[0, rs, :] = (y4[:, i * ct:(i + 1) * ct] + a_ref[0, rs, :] * d_ref[...]) * gt_ref[0, rs, :]


def _hyena_stage(a_arr, a_blk0, g_arr, g_blk0, gspec, g_blk0_spec, d, fmat, fimat, lb):
    b, seq, _ = a_arr.shape
    nct = HY_C // CT
    nd = 2 * P_BLK - 1
    return pl.pallas_call(
        functools.partial(_hyena_kernel, lb=lb), name="hyena",
        grid=(nct, b),
        in_specs=[pl.BlockSpec((1, seq, CT), lambda c, bi: (bi, 0, a_blk0 + c)),
                  pl.BlockSpec((1, seq, CT), lambda c, bi: (bi, 0, g_blk0 + c)),
                  pl.BlockSpec((nd, 2 * lb, CT), lambda c, bi: (0, 0, g_blk0_spec + c)),
                  pl.BlockSpec((1, CT), lambda c, bi: (0, c)),
                  _full(fmat.shape), _full(fimat.shape)],
        out_specs=pl.BlockSpec((1, seq, CT), lambda c, bi: (bi, 0, c)),
        out_shape=jax.ShapeDtypeStruct((b, seq, HY_C), F32),
        compiler_params=_cparams(("parallel", "parallel")),
    )(a_arr, g_arr, gspec, d, fmat, fimat)


def _group_norm_tile(xs, g):
    lane = lax.broadcasted_iota(jnp.int32, xs.shape, 1)
    lo = lane < GROUP
    sq = xs * xs
    s_lo = jnp.sum(jnp.where(lo, sq, 0.0), axis=-1, keepdims=True)
    s_hi = jnp.sum(jnp.where(lo, 0.0, sq), axis=-1, keepdims=True)
    r = jnp.where(lo, lax.rsqrt(s_lo / GROUP + EPS), lax.rsqrt(s_hi / GROUP + EPS))
    return xs * r * g


def _ffn2_kernel(x0_ref, a0_ref, h0_ref, xn_ref, an_ref, hn_ref, hg_ref, wo_ref, g_ref, wg_ref, wu_ref,
                 wd_ref, gf_ref, o_ref, x2_sc, x2n_sc):
    def prepare(x_ref, a_ref, h_ref):
        tiles = []
        n_a = a_ref.shape[1] // LANES
        for s in range(D_MODEL // LANES):
            if s < n_a:
                tiles.append(a_ref[:, s * LANES:(s + 1) * LANES].astype(BF16))
                continue
            src = h_ref[:, (s - n_a) * LANES:(s - n_a + 1) * LANES]
            tiles.append(_group_norm_tile(src, hg_ref[:, s * LANES:(s + 1) * LANES]).astype(BF16))
        mix = jnp.concatenate(tiles, axis=1)
        x2 = x_ref[...] + jnp.dot(mix, wo_ref[...], preferred_element_type=F32)
        return x2, _rms(x2, g_ref[...]).astype(BF16)

    @pl.when(pl.program_id(0) == 0)
    def _():
        x2_sc[...], x2n_sc[...] = prepare(x0_ref, a0_ref, h0_ref)

    y = _swiglu_residual(x2_sc[...], x2n_sc[...], wg_ref, wu_ref, wd_ref)
    o_ref[...] = _rms(y, gf_ref[...])
    x2_sc[...], x2n_sc[...] = prepare(xn_ref, an_ref, hn_ref)


def _ffn2(x1, a2d, h2d, hg, wo, g, wg, wu, wd, gf):
    t = x1.shape[0]
    n = t // TM_FFN
    first = lambda w: pl.BlockSpec((TM_FFN, w), lambda i: (0, 0))
    nxt = lambda w: pl.BlockSpec((TM_FFN, w), _next_tile(n))
    wa, wh = a2d.shape[1], h2d.shape[1]
    return pl.pallas_call(
        _ffn2_kernel, name="ffn2",
        grid=(n,),
        in_specs=[first(D_MODEL), first(wa), first(wh), nxt(D_MODEL), nxt(wa), nxt(wh),
                  _full(hg.shape), _full(wo.shape),
                  _full(g.shape), _full(wg.shape), _full(wu.shape), _full(wd.shape), _full(gf.shape)],
        out_specs=pl.BlockSpec((TM_FFN, D_MODEL), lambda i: (i, 0)),
        out_shape=jax.ShapeDtypeStruct(x1.shape, F32),
        scratch_shapes=[pltpu.VMEM((TM_FFN, D_MODEL), F32), pltpu.VMEM((TM_FFN, D_MODEL), BF16)],
        compiler_params=_cparams(("arbitrary",)),
    )(x1, a2d, h2d, x1, a2d, h2d, hg, wo, g, wg, wu, wd, gf)


def _rope_tables(seq):
    inv = 1.0 / (ROPE_THETA ** (jnp.arange(0, ROPE, 2, dtype=F32) / ROPE))
    ang = jnp.arange(seq, dtype=F32)[:, None] * inv[None, :]
    cos, sin = jnp.cos(ang), jnp.sin(ang)
    z64 = jnp.zeros((seq, NOPE), F32)
    z32 = jnp.zeros((seq, HEAD_PAD - NOPE - ROPE), F32)
    cq_t = jnp.concatenate([jnp.ones((seq, NOPE), F32), cos, cos, z32], axis=1)
    ck_t = jnp.concatenate([z64, cos, cos, z32], axis=1)
    sn_t = jnp.concatenate([z64, -sin, sin, z32], axis=1)
    return cq_t, ck_t, sn_t


def _dft_matrices(lb):
    n2 = 2 * lb
    nch = lb // RC
    two_pi = 2.0 * math.pi

    def tables(c, q, n):
        ang_c = (two_pi / (n2 // RC)) * ((c * n) % (n2 // RC)).astype(F32)
        ang_q = (two_pi / n2) * ((q * n) % n2).astype(F32)
        cc, sc, cq, sq = jnp.cos(ang_c), jnp.sin(ang_c), jnp.cos(ang_q), jnp.sin(ang_q)
        return cc * cq - sc * sq, sc * cq + cc * sq

    ar = lambda m: jnp.arange(m, dtype=jnp.int32)
    alt = lambda n: (1 - 2 * (n % 2)).astype(F32)
    c, q, n = ar(nch)[:, None, None], ar(RC)[None, :, None], ar(lb)[None, None, :]
    cos, sin = tables(c, q, n)
    f0 = (c == 0) & (q == 0)
    fwd = jnp.stack([cos, jnp.where(f0, alt(n), -sin)], axis=1).reshape(n2, lb)
    n, c, q = ar(lb)[:, None, None], ar(nch)[None, :, None], ar(RC)[None, None, :]
    cos, sin = tables(c, q, n)
    f0 = (c == 0) & (q == 0)
    inv_re = jnp.where(f0, 1.0 / n2, (2.0 / n2) * cos)
    inv_im = jnp.where(f0, alt(n) / n2, (-2.0 / n2) * sin)
    inv = jnp.stack([inv_re, inv_im], axis=2).reshape(lb, n2)
    return fwd.astype(BF16), inv.astype(BF16)


def kernel(x, ffn1_norm_g, ffn1_w_gate, ffn1_w_up, ffn1_w_down, mix_norm_g, w_in, q_norm_g, w_uq, kv_norm_g, w_ukv, hyena_conv_w, hyena_conv_b, filt_w1, filt_b1, filt_w2, filt_b2, filt_w3, filt_freq, hyena_d, head_norm_g, w_out, ffn2_norm_g, ffn2_w_gate, ffn2_w_up, ffn2_w_down, final_norm_g):
    b, seq, d = x.shape
    t = b * seq
    lb = seq // P_BLK
    l = 0
    row = lambda v: v.reshape(1, -1)

    x1 = _ffn1(x.reshape(t, d), row(ffn1_norm_g[l]), ffn1_w_gate[l].astype(BF16),
               ffn1_w_up[l].astype(BF16), ffn1_w_down[l].astype(BF16))

    wi = w_in[l]
    o_kr = Q_RANK + KV_RANK
    half = ROPE // 2
    zc = lambda n: jnp.zeros((d, n), F32)
    kr = wi[:, o_kr:o_kr + ROPE]
    win_p = jnp.concatenate([
        wi[:, :o_kr],
        zc(NOPE), kr, zc(HEAD_PAD - NOPE - ROPE),
        zc(NOPE), kr[:, half:], kr[:, :half], zc(HEAD_PAD - NOPE - ROPE),
        wi[:, o_kr + ROPE:]], axis=1).astype(BF16)
    wq3 = w_uq[l].reshape(Q_RANK, MLA_HEADS, NOPE + ROPE)
    zq = lambda n: jnp.zeros((Q_RANK, MLA_HEADS, n), F32)
    wq_p = jnp.concatenate([wq3, zq(HEAD_PAD - NOPE - ROPE)], axis=2).reshape(Q_RANK, -1).astype(BF16)
    wqs_p = jnp.concatenate([zq(NOPE), wq3[:, :, NOPE + half:], wq3[:, :, NOPE:NOPE + half],
                             zq(HEAD_PAD - NOPE - ROPE)], axis=2).reshape(Q_RANK, -1).astype(BF16)
    wkv3 = w_ukv[l].reshape(KV_RANK, MLA_HEADS, NOPE + V_DIM)
    wk_p = jnp.concatenate([wkv3[:, :, :NOPE], jnp.zeros((KV_RANK, MLA_HEADS, HEAD_PAD - NOPE), F32)],
                           axis=2).reshape(KV_RANK, -1).astype(BF16)
    wv_p = wkv3[:, :, NOPE:].reshape(KV_RANK, -1).astype(BF16)
    cq_t, ck_t, sn_t = _rope_tables(seq)
    qt, k4, vt, u3 = _inproj(x1.reshape(b, seq, d), row(mix_norm_g[l]), win_p,
                             hyena_conv_w[l], row(hyena_conv_b[l]), row(q_norm_g[l]),
                             wq_p.T, wqs_p.T, row(kv_norm_g[l]), wk_p, wv_p.T,
                             cq_t.T, sn_t.T, ck_t, sn_t)

    a = _attention(qt, k4, vt, head_norm_g[l][:MLA_HEADS * V_DIM].reshape(-1, 1))

    col = lambda v: v.astype(F32).reshape(-1, 1)
    bands = col(jnp.linspace(1e-4, FILTER_BANDS - 1, FILTER_BANDS, dtype=F32))
    w1 = filt_w1[l].astype(F32)
    w3d = filt_w3[l].astype(F32).reshape(FILTER_HIDDEN, HY_ORDER, 2, HY_C).transpose(2, 1, 3, 0)
    w3d = w3d.reshape(2, HY_ORDER * HY_C, FILTER_HIDDEN).astype(BF16)
    deltas = jnp.abs(jnp.linspace(MIN_DECAY, MAX_DECAY, HY_C, dtype=F32))
    dl = col(jnp.tile(deltas, HY_ORDER))
    kk, asum = _filters(bands, col(w1[0]), w1[1:1 + FILTER_BANDS].T, w1[1 + FILTER_BANDS:].T,
                        col(filt_b1[l]), filt_w2[l].astype(F32).T, col(filt_b2[l]),
                        w3d, col(filt_freq[l]), dl, seq)
    fmat, fimat = _dft_matrices(lb)
    lag0 = jnp.zeros((BF16_ROWS, lb), BF16).at[0, 0].set(1.0)
    gspec = _spectra(kk, asum, jnp.concatenate([fmat, lag0], axis=0), lb)

    nct = HY_C // CT
    dsk = hyena_d[l].astype(F32)
    z1 = _hyena_stage(u3, 0, u3, nct, gspec, 0, dsk[0:1], fmat, fimat, lb)
    hy = _hyena_stage(z1, 0, u3, 2 * nct, gspec, nct, dsk[1:2], fmat, fimat, lb)

    out = _ffn2(x1, a.reshape(t, -1), hy.reshape(t, -1), row(head_norm_g[l]), w_out[l].astype(BF16),
                row(ffn2_norm_g[l]), ffn2_w_gate[l].astype(BF16), ffn2_w_up[l].astype(BF16),
                ffn2_w_down[l].astype(BF16), row(final_norm_g))
    return out.reshape(b, seq, d)
```
